```python
import jax
import jax.numpy as jnp
from jax import lax
import numpy as np


D_MODEL = 1024
BATCH = 8
SEQ = 4096
DEPTH = 2

NSA_HEADS = 8
NSA_GROUPS = 2
NSA_HEADS_PER_GROUP = NSA_HEADS // NSA_GROUPS
HEAD_DIM = 64
NSA_WIDTH = NSA_HEADS * HEAD_DIM
NSA_KV = NSA_GROUPS * HEAD_DIM
CMP_BLOCK = 32
CMP_STRIDE = 16
CMP_HIDDEN = 128
SLC_BLOCK = 64
SLC_TOPK = 16
WINDOW = 512
NSA_QBLOCK = 64
GLA_HEADS = 4
GLA_KEY_HEAD = 64
GLA_VALUE_HEAD = 128
GLA_KEY_DIM = GLA_HEADS * GLA_KEY_HEAD
GLA_VALUE_DIM = GLA_HEADS * GLA_VALUE_HEAD
GLA_GATE_RANK = 16
GLA_TAU = 16.0
GLA_CHUNK = 64
ROPE_THETA = 10000.0
NORM_EPS = 1e-6
N_BRANCHES = 2
IN_SPLITS = (NSA_WIDTH, NSA_KV, NSA_KV, NSA_KV, NSA_KV, NSA_KV, NSA_KV, 3 * NSA_HEADS, NSA_WIDTH,
             GLA_KEY_DIM, GLA_KEY_DIM, GLA_VALUE_DIM, GLA_GATE_RANK, GLA_VALUE_DIM, N_BRANCHES * D_MODEL)
D_IN = (2 * NSA_WIDTH + 6 * NSA_KV + 3 * NSA_HEADS + 2 * GLA_KEY_DIM + 2 * GLA_VALUE_DIM
        + GLA_GATE_RANK + N_BRANCHES * D_MODEL)

kernel_name = 'nsa_gla_gated_hybrid_trunk'


def _rms(x, g):
    xf = x.astype(jnp.float32)
    y = xf * lax.rsqrt(jnp.mean(xf * xf, axis=-1, keepdims=True) + NORM_EPS)
    return (y * g.astype(jnp.float32)).astype(x.dtype)


def _rope(x, cos, sin):
    x1, x2 = jnp.split(x.astype(jnp.float32), 2, axis=-1)
    c = cos[None, :, None, :]
    s = sin[None, :, None, :]
    return jnp.concatenate([x1 * c - x2 * s, x2 * c + x1 * s], axis=-1).astype(x.dtype)


def _masked_softmax(s, mask):
    s = jnp.where(mask, s.astype(jnp.float32), -jnp.inf)
    m = jnp.max(s, axis=-1, keepdims=True)
    m = jnp.where(jnp.isfinite(m), m, 0.0)
    e = jnp.where(mask, jnp.exp(s - m), 0.0)
    return e / jnp.maximum(jnp.sum(e, axis=-1, keepdims=True), 1e-30)


def _compress(kraw, pos_emb, w1, w2):
    S = kraw.shape[1]
    ncmp = (S - CMP_BLOCK) // CMP_STRIDE + 1
    idx = CMP_STRIDE * jnp.arange(ncmp)[:, None] + jnp.arange(CMP_BLOCK)[None, :]
    blocks = kraw[:, idx] + pos_emb[None, None, :, None, :]
    hid = jax.nn.silu(jnp.einsum('bjlgd,ldh->bjgh', blocks, w1))
    return jnp.einsum('bjgh,hd->bgjd', hid, w2)


def _nsa_attention(q_rope, q_nope, k_cmp, v_cmp, k_slc, v_slc, k_win, v_win, gate):
    B, S, G, HG, d = q_rope.shape
    scale = d ** -0.5
    ncmp = k_cmp.shape[2]
    nblk = S // SLC_BLOCK
    topk = min(SLC_TOPK, nblk)
    nq = S // NSA_QBLOCK
    cs = CMP_STRIDE * jnp.arange(ncmp)
    cmp_end = cs + CMP_BLOCK - 1
    bs = SLC_BLOCK * jnp.arange(nblk)
    overlap = ((cs[:, None] < bs[None, :] + SLC_BLOCK) & (cs[:, None] + CMP_BLOCK > bs[None, :])).astype(jnp.float32)
    ks_blk = k_slc.reshape(B, nblk, SLC_BLOCK, G, d).transpose(0, 3, 1, 2, 4)
    vs_blk = v_slc.reshape(B, nblk, SLC_BLOCK, G, d).transpose(0, 3, 1, 2, 4)
    kw_pad = jnp.pad(k_win, ((0, 0), (WINDOW, 0), (0, 0), (0, 0)))
    vw_pad = jnp.pad(v_win, ((0, 0), (WINDOW, 0), (0, 0), (0, 0)))
    b_ix = jnp.arange(B)[:, None, None, None]
    g_ix = jnp.arange(G)[None, :, None, None]
    blk_ids = jnp.arange(nblk)

    def query_block(i):
        start = i * NSA_QBLOCK
        t = start + jnp.arange(NSA_QBLOCK)
        qr = lax.dynamic_slice_in_dim(q_rope, start, NSA_QBLOCK, axis=1)
        qn = lax.dynamic_slice_in_dim(q_nope, start, NSA_QBLOCK, axis=1)
        gt = lax.dynamic_slice_in_dim(gate, start, NSA_QBLOCK, axis=1)
        s_c = jnp.einsum('bqghd,bgjd->bghqj', qn, k_cmp) * scale
        p_c = _masked_softmax(s_c, cmp_end[None, :] <= t[:, None])
        o_c = jnp.einsum('bghqj,bgjd->bqghd', p_c.astype(v_cmp.dtype), v_cmp)
        imp = jnp.einsum('bghqj,jn->bgqn', p_c, overlap)
        cur = t // SLC_BLOCK
        forced = ((blk_ids[None, :] == 0) | (blk_ids[None, :] == cur[:, None])
                  | (blk_ids[None, :] == cur[:, None] - 1))
        valid = bs[None, :] <= t[:, None]
        imp = jnp.where(forced, jnp.inf, jnp.where(valid, imp, -jnp.inf))
        _, sel = lax.top_k(imp, topk)
        k_sel = ks_blk[b_ix, g_ix, sel]
        v_sel = vs_blk[b_ix, g_ix, sel]
        kpos = sel[..., None] * SLC_BLOCK + jnp.arange(SLC_BLOCK)
        m_s = (kpos <= t[None, None, :, None, None]).reshape(B, G, 1, NSA_QBLOCK, topk * SLC_BLOCK)
        s_s = jnp.einsum('bqghd,bgqkld->bghqkl', qr, k_sel).reshape(B, G, HG, NSA_QBLOCK, topk * SLC_BLOCK) * scale
        p_s = _masked_softmax(s_s, m_s).reshape(B, G, HG, NSA_QBLOCK, topk, SLC_BLOCK)
        o_s = jnp.einsum('bghqkl,bgqkld->bqghd', p_s.astype(v_sel.dtype), v_sel)
        kwin = lax.dynamic_slice_in_dim(kw_pad, start, WINDOW + NSA_QBLOCK, axis=1)
        vwin = lax.dynamic_slice_in_dim(vw_pad, start, WINDOW + NSA_QBLOCK, axis=1)
        kp = start - WINDOW + jnp.arange(WINDOW + NSA_QBLOCK)
        m_w = (kp[None, :] >= 0) & (kp[None, :] <= t[:, None]) & (kp[None, :] > t[:, None] - WINDOW)
        s_w = jnp.einsum('bqghd,bkgd->bghqk', qr, kwin) * scale
        p_w = _masked_softmax(s_w, m_w)
        o_w = jnp.einsum('bghqk,bkgd->bqghd', p_w.astype(vwin.dtype), vwin)
        return gt[..., 0:1] * o_c + gt[..., 1:2] * o_s + gt[..., 2:3] * o_w

    out = lax.map(query_block, jnp.arange(nq))
    return out.transpose(1, 0, 2, 3, 4, 5).reshape(B, S, G * HG * d)


def _gla(q, k, v, log_a):
    B, S, H, dk = q.shape
    dv = v.shape[-1]
    C = GLA_CHUNK
    nc = S // C
    f = lambda z: z.astype(jnp.float32).reshape(B, nc, C, H, -1).transpose(0, 3, 1, 2, 4)
    q = f(q) * dk ** -0.5
    k = f(k)
    v = f(v)
    b = jnp.cumsum(f(log_a), axis=3)
    b_last = b[:, :, :, -1:, :]
    qg = q * jnp.exp(b)
    kg = k * jnp.exp(-b)
    causal = jnp.tril(jnp.ones((C, C), dtype=bool))
    att = jnp.where(causal, jnp.einsum('bhnid,bhnjd->bhnij', qg, kg), 0.0)
    o_intra = jnp.einsum('bhnij,bhnjv->bhniv', att, v)
    upd = jnp.einsum('bhncd,bhncv->bhndv', k * jnp.exp(b_last - b), v)
    decay = jnp.exp(b_last[:, :, :, 0, :])

    def step(state, inp):
        dec, u = inp
        return dec[..., None] * state + u, state

    _, s_prev = lax.scan(step, jnp.zeros((B, H, dk, dv), jnp.float32),
                         (decay.transpose(2, 0, 1, 3), upd.transpose(2, 0, 1, 3, 4)))
    s_prev = s_prev.transpose(1, 2, 0, 3, 4)
    o = o_intra + jnp.einsum('bhncd,bhndv->bhncv', qg, s_prev)
    return o.transpose(0, 2, 3, 1, 4).reshape(B, S, H, dv)


def _layer(x, cos, sin, g_pre, w_in, cmp_pos_k, cmp_w1_k, cmp_w2_k, cmp_pos_v, cmp_w1_v, cmp_w2_v,
           gla_w_a, gla_b_a, gla_g_norm, w_up_nsa, w_up_gla, w_out, g_post):
    B, S, _ = x.shape
    G, HG, d = NSA_GROUPS, NSA_HEADS_PER_GROUP, HEAD_DIM
    h = _rms(x, g_pre)
    proj = h @ w_in
    offs = np.cumsum(IN_SPLITS)[:-1].tolist()
    (p_q, p_kc, p_vc, p_ks, p_vs, p_kw, p_vw, p_ng, p_nz,
     p_gq, p_gk, p_gv, p_ga, p_gr, p_mg) = jnp.split(proj, offs, axis=-1)
    q = p_q.reshape(B, S, NSA_HEADS, d)
    q_rope = _rope(q, cos, sin).reshape(B, S, G, HG, d)
    q_nope = q.reshape(B, S, G, HG, d)
    k_cmp = _compress(p_kc.reshape(B, S, G, d), cmp_pos_k, cmp_w1_k, cmp_w2_k)
    v_cmp = _compress(p_vc.reshape(B, S, G, d), cmp_pos_v, cmp_w1_v, cmp_w2_v)
    k_slc = _rope(p_ks.reshape(B, S, G, d), cos, sin)
    v_slc = p_vs.reshape(B, S, G, d)
    k_win = _rope(p_kw.reshape(B, S, G, d), cos, sin)
    v_win = p_vw.reshape(B, S, G, d)
    nsa_gate = jax.nn.sigmoid(p_ng).reshape(B, S, G, HG, 3)
    o_nsa = _nsa_attention(q_rope, q_nope, k_cmp, v_cmp, k_slc, v_slc, k_win, v_win, nsa_gate)
    o_nsa = o_nsa * jax.nn.silu(p_nz)
    gq = p_gq.reshape(B, S, GLA_HEADS, GLA_KEY_HEAD)
    gk = p_gk.reshape(B, S, GLA_HEADS, GLA_KEY_HEAD)
    gv = p_gv.reshape(B, S, GLA_HEADS, GLA_VALUE_HEAD)
    a_logit = (p_ga @ gla_w_a + gla_b_a).astype(jnp.float32)
    log_a = (jax.nn.log_sigmoid(a_logit) / GLA_TAU).reshape(B, S, GLA_HEADS, GLA_KEY_HEAD)
    o_gla = _rms(_gla(gq, gk, gv, log_a), gla_g_norm).astype(x.dtype)
    o_gla = (o_gla * jax.nn.silu(p_gr.reshape(B, S, GLA_HEADS, GLA_VALUE_HEAD))).reshape(B, S, GLA_VALUE_DIM)
    m_nsa, m_gla = jnp.split(jax.nn.sigmoid(p_mg), N_BRANCHES, axis=-1)
    y = m_nsa * (o_nsa @ w_up_nsa) + m_gla * (o_gla @ w_up_gla)
    out = y @ w_out
    return x + _rms(out, g_post)


def setup_inputs(seed: int = 0) -> dict:
    key = jax.random.key(seed)
    ks = jax.random.split(key, 16)
    L, D = DEPTH, D_MODEL
    nrm = lambda k, shape, fan_in: jax.random.normal(k, shape, jnp.float32) * fan_in ** -0.5
    small = lambda k, shape, s: s * jax.random.normal(k, shape, jnp.float32)
    return {
        'x': jax.random.normal(ks[0], (BATCH, SEQ, D), jnp.float32),
        'g_pre': 1.0 + small(ks[1], (L, D), 0.01),
        'w_in': nrm(ks[2], (L, D, D_IN), D),
        'cmp_pos_k': small(ks[3], (L, CMP_BLOCK, HEAD_DIM), 0.02),
        'cmp_w1_k': nrm(ks[4], (L, CMP_BLOCK, HEAD_DIM, CMP_HIDDEN), CMP_BLOCK * HEAD_DIM),
        'cmp_w2_k': nrm(ks[5], (L, CMP_HIDDEN, HEAD_DIM), CMP_HIDDEN),
        'cmp_pos_v': small(ks[6], (L, CMP_BLOCK, HEAD_DIM), 0.02),
        'cmp_w1_v': nrm(ks[7], (L, CMP_BLOCK, HEAD_DIM, CMP_HIDDEN), CMP_BLOCK * HEAD_DIM),
        'cmp_w2_v': nrm(ks[8], (L, CMP_HIDDEN, HEAD_DIM), CMP_HIDDEN),
        'gla_w_a': nrm(ks[9], (L, GLA_GATE_RANK, GLA_KEY_DIM), GLA_GATE_RANK),
        'gla_b_a': small(ks[10], (L, GLA_KEY_DIM), 0.1),
        'gla_g_norm': 1.0 + small(ks[11], (L, GLA_VALUE_HEAD), 0.01),
        'w_up_nsa': nrm(ks[12], (L, NSA_WIDTH, D), NSA_WIDTH),
        'w_up_gla': nrm(ks[13], (L, GLA_VALUE_DIM, D), GLA_VALUE_DIM),
        'w_out': nrm(ks[14], (L, D, D), D),
        'g_post': 1.0 + small(ks[15], (L, D), 0.01),
    }


def reference(x, g_pre, w_in, cmp_pos_k, cmp_w1_k, cmp_w2_k, cmp_pos_v, cmp_w1_v, cmp_w2_v,
              gla_w_a, gla_b_a, gla_g_norm, w_up_nsa, w_up_gla, w_out, g_post):
    S = x.shape[1]
    pos = jnp.arange(S, dtype=jnp.float32)
    inv_freq = ROPE_THETA ** (-jnp.arange(0, HEAD_DIM, 2, dtype=jnp.float32) / HEAD_DIM)
    ang = pos[:, None] * inv_freq[None, :]
    cos, sin = jnp.cos(ang), jnp.sin(ang)
    for l in range(DEPTH):
        x = _layer(x, cos, sin, g_pre[l], w_in[l], cmp_pos_k[l], cmp_w1_k[l], cmp_w2_k[l],
                   cmp_pos_v[l], cmp_w1_v[l], cmp_w2_v[l], gla_w_a[l], gla_b_a[l], gla_g_norm[l],
                   w_up_nsa[l], w_up_gla[l], w_out[l], g_post[l])
    return x
```

```python
import functools

import numpy as np
import jax
import jax.numpy as jnp
from jax import lax
from jax.experimental import pallas as pl
from jax.experimental.pallas import tpu as pltpu

F32 = jnp.float32
BF16 = jnp.bfloat16

D_MODEL = 1024
NSA_HEADS = 8
NSA_GROUPS = 2
HPG = NSA_HEADS // NSA_GROUPS
HEAD_DIM = 64
CMP_BLOCK = 32
CMP_STRIDE = 16
CMP_HIDDEN = 128
SLC_BLOCK = 64
SLC_TOPK = 16
WINDOW = 512
GLA_HEADS = 4
GLA_DK = 64
GLA_DV = 128
GLA_RANK = 16
GLA_TAU = 16.0
GLA_CHUNK = 64
ROPE_THETA = 10000.0
NORM_EPS = 1e-6
QK_SCALE = HEAD_DIM ** -0.5
GLA_SCALE = GLA_DK ** -0.5

LANES = 128
MASK_BIG = 1e30
VMEM_LIMIT = 56 * 1024 * 1024

NT_DIMS = (((1,), (1,)), ((), ()))
TN_DIMS = (((0,), (0,)), ((), ()))

_SPLITS = (("q", 512), ("kc", 128), ("vc", 128), ("ks", 128), ("vs", 128), ("kw", 128), ("vw", 128),
           ("ng", 24), ("nz", 512), ("gq", 256), ("gk", 256), ("gv", 512), ("ga", 16), ("gr", 512),
           ("mg", 2048))
_OFF = {}
_o = 0
for _n, _s in _SPLITS:
    _OFF[_n] = _o
    _o += _s
D_IN = _o
_ZCOL = D_IN


def _half_chunks(base, n):
    out = []
    for c in range(n):
        out += list(range(base + c * 64, base + (c + 1) * 64)) + [_ZCOL] * 64
    return out


def _proj_layout():
    segs = []
    segs.append(("q", _half_chunks(_OFF["q"], NSA_HEADS)))
    segs.append(("kcvc", list(range(_OFF["kc"], _OFF["kc"] + 256))))
    for nm in ("ks", "kw", "vs", "vw"):
        segs.append((nm, _half_chunks(_OFF[nm], NSA_GROUPS)))
    ng = []
    for g in range(NSA_GROUPS):
        cols = [_OFF["ng"] + g * 3 * HPG + h * 3 + br for br in range(3) for h in range(HPG)]
        ng += cols + [_ZCOL] * (LANES - len(cols))
    segs.append(("ng", ng))
    segs.append(("nz", list(range(_OFF["nz"], _OFF["nz"] + 512))))
    segs.append(("gq", _half_chunks(_OFF["gq"], GLA_HEADS)))
    segs.append(("gk", _half_chunks(_OFF["gk"], GLA_HEADS)))
    segs.append(("gv", list(range(_OFF["gv"], _OFF["gv"] + 512))))
    segs.append(("ga", list(range(_OFF["ga"], _OFF["ga"] + GLA_RANK)) + [_ZCOL] * (LANES - GLA_RANK)))
    segs.append(("gr", list(range(_OFF["gr"], _OFF["gr"] + 512))))
    segs.append(("mg", list(range(_OFF["mg"], _OFF["mg"] + 2048))))
    idx, pofs, o = [], {}, 0
    for nm, cols in segs:
        assert len(cols) % LANES == 0
        pofs[nm] = (o, len(cols))
        idx += cols
        o += len(cols)
    return np.asarray(idx, np.int32), pofs


_PIDX, _POFS = _proj_layout()
NP_COLS = int(_PIDX.shape[0])


def _sigmoid(v):
    return 1.0 / (1.0 + jnp.exp(-v))


def _proj_kernel(x_ref, g_ref, w_ref, cos_ref, sin_ref, noh_ref,
                 qn_ref, qr_ref, kcvc_ref, ksa_ref, kwr_ref, vs_ref, vw_ref, gate_ref, nzs_ref,
                 gq_ref, gk_ref, gv_ref, ga_ref, gr_ref, mgs_ref):
    x = x_ref[...]
    y = x * lax.rsqrt(jnp.mean(x * x, axis=-1, keepdims=True) + NORM_EPS)
    h = (y * g_ref[...]).astype(BF16)
    cos = cos_ref[...]
    sin = sin_ref[...]
    lane = lax.broadcasted_iota(jnp.int32, cos.shape, 1)
    lo_half = (lane & (HEAD_DIM - 1)) < (HEAD_DIM // 2)

    def rope(v):
        rot = jnp.where(lo_half, pltpu.roll(v, LANES - HEAD_DIM // 2, 1), pltpu.roll(v, HEAD_DIM // 2, 1))
        return v * cos + rot * sin

    def mm(name, c0, w):
        s = _POFS[name][0] + c0
        return jnp.dot(h, w_ref[:, s:s + w], preferred_element_type=F32)

    def chunks(name):
        width = _POFS[name][1]
        step = 256 if width % 256 == 0 else LANES
        for c0 in range(0, width, step):
            yield c0, step, mm(name, c0, step)

    for c0, w, v in chunks("q"):
        for k in range(w // LANES):
            vk = v[:, k * LANES:(k + 1) * LANES]
            sl = slice(c0 + k * LANES, c0 + (k + 1) * LANES)
            qn_ref[:, sl] = (vk * QK_SCALE).astype(BF16)
            qr_ref[:, sl] = (rope(vk) * QK_SCALE).astype(BF16)
    for c0, w, v in chunks("kcvc"):
        kcvc_ref[:, c0:c0 + w] = v
    noh = noh_ref[...]
    for c0, w, v in chunks("ks"):
        for k in range(w // LANES):
            sl = slice(c0 + k * LANES, c0 + (k + 1) * LANES)
            ksa_ref[:, sl] = (rope(v[:, k * LANES:(k + 1) * LANES]) + noh).astype(BF16)
    for c0, w, v in chunks("kw"):
        for k in range(w // LANES):
            sl = slice(c0 + k * LANES, c0 + (k + 1) * LANES)
            kwr_ref[:, sl] = rope(v[:, k * LANES:(k + 1) * LANES]).astype(BF16)
    for c0, w, v in chunks("vs"):
        vs_ref[:, c0:c0 + w] = v.astype(BF16)
    for c0, w, v in chunks("vw"):
        vw_ref[:, c0:c0 + w] = v.astype(BF16)
    for c0, w, v in chunks("ng"):
        gate_ref[:, c0:c0 + w] = _sigmoid(v)
    for c0, w, v in chunks("nz"):
        nzs_ref[:, c0:c0 + w] = v * _sigmoid(v)
    for name, ref in (("gq", gq_ref), ("gk", gk_ref), ("gv", gv_ref), ("ga", ga_ref), ("gr", gr_ref)):
        for c0, w, v in chunks(name):
            ref[:, c0:c0 + w] = v
    for c0, w, v in chunks("mg"):
        mgs_ref[:, c0:c0 + w] = _sigmoid(v)


def _proj_call(xf, g_pre, w_p, cosf, sinf, noh, S, tm):
    n, d = xf.shape
    nt = n // tm
    spt = S // tm
    row = lambda width: pl.BlockSpec((tm, width), lambda i: (i, 0))
    pos = pl.BlockSpec((tm, LANES), lambda i: (i % spt, 0))
    outs = [("qn", 1024, BF16), ("qr", 1024, BF16), ("kcvc", 256, F32), ("ksa", 256, BF16),
            ("kwr", 256, BF16), ("vs", 256, BF16), ("vw", 256, BF16), ("gate", 256, F32),
            ("nzs", 512, F32), ("gq", 512, F32), ("gk", 512, F32), ("gv", 512, F32),
            ("ga", 128, F32), ("gr", 512, F32), ("mgs", 2048, F32)]
    res = pl.pallas_call(
        _proj_kernel,
        grid=(nt,),
        in_specs=[row(d),
                  pl.BlockSpec((1, d), lambda i: (0, 0)),
                  pl.BlockSpec((d, NP_COLS), lambda i: (0, 0), pipeline_mode=pl.Buffered(1)),
                  pos, pos, pos],
        out_specs=[row(w) for _, w, _ in outs],
        out_shape=[jax.ShapeDtypeStruct((n, w), dt) for _, w, dt in outs],
        compiler_params=pltpu.CompilerParams(dimension_semantics=("parallel",),
                                             vmem_limit_bytes=VMEM_LIMIT),
        name="proj",
    )(xf, g_pre, w_p, cosf, sinf, noh)
    return {nm: r for (nm, _, _), r in zip(outs, res)}


def _cmp_kernel(r_ref, pa_ref, pb_ref, w1a_ref, w1b_ref, w2_ref, o_ref):
    r = r_ref[0]
    a = jnp.dot((r + pa_ref[...]).astype(BF16), w1a_ref[...], preferred_element_type=F32)
    b = jnp.dot((r + pb_ref[...]).astype(BF16), w1b_ref[...], preferred_element_type=F32)
    hid = a + pltpu.roll(b, b.shape[0] - 1, 0)
    hid = hid * _sigmoid(hid)
    o_ref[0] = jnp.dot(hid.astype(BF16), w2_ref[...], preferred_element_type=F32).astype(BF16)


def _cmp_call(r, pa, pb, w1a, w1b, w2):
    B, nc, k = r.shape
    full = lambda a: pl.BlockSpec(a.shape, lambda b: (0,) * a.ndim)
    return pl.pallas_call(
        _cmp_kernel,
        grid=(B,),
        in_specs=[pl.BlockSpec((1, nc, k), lambda b: (b, 0, 0)), full(pa), full(pb), full(w1a), full(w1b), full(w2)],
        out_specs=pl.BlockSpec((1, nc, w2.shape[1]), lambda b: (b, 0, 0)),
        out_shape=jax.ShapeDtypeStruct((B, nc, w2.shape[1]), BF16),
        compiler_params=pltpu.CompilerParams(dimension_semantics=("parallel",), vmem_limit_bytes=VMEM_LIMIT),
        name="compress",
    )(r, pa, pb, w1a, w1b, w2)


def _softmax_masked(s, mask):
    s = jnp.where(mask, s, -jnp.inf)
    m = jnp.max(s, axis=-1, keepdims=True)
    m = jnp.where(jnp.isfinite(m), m, 0.0)
    e = jnp.where(mask, jnp.exp(s - m), 0.0)
    return e * (1.0 / jnp.maximum(jnp.sum(e, axis=-1, keepdims=True), 1e-30))


def _nsa_kernel(qn_ref, qr_ref, gate_ref, kc_ref, vc_ref, ksa_ref, vs_ref, kw_ref, vw_ref, ovt_ref, o_ref,
                *, qt, nb, nc):
    i = pl.program_id(2)
    s0 = i * qt
    m_rows = HPG * qt

    def stack_heads(ref):
        return jnp.concatenate([ref[:, h * LANES:(h + 1) * LANES] for h in range(HPG)], axis=0)

    qn = stack_heads(qn_ref)
    qr = stack_heads(qr_ref)
    t_col = s0 + lax.broadcasted_iota(jnp.int32, (qt, 1), 0)

    s = lax.dot_general(qn, kc_ref[0], NT_DIMS, preferred_element_type=F32).reshape(HPG, qt, nc)
    jj = lax.broadcasted_iota(jnp.int32, (qt, nc), 1)
    cmask = (CMP_STRIDE * jj + (CMP_BLOCK - 1)) <= t_col
    p = _softmax_masked(s, cmask[None])
    o_c = jnp.dot(p.reshape(m_rows, nc).astype(BF16), vc_ref[0], preferred_element_type=F32)

    psum = p[0] + p[1] + p[2] + p[3]
    hi = psum.astype(BF16)
    r1 = psum - hi.astype(F32)
    mid = r1.astype(BF16)
    lo = (r1 - mid.astype(F32)).astype(BF16)
    ovt = ovt_ref[...]
    imp_t = (lax.dot_general(ovt, hi, NT_DIMS, preferred_element_type=F32)
             + lax.dot_general(ovt, mid, NT_DIMS, preferred_element_type=F32)
             + lax.dot_general(ovt, lo, NT_DIMS, preferred_element_type=F32))
    n_idx = lax.broadcasted_iota(jnp.int32, (nb, qt), 0)
    cur = (s0 + lax.broadcasted_iota(jnp.int32, (nb, qt), 1)) // SLC_BLOCK
    forced = (n_idx == 0) | (n_idx == cur) | (n_idx == cur - 1)
    valid = n_idx <= cur
    val = jnp.where(forced, jnp.inf, jnp.where(valid, imp_t, -jnp.inf))
    rank = jnp.zeros((nb, qt), F32)
    for mb in range(nb):
        row = val[mb:mb + 1, :]
        beats = (row > val) | ((row == val) & (n_idx > mb))
        rank = rank + jnp.where(beats, 1.0, 0.0)
    sel = (rank < float(min(SLC_TOPK, nb))) & valid
    notsel = jnp.where(sel, 0.0, 1.0)
    pieces = [jnp.zeros((HEAD_DIM, qt), F32), notsel]
    if nb < LANES - HEAD_DIM:
        pieces.append(jnp.zeros((LANES - HEAD_DIM - nb, qt), F32))
    ns = jnp.concatenate(pieces, axis=0).T.astype(BF16)
    q_aug = qr + jnp.concatenate([ns] * HPG, axis=0)

    def slc_step(j, carry, diag):
        m_i, l_i, acc = carry
        off = pl.multiple_of(j * qt, qt)
        k = ksa_ref[pl.ds(off, qt), :]
        v = vs_ref[pl.ds(off, qt), :]
        sc = lax.dot_general(q_aug, k, NT_DIMS, preferred_element_type=F32)
        if diag:
            kpos = j * qt + lax.broadcasted_iota(jnp.int32, (qt, qt), 1)
            sc = jnp.where((kpos <= t_col)[None], sc.reshape(HPG, qt, qt), -MASK_BIG).reshape(m_rows, qt)
        m_new = jnp.maximum(m_i, jnp.max(sc, axis=-1, keepdims=True))
        alpha = jnp.exp(m_i - m_new)
        pe = jnp.exp(sc - m_new)
        l_new = alpha * l_i + jnp.sum(pe, axis=-1, keepdims=True)
        acc_new = alpha * acc + jnp.dot(pe.astype(BF16), v, preferred_element_type=F32)
        return m_new, l_new, acc_new

    init = (jnp.full((m_rows, 1), -jnp.inf, F32), jnp.zeros((m_rows, 1), F32), jnp.zeros((m_rows, LANES), F32))
    carry = lax.fori_loop(0, i, lambda j, c: slc_step(j, c, False), init)
    _, l_i, acc = slc_step(i, carry, True)
    o_s = acc * (1.0 / l_i)

    ws = WINDOW + qt
    start = pl.multiple_of(jnp.maximum(s0 - WINDOW, 0), qt)
    kw = kw_ref[pl.ds(start, ws), :]
    vw = vw_ref[pl.ds(start, ws), :]
    sw = lax.dot_general(qr, kw, NT_DIMS, preferred_element_type=F32).reshape(HPG, qt, ws)
    kp = start + lax.broadcasted_iota(jnp.int32, (qt, ws), 1)
    wmask = (kp <= t_col) & (kp > t_col - WINDOW)
    pw = _softmax_masked(sw, wmask[None])
    o_w = jnp.dot(pw.reshape(m_rows, ws).astype(BF16), vw, preferred_element_type=F32)

    g = gate_ref[...]
    outs = []
    for h in range(HPG):
        sl = slice(h * qt, (h + 1) * qt)
        outs.append(g[:, h:h + 1] * o_c[sl] + g[:, HPG + h:HPG + h + 1] * o_s[sl]
                    + g[:, 2 * HPG + h:2 * HPG + h + 1] * o_w[sl])
    for c in range(HPG // 2):
        o_ref[:, c * LANES:(c + 1) * LANES] = outs[2 * c] + pltpu.roll(outs[2 * c + 1], HEAD_DIM, 1)


def _nsa_call(p, cmp, ovt, B, S, qt):
    nq = S // qt
    nb = S // SLC_BLOCK
    nc = S // CMP_STRIDE
    qspec = pl.BlockSpec((qt, HPG * LANES), lambda b, g, i: (b * nq + i, g))
    gspec = pl.BlockSpec((qt, LANES), lambda b, g, i: (b * nq + i, g))
    kvspec = pl.BlockSpec((S, LANES), lambda b, g, i: (b, g))
    kernel = functools.partial(_nsa_kernel, qt=qt, nb=nb, nc=nc)
    return pl.pallas_call(
        kernel,
        grid=(B, NSA_GROUPS, nq),
        in_specs=[qspec, qspec, gspec,
                  pl.BlockSpec((1, nc, LANES), lambda b, g, i: (b, 0, g)),
                  pl.BlockSpec((1, nc, LANES), lambda b, g, i: (b, 0, NSA_GROUPS + g)),
                  kvspec, kvspec, kvspec, kvspec,
                  pl.BlockSpec(ovt.shape, lambda b, g, i: (0, 0))],
        out_specs=pl.BlockSpec((qt, HPG * HEAD_DIM), lambda b, g, i: (b * nq + i, g)),
        out_shape=jax.ShapeDtypeStruct((B * S, NSA_HEADS * HEAD_DIM), F32),
        compiler_params=pltpu.CompilerParams(dimension_semantics=("parallel", "parallel", "arbitrary"),
                                             vmem_limit_bytes=VMEM_LIMIT),
        name="nsa",
    )(p["qn"], p["qr"], p["gate"], cmp, cmp, p["ksa"], p["vs"], p["kwr"], p["vw"], ovt)


def _gla_kernel(gq_ref, gk_ref, gv_ref, ga_ref, gr_ref, wa_ref, ba_ref, gn_ref, o_ref, st_ref, *, tc):
    @pl.when(pl.program_id(1) == 0)
    def _():
        st_ref[...] = jnp.zeros(st_ref.shape, F32)

    a = jnp.dot(ga_ref[...], wa_ref[...], preferred_element_type=F32,
                precision=lax.Precision.HIGHEST) + ba_ref[...]
    log_a = (jnp.minimum(a, 0.0) - jnp.log1p(jnp.exp(-jnp.abs(a)))) * (1.0 / GLA_TAU)
    c_sz = GLA_CHUNK
    r_i = lax.broadcasted_iota(jnp.int32, (c_sz, c_sz), 0)
    c_i = lax.broadcasted_iota(jnp.int32, (c_sz, c_sz), 1)
    causal = r_i >= c_i
    tril = jnp.where(causal, 1.0, 0.0)
    gn = gn_ref[...]
    for c in range(tc // c_sz):
        rows = slice(c * c_sz, (c + 1) * c_sz)
        bcum = jnp.dot(tril, log_a[rows], preferred_element_type=F32, precision=lax.Precision.HIGHEST)
        bl = bcum[c_sz - 1:c_sz]
        gk = gk_ref[rows, :]
        qg = (gq_ref[rows, :] * GLA_SCALE) * jnp.exp(bcum)
        kg = gk * jnp.exp(-bcum)
        kd = gk * jnp.exp(bl - bcum)
        dec = jnp.exp(bl)
        for h in range(GLA_HEADS):
            ls = slice(h * LANES, (h + 1) * LANES)
            qg_h = qg[:, ls].astype(BF16)
            v_h = gv_ref[rows, ls].astype(BF16)
            att = lax.dot_general(qg_h, kg[:, ls].astype(BF16), NT_DIMS, preferred_element_type=F32)
            att = jnp.where(causal, att, 0.0)
            st = st_ref[h]
            o = (jnp.dot(att.astype(BF16), v_h, preferred_element_type=F32)
                 + lax.dot_general(qg_h, st.astype(BF16), NT_DIMS, preferred_element_type=F32))
            st_ref[h] = st * dec[:, ls] + lax.dot_general(v_h, kd[:, ls].astype(BF16), TN_DIMS,
                                                          preferred_element_type=F32)
            y = o * lax.rsqrt(jnp.mean(o * o, axis=-1, keepdims=True) + NORM_EPS) * gn
            gr = gr_ref[rows, ls]
            o_ref[rows, ls] = y * (gr * _sigmoid(gr))


def _gla_call(p, wa, ba, gn, B, S, tc):
    nt = S // tc
    row = lambda w: pl.BlockSpec((tc, w), lambda b, t: (b * nt + t, 0))
    full = lambda a: pl.BlockSpec(a.shape, lambda b, t: (0,) * a.ndim)
    width = GLA_HEADS * LANES
    return pl.pallas_call(
        functools.partial(_gla_kernel, tc=tc),
        grid=(B, nt),
        in_specs=[row(width), row(width), row(width), row(LANES), row(width), full(wa), full(ba), full(gn)],
        out_specs=row(width),
        out_shape=jax.ShapeDtypeStruct((B * S, width), F32),
        scratch_shapes=[pltpu.VMEM((GLA_HEADS, GLA_DV, LANES), F32)],
        compiler_params=pltpu.CompilerParams(dimension_semantics=("parallel", "arbitrary"),
                                             vmem_limit_bytes=VMEM_LIMIT),
        name="gla",
    )(p["gq"], p["gk"], p["gv"], p["ga"], p["gr"], wa, ba, gn)


def _out_kernel(on_ref, nzs_ref, og_ref, mgs_ref, x_ref, wn_ref, wg_ref, wo_ref, gp_ref, o_ref):
    d = x_ref.shape[1]
    a = jnp.dot((on_ref[...] * nzs_ref[...]).astype(BF16), wn_ref[...], preferred_element_type=F32)
    b = jnp.dot(og_ref[...].astype(BF16), wg_ref[...], preferred_element_type=F32)
    y = mgs_ref[:, :d] * a + mgs_ref[:, d:] * b
    out = jnp.dot(y.astype(BF16), wo_ref[...], preferred_element_type=F32)
    r = out * lax.rsqrt(jnp.mean(out * out, axis=-1, keepdims=True) + NORM_EPS)
    o_ref[...] = x_ref[...] + r * gp_ref[...]


def _out_call(o_nsa, nzs, o_gla, mgs, xf, wn, wg, wo, gp, tm):
    n, d = xf.shape
    row = lambda w: pl.BlockSpec((tm, w), lambda i: (i, 0))
    full = lambda a: pl.BlockSpec(a.shape, lambda i: (0,) * a.ndim)
    return pl.pallas_call(
        _out_kernel,
        grid=(n // tm,),
        in_specs=[row(o_nsa.shape[1]), row(nzs.shape[1]), row(o_gla.shape[1]), row(mgs.shape[1]), row(d),
                  full(wn), full(wg), full(wo), full(gp)],
        out_specs=row(d),
        out_shape=jax.ShapeDtypeStruct((n, d), F32),
        compiler_params=pltpu.CompilerParams(dimension_semantics=("parallel",), vmem_limit_bytes=VMEM_LIMIT),
        name="merge_out",
    )(o_nsa, nzs, o_gla, mgs, xf, wn, wg, wo, gp)


def _position_tables(S):
    pos = jnp.arange(S, dtype=F32)
    inv_freq = ROPE_THETA ** (-jnp.arange(0, HEAD_DIM, 2, dtype=F32) / HEAD_DIM)
    ang = pos[:, None] * inv_freq[None, :]
    cos, sin = jnp.cos(ang), jnp.sin(ang)
    cosf = jnp.concatenate([cos] * 4, axis=1)
    sinf = jnp.concatenate([-sin, sin, -sin, sin], axis=1)
    blk = np.arange(S) // SLC_BLOCK
    noh = np.zeros((S, LANES), np.float32)
    noh[np.arange(S), HEAD_DIM + blk] = -MASK_BIG
    nb, nc = S // SLC_BLOCK, S // CMP_STRIDE
    cs = CMP_STRIDE * np.arange(nc)
    bs = SLC_BLOCK * np.arange(nb)
    ov = (cs[None, :] < bs[:, None] + SLC_BLOCK) & (cs[None, :] + CMP_BLOCK > bs[:, None])
    ov[:, (S - CMP_BLOCK) // CMP_STRIDE + 1:] = False
    return cosf, sinf, jnp.asarray(noh), jnp.asarray(ov.astype(np.float32)).astype(BF16)


def _compress_weights(pos_k, w1_k, w2_k, pos_v, w1_v, w2_v):
    half = CMP_BLOCK // 2
    w1a = jnp.zeros((half, 2, NSA_GROUPS, HEAD_DIM, 2, NSA_GROUPS, CMP_HIDDEN), F32)
    w1b = jnp.zeros_like(w1a)
    for wi, w1 in enumerate((w1_k, w1_v)):
        for g in range(NSA_GROUPS):
            w1a = w1a.at[:, wi, g, :, wi, g, :].set(w1[:half])
            w1b = w1b.at[:, wi, g, :, wi, g, :].set(w1[half:])
    kdim = half * 2 * NSA_GROUPS * HEAD_DIM
    hdim = 2 * NSA_GROUPS * CMP_HIDDEN
    w1a = w1a.reshape(kdim, hdim).astype(BF16)
    w1b = w1b.reshape(kdim, hdim).astype(BF16)
    w2 = jnp.zeros((2, NSA_GROUPS, CMP_HIDDEN, 2, NSA_GROUPS, LANES), F32)
    for wi, w2_ in enumerate((w2_k, w2_v)):
        for g in range(NSA_GROUPS):
            w2 = w2.at[wi, g, :, wi, g, :HEAD_DIM].set(w2_)
    w2 = w2.reshape(hdim, 2 * NSA_GROUPS * LANES).astype(BF16)

    def pos_row(sl):
        pk = jnp.broadcast_to(pos_k[sl][:, None, :], (half, NSA_GROUPS, HEAD_DIM))
        pv = jnp.broadcast_to(pos_v[sl][:, None, :], (half, NSA_GROUPS, HEAD_DIM))
        return jnp.stack([pk, pv], axis=1).reshape(1, kdim)

    return pos_row(slice(0, half)), pos_row(slice(half, CMP_BLOCK)), w1a, w1b, w2


def _layer(xf, tabs, B, S, g_pre, w_in, cmp_pos_k, cmp_w1_k, cmp_w2_k, cmp_pos_v, cmp_w1_v, cmp_w2_v,
           gla_w_a, gla_b_a, gla_g_norm, w_up_nsa, w_up_gla, w_out, g_post):
    cosf, sinf, noh, ovt = tabs
    d = xf.shape[1]
    w_ext = jnp.concatenate([w_in, jnp.zeros((d, 1), w_in.dtype)], axis=1)
    w_p = jnp.take(w_ext, jnp.asarray(_PIDX), axis=1).astype(BF16)
    p = _proj_call(xf, g_pre.reshape(1, d), w_p, cosf, sinf, noh, S, tm=256)

    pa, pb, w1a, w1b, w2 = _compress_weights(cmp_pos_k, cmp_w1_k, cmp_w2_k, cmp_pos_v, cmp_w1_v, cmp_w2_v)
    nc = S // CMP_STRIDE
    r = p["kcvc"].reshape(B, nc, CMP_STRIDE * p["kcvc"].shape[1])
    cmp = _cmp_call(r, pa, pb, w1a, w1b, w2)
    o_nsa = _nsa_call(p, cmp, ovt, B, S, qt=256)

    wa = jnp.zeros((LANES, GLA_HEADS, LANES), F32).at[:GLA_RANK, :, :GLA_DK].set(
        gla_w_a.reshape(GLA_RANK, GLA_HEADS, GLA_DK)).reshape(LANES, GLA_HEADS * LANES)
    ba = jnp.zeros((GLA_HEADS, LANES), F32).at[:, :GLA_DK].set(
        gla_b_a.reshape(GLA_HEADS, GLA_DK)).reshape(1, GLA_HEADS * LANES)
    o_gla = _gla_call(p, wa, ba, gla_g_norm.reshape(1, GLA_DV), B, S, tc=256)

    return _out_call(o_nsa, p["nzs"], o_gla, p["mgs"], xf, w_up_nsa.astype(BF16), w_up_gla.astype(BF16),
                     w_out.astype(BF16), g_post.reshape(1, d), tm=256)


def kernel(x, g_pre, w_in, cmp_pos_k, cmp_w1_k, cmp_w2_k, cmp_pos_v, cmp_w1_v, cmp_w2_v, gla_w_a, gla_b_a,
           gla_g_norm, w_up_nsa, w_up_gla, w_out, g_post):
    B, S, d = x.shape
    assert d == D_MODEL and S % 2048 == 0 and S // SLC_BLOCK <= LANES - HEAD_DIM
    tabs = _position_tables(S)
    xf = x.reshape(B * S, d)
    for l in range(g_pre.shape[0]):
        xf = _layer(xf, tabs, B, S, g_pre[l], w_in[l], cmp_pos_k[l], cmp_w1_k[l], cmp_w2_k[l],
                    cmp_pos_v[l], cmp_w1_v[l], cmp_w2_v[l], gla_w_a[l], gla_b_a[l], gla_g_norm[l],
                    w_up_nsa[l], w_up_gla[l], w_out[l], g_post[l])
    return xf.reshape(B, S, d)
```

```python
import functools

import numpy as np
import jax
import jax.numpy as jnp
from jax import lax
from jax.experimental import pallas as pl
from jax.experimental.pallas import tpu as pltpu

F32 = jnp.float32
BF16 = jnp.bfloat16

D_MODEL = 1024
NSA_HEADS = 8
NSA_GROUPS = 2
HPG = NSA_HEADS // NSA_GROUPS
HEAD_DIM = 64
CMP_BLOCK = 32
CMP_STRIDE = 16
CMP_HIDDEN = 128
SLC_BLOCK = 64
SLC_TOPK = 16
WINDOW = 512
GLA_HEADS = 4
GLA_DK = 64
GLA_DV = 128
GLA_RANK = 16
GLA_TAU = 16.0
GLA_CHUNK = 64
ROPE_THETA = 10000.0
NORM_EPS = 1e-6
QK_SCALE = HEAD_DIM ** -0.5
Q_PRESCALE = QK_SCALE * 1.4426950408889634
GLA_SCALE = GLA_DK ** -0.5
GATE_ROWS = 16
TILE = 256

LANES = 128
MASK_BIG = 1e30
VMEM_LIMIT = 56 * 1024 * 1024

NT_DIMS = (((1,), (1,)), ((), ()))
TN_DIMS = (((0,), (0,)), ((), ()))

_SPLITS = (("q", 512), ("kc", 128), ("vc", 128), ("ks", 128), ("vs", 128), ("kw", 128), ("vw", 128),
           ("ng", 24), ("nz", 512), ("gq", 256), ("gk", 256), ("gv", 512), ("ga", 16), ("gr", 512),
           ("mg", 2048))
_OFF = {}
_o = 0
for _n, _s in _SPLITS:
    _OFF[_n] = _o
    _o += _s
D_IN = _o
_ZCOL = D_IN


def _half_chunks(base, n):
    out = []
    for c in range(n):
        out += list(range(base + c * 64, base + (c + 1) * 64)) + [_ZCOL] * 64
    return out


def _proj_layout():
    segs = []
    segs.append(("q", _half_chunks(_OFF["q"], NSA_HEADS)))
    segs.append(("kcvc", list(range(_OFF["kc"], _OFF["kc"] + 256))))
    for nm in ("ks", "kw", "vs", "vw"):
        segs.append((nm, _half_chunks(_OFF[nm], NSA_GROUPS)))
    ng = []
    for g in range(NSA_GROUPS):
        cols = [_OFF["ng"] + g * 3 * HPG + h * 3 + br for br in range(3) for h in range(HPG)]
        ng += cols + [_ZCOL] * (LANES - len(cols))
    segs.append(("ng", ng))
    segs.append(("nz", list(range(_OFF["nz"], _OFF["nz"] + 512))))
    segs.append(("gq", _half_chunks(_OFF["gq"], GLA_HEADS)))
    segs.append(("gk", _half_chunks(_OFF["gk"], GLA_HEADS)))
    segs.append(("gv", list(range(_OFF["gv"], _OFF["gv"] + 512))))
    segs.append(("ga", list(range(_OFF["ga"], _OFF["ga"] + GLA_RANK)) + [_ZCOL] * (LANES - GLA_RANK)))
    segs.append(("gr", list(range(_OFF["gr"], _OFF["gr"] + 512))))
    segs.append(("mg", list(range(_OFF["mg"], _OFF["mg"] + 2048))))
    idx, pofs, o = [], {}, 0
    for nm, cols in segs:
        assert len(cols) % LANES == 0
        pofs[nm] = (o, len(cols))
        idx += cols
        o += len(cols)
    return np.asarray(idx, np.int32), pofs


_PIDX, _POFS = _proj_layout()
NP_COLS = int(_PIDX.shape[0])


def _sigmoid(v):
    return 1.0 / (1.0 + jnp.exp(-v))


def _proj_kernel(x_ref, g_ref, w_ref, cos_ref, sin_ref, noh_ref,
                 qn_ref, qr_ref, kcvc_ref, ksa_ref, kwr_ref, vst_ref, vwt_ref, gate_ref, nzs_ref,
                 gq_ref, gk_ref, gv_ref, ga_ref, gr_ref, mgs_ref):
    x = x_ref[...]
    y = x * lax.rsqrt(jnp.mean(x * x, axis=-1, keepdims=True) + NORM_EPS)
    h = (y * g_ref[...]).astype(BF16)
    cos = cos_ref[...]
    sin = sin_ref[...]
    lane = lax.broadcasted_iota(jnp.int32, cos.shape, 1)
    lo_half = (lane & (HEAD_DIM - 1)) < (HEAD_DIM // 2)

    def rope(v):
        rot = jnp.where(lo_half, pltpu.roll(v, LANES - HEAD_DIM // 2, 1), pltpu.roll(v, HEAD_DIM // 2, 1))
        return v * cos + rot * sin

    def mm(name, c0, w):
        s = _POFS[name][0] + c0
        return jnp.dot(h, w_ref[:, s:s + w], preferred_element_type=F32)

    def chunks(name):
        width = _POFS[name][1]
        step = 256 if width % 256 == 0 else LANES
        for c0 in range(0, width, step):
            yield c0, step, mm(name, c0, step)

    for c0, w, v in chunks("q"):
        for k in range(w // LANES):
            vk = v[:, k * LANES:(k + 1) * LANES]
            hd = (c0 + k * LANES) // LANES
            qn_ref[0, hd] = (vk * Q_PRESCALE).T.astype(BF16)
            qr_ref[0, hd] = (rope(vk) * Q_PRESCALE).T.astype(BF16)
    for c0, w, v in chunks("kcvc"):
        kcvc_ref[:, c0:c0 + w] = v
    noh = noh_ref[...]
    for c0, w, v in chunks("ks"):
        for k in range(w // LANES):
            sl = slice(c0 + k * LANES, c0 + (k + 1) * LANES)
            ksa_ref[:, sl] = (rope(v[:, k * LANES:(k + 1) * LANES]) + noh).astype(BF16)
    for c0, w, v in chunks("kw"):
        for k in range(w // LANES):
            sl = slice(c0 + k * LANES, c0 + (k + 1) * LANES)
            kwr_ref[:, sl] = rope(v[:, k * LANES:(k + 1) * LANES]).astype(BF16)
    ones_lane = jnp.where(lane == HEAD_DIM, 1.0, 0.0)
    for name, ref in (("vs", vst_ref), ("vw", vwt_ref)):
        for c0, w, v in chunks(name):
            for k in range(w // LANES):
                ref[(c0 + k * LANES) // LANES, 0] = (v[:, k * LANES:(k + 1) * LANES] + ones_lane).T.astype(BF16)
    for c0, w, v in chunks("ng"):
        for k in range(w // LANES):
            gt = _sigmoid(v[:, k * LANES:(k + 1) * LANES]).T
            gate_ref[0, (c0 + k * LANES) // LANES] = gt[:GATE_ROWS]
    for c0, w, v in chunks("nz"):
        nzs_ref[:, c0:c0 + w] = v * _sigmoid(v)
    for name, ref in (("gq", gq_ref), ("gk", gk_ref), ("gv", gv_ref), ("ga", ga_ref), ("gr", gr_ref)):
        for c0, w, v in chunks(name):
            ref[:, c0:c0 + w] = v
    for c0, w, v in chunks("mg"):
        mgs_ref[:, c0:c0 + w] = _sigmoid(v)


def _proj_call(xf, g_pre, w_p, cosf, sinf, noh, S, tm):
    n, d = xf.shape
    nt = n // tm
    spt = S // tm
    row = lambda width: pl.BlockSpec((tm, width), lambda i: (i, 0))
    pos = pl.BlockSpec((tm, LANES), lambda i: (i % spt, 0))
    qt_shape = ((nt, NSA_HEADS, LANES, tm), (1, NSA_HEADS, LANES, tm), lambda i: (i, 0, 0, 0))
    vt_shape = ((NSA_GROUPS, nt, LANES, tm), (NSA_GROUPS, 1, LANES, tm), lambda i: (0, i, 0, 0))
    gt_shape = ((nt, NSA_GROUPS, GATE_ROWS, tm), (1, NSA_GROUPS, GATE_ROWS, tm), lambda i: (i, 0, 0, 0))
    outs = [("qn", qt_shape, BF16), ("qr", qt_shape, BF16), ("kcvc", 256, F32), ("ksa", 256, BF16),
            ("kwr", 256, BF16), ("vst", vt_shape, BF16), ("vwt", vt_shape, BF16), ("gate", gt_shape, F32),
            ("nzs", 512, F32), ("gq", 512, F32), ("gk", 512, F32), ("gv", 512, F32),
            ("ga", 128, F32), ("gr", 512, F32), ("mgs", 2048, F32)]
    out_specs, out_shape = [], []
    for _, sh, dt in outs:
        if isinstance(sh, int):
            out_specs.append(row(sh))
            out_shape.append(jax.ShapeDtypeStruct((n, sh), dt))
        else:
            out_specs.append(pl.BlockSpec(sh[1], sh[2]))
            out_shape.append(jax.ShapeDtypeStruct(sh[0], dt))
    res = pl.pallas_call(
        _proj_kernel,
        grid=(nt,),
        in_specs=[row(d),
                  pl.BlockSpec((1, d), lambda i: (0, 0)),
                  pl.BlockSpec((d, NP_COLS), lambda i: (0, 0), pipeline_mode=pl.Buffered(1)),
                  pos, pos, pos],
        out_specs=out_specs,
        out_shape=out_shape,
        compiler_params=pltpu.CompilerParams(dimension_semantics=("parallel",),
                                             vmem_limit_bytes=VMEM_LIMIT),
        name="proj",
    )(xf, g_pre, w_p, cosf, sinf, noh)
    return {nm: r for (nm, _, _), r in zip(outs, res)}


def _cmp_kernel(r_ref, pa_ref, pb_ref, w1a_ref, w1b_ref, w2_ref, kc_ref, vct_ref):
    r = r_ref[0]
    a = jnp.dot((r + pa_ref[...]).astype(BF16), w1a_ref[...], preferred_element_type=F32)
    b = jnp.dot((r + pb_ref[...]).astype(BF16), w1b_ref[...], preferred_element_type=F32)
    hid = a + pltpu.roll(b, b.shape[0] - 1, 0)
    hid = hid * _sigmoid(hid)
    out = jnp.dot(hid.astype(BF16), w2_ref[...], preferred_element_type=F32)
    kw = NSA_GROUPS * LANES
    kc_ref[0] = out[:, :kw].astype(BF16)
    for g in range(NSA_GROUPS):
        vct_ref[0, g] = out[:, kw + g * LANES:kw + (g + 1) * LANES].T.astype(BF16)


def _cmp_call(r, pa, pb, w1a, w1b, w2):
    B, nc, k = r.shape
    full = lambda a: pl.BlockSpec(a.shape, lambda b: (0,) * a.ndim)
    kw = NSA_GROUPS * LANES
    return pl.pallas_call(
        _cmp_kernel,
        grid=(B,),
        in_specs=[pl.BlockSpec((1, nc, k), lambda b: (b, 0, 0)), full(pa), full(pb), full(w1a), full(w1b), full(w2)],
        out_specs=[pl.BlockSpec((1, nc, kw), lambda b: (b, 0, 0)),
                   pl.BlockSpec((1, NSA_GROUPS, LANES, nc), lambda b: (b, 0, 0, 0))],
        out_shape=[jax.ShapeDtypeStruct((B, nc, kw), BF16),
                   jax.ShapeDtypeStruct((B, NSA_GROUPS, LANES, nc), BF16)],
        compiler_params=pltpu.CompilerParams(dimension_semantics=("parallel",), vmem_limit_bytes=VMEM_LIMIT),
        name="compress",
    )(r, pa, pb, w1a, w1b, w2)


def _softmax_masked_t(s, mask):
    s = jnp.where(mask, s, -jnp.inf)
    m = jnp.max(s, axis=0, keepdims=True)
    m = jnp.where(jnp.isfinite(m), m, 0.0)
    e = jnp.where(mask, jnp.exp2(s - m), 0.0)
    return e * (1.0 / jnp.maximum(jnp.sum(e, axis=0, keepdims=True), 1e-30))


def _nsa_kernel(qn_ref, qr_ref, gate_ref, kc_ref, vct_ref, ksa_ref, vst_ref, kw_ref, vwt_ref, ovt_ref, o_ref,
                qa_sc, acc_sc, m_sc, out_sc, *, qt, nb, nc):
    i = pl.program_id(2)
    s0 = i * qt
    gate_row = lambda r: gate_ref[0, 0, r:r + 1, :]
    hd = HEAD_DIM

    jj = lax.broadcasted_iota(jnp.int32, (nc, qt), 0)
    tt = s0 + lax.broadcasted_iota(jnp.int32, (nc, qt), 1)
    cmask = (CMP_STRIDE * jj + (CMP_BLOCK - 1)) <= tt
    kc = kc_ref[0]
    vct = vct_ref[0, 0]
    has_key = (s0 + lax.broadcasted_iota(jnp.int32, (1, qt), 1)) >= (CMP_BLOCK - 1)
    psum = None
    s_all = [jnp.dot(kc, qn_ref[0, h], preferred_element_type=F32) for h in range(HPG)]
    for h in range(HPG):
        s = jnp.where(cmask, s_all[h], -MASK_BIG)
        e = jnp.exp2(s - jnp.max(s, axis=0, keepdims=True))
        inv = jnp.where(has_key, 1.0 / jnp.sum(e, axis=0, keepdims=True), 0.0)
        p = e * inv
        o = jnp.dot(vct, p.astype(BF16), preferred_element_type=F32)
        out_sc[h * hd:(h + 1) * hd, :] = gate_row(h) * o[:hd]
        psum = p if psum is None else psum + p

    hi = psum.astype(BF16)
    r1 = psum - hi.astype(F32)
    mid = r1.astype(BF16)
    lo = (r1 - mid.astype(F32)).astype(BF16)
    ovt = ovt_ref[...]
    imp_t = (jnp.dot(ovt, hi, preferred_element_type=F32) + jnp.dot(ovt, mid, preferred_element_type=F32)
             + jnp.dot(ovt, lo, preferred_element_type=F32))
    n_idx = lax.broadcasted_iota(jnp.int32, (nb, qt), 0)
    cur = (s0 + lax.broadcasted_iota(jnp.int32, (nb, qt), 1)) // SLC_BLOCK
    forced = (n_idx == 0) | (n_idx == cur) | (n_idx == cur - 1)
    valid = n_idx <= cur
    val = jnp.where(forced, jnp.inf, jnp.where(valid, imp_t, -jnp.inf))
    sub = 8
    ranks = []
    for gi in range(nb // sub):
        vg = val[gi * sub:(gi + 1) * sub]
        ng = gi * sub + lax.broadcasted_iota(jnp.int32, (sub, qt), 0)
        cnt = jnp.zeros((sub, qt), F32)
        for mb in range(nb):
            row = val[mb:mb + 1, :]
            if mb < gi * sub:
                beats = row >= vg
            elif mb >= (gi + 1) * sub:
                beats = row > vg
            else:
                beats = (row > vg) | ((row == vg) & (ng > mb))
            cnt = cnt + jnp.where(beats, 1.0, 0.0)
        ranks.append(cnt)
    rank = jnp.concatenate(ranks, axis=0)
    sel = (rank < float(min(SLC_TOPK, nb))) & valid
    notsel = jnp.where(sel, 0.0, 1.0).astype(BF16)
    if nb < LANES - hd:
        notsel = jnp.concatenate([notsel, jnp.zeros((LANES - hd - nb, qt), BF16)], axis=0)
    for h in range(HPG):
        qa_sc[h, :hd, :] = qr_ref[0, h, :hd, :]
        qa_sc[h, hd:, :] = notsel

    r_i = lax.broadcasted_iota(jnp.int32, (qt, qt), 0)
    c_i = lax.broadcasted_iota(jnp.int32, (qt, qt), 1)

    def reset():
        acc_sc[...] = jnp.zeros(acc_sc.shape, F32)
        m_sc[...] = jnp.full(m_sc.shape, -jnp.inf, F32)

    def scores(k, q_of, h):
        return jnp.dot(k, q_of(h), preferred_element_type=F32)

    def fold(h, s, vt, mask):
        if mask is not None:
            s = jnp.where(mask, s, -MASK_BIG)
        m_old = m_sc[h]
        m_new = jnp.maximum(m_old, jnp.max(s, axis=0, keepdims=True))
        alpha = jnp.exp2(m_old - m_new)
        pe = jnp.exp2(s - m_new).astype(BF16)
        acc_sc[h] = alpha * acc_sc[h] + jnp.dot(vt, pe, preferred_element_type=F32)
        m_sc[h] = m_new

    ahead = 4

    def tile(pending, k, vt, q_of, mask, k_next):
        pending = list(pending)
        nxt = []
        for h in range(HPG):
            if h + ahead < HPG:
                pending.append(scores(k, q_of, h + ahead))
            elif k_next is not None:
                nxt.append(scores(k_next, q_of, h + ahead - HPG))
            fold(h, pending[h], vt, mask)
        return tuple(nxt)

    def first_scores(k, q_of):
        return tuple(scores(k, q_of, h) for h in range(ahead))

    def finish(branch):
        for h in range(HPG):
            scale = gate_row(branch * HPG + h) * (1.0 / acc_sc[h, hd:hd + 1, :])
            out_sc[h * hd:(h + 1) * hd, :] += acc_sc[h, :hd, :] * scale

    def rows(j):
        return pl.ds(pl.multiple_of(j * qt, qt), qt)

    reset()
    qa_of = lambda h: qa_sc[h]
    k0 = ksa_ref[rows(0), :]

    def slc_body(j, carry):
        return tile(carry, ksa_ref[rows(j), :], vst_ref[0, j], qa_of, None, ksa_ref[rows(j + 1), :])

    carry = lax.fori_loop(0, i, slc_body, first_scores(k0, qa_of))
    tile(carry, ksa_ref[rows(i), :], vst_ref[0, i], qa_of, r_i <= c_i, None)
    finish(1)

    reset()
    qr_of = lambda h: qr_ref[0, h]
    d_i = r_i - c_i
    mask_a = d_i > jnp.where(i >= 2, 0, qt)
    mask_b = d_i > jnp.where(i >= 1, -qt, qt)
    ja = jnp.maximum(i - 2, 0)
    jb = jnp.maximum(i - 1, 0)
    ka, kb, kc_w = kw_ref[rows(ja), :], kw_ref[rows(jb), :], kw_ref[rows(i), :]
    carry = first_scores(ka, qr_of)
    carry = tile(carry, ka, vwt_ref[0, ja], qr_of, mask_a, kb)
    carry = tile(carry, kb, vwt_ref[0, jb], qr_of, mask_b, kc_w)
    tile(carry, kc_w, vwt_ref[0, i], qr_of, r_i <= c_i, None)
    finish(2)

    o_ref[...] = out_sc[...].T


def _nsa_call(p, kc, vct, ovt, B, S, qt):
    assert WINDOW == 2 * qt
    nq = S // qt
    nb = S // SLC_BLOCK
    nc = S // CMP_STRIDE
    qspec = pl.BlockSpec((1, HPG, LANES, qt), lambda b, g, i: (b * nq + i, g, 0, 0))
    kspec = pl.BlockSpec((S, LANES), lambda b, g, i: (b, g))
    vspec = pl.BlockSpec((1, nq, LANES, qt), lambda b, g, i: (g, b, 0, 0))
    kernel = functools.partial(_nsa_kernel, qt=qt, nb=nb, nc=nc)
    return pl.pallas_call(
        kernel,
        grid=(B, NSA_GROUPS, nq),
        in_specs=[qspec, qspec,
                  pl.BlockSpec((1, 1, GATE_ROWS, qt), lambda b, g, i: (b * nq + i, g, 0, 0)),
                  pl.BlockSpec((1, nc, LANES), lambda b, g, i: (b, 0, g)),
                  pl.BlockSpec((1, 1, LANES, nc), lambda b, g, i: (b, g, 0, 0)),
                  kspec, vspec, kspec, vspec,
                  pl.BlockSpec(ovt.shape, lambda b, g, i: (0, 0))],
        out_specs=pl.BlockSpec((qt, HPG * HEAD_DIM), lambda b, g, i: (b * nq + i, g)),
        out_shape=jax.ShapeDtypeStruct((B * S, NSA_HEADS * HEAD_DIM), F32),
        scratch_shapes=[pltpu.VMEM((HPG, LANES, qt), BF16),
                        pltpu.VMEM((HPG, LANES, qt), F32),
                        pltpu.VMEM((HPG, 1, qt), F32),
                        pltpu.VMEM((HPG * HEAD_DIM, qt), F32)],
        compiler_params=pltpu.CompilerParams(dimension_semantics=("parallel", "parallel", "arbitrary"),
                                             vmem_limit_bytes=VMEM_LIMIT),
        name="nsa",
    )(p["qn"], p["qr"], p["gate"], kc, vct, p["ksa"], p["vst"], p["kwr"], p["vwt"], ovt)


def _gla_kernel(gq_ref, gk_ref, gv_ref, ga_ref, gr_ref, wa_ref, ba_ref, gn_ref, o_ref, st_ref, *, tc):
    @pl.when(pl.program_id(1) == 0)
    def _():
        st_ref[...] = jnp.zeros(st_ref.shape, F32)

    a = jnp.dot(ga_ref[...], wa_ref[...], preferred_element_type=F32,
                precision=lax.Precision.HIGHEST) + ba_ref[...]
    log_a = (jnp.minimum(a, 0.0) - jnp.log1p(jnp.exp(-jnp.abs(a)))) * (1.0 / GLA_TAU)
    c_sz = GLA_CHUNK
    r_i = lax.broadcasted_iota(jnp.int32, (c_sz, c_sz), 0)
    c_i = lax.broadcasted_iota(jnp.int32, (c_sz, c_sz), 1)
    causal = r_i >= c_i
    tril = jnp.where(causal, 1.0, 0.0)
    gn = gn_ref[...]
    for c in range(tc // c_sz):
        rows = slice(c * c_sz, (c + 1) * c_sz)
        bcum = jnp.dot(tril, log_a[rows], preferred_element_type=F32, precision=lax.Precision.HIGHEST)
        bl = bcum[c_sz - 1:c_sz]
        gk = gk_ref[rows, :]
        qg = (gq_ref[rows, :] * GLA_SCALE) * jnp.exp(bcum)
        kg = gk * jnp.exp(-bcum)
        kd = gk * jnp.exp(bl - bcum)
        dec = jnp.exp(bl)
        for h in range(GLA_HEADS):
            ls = slice(h * LANES, (h + 1) * LANES)
            qg_h = qg[:, ls].astype(BF16)
            v_h = gv_ref[rows, ls].astype(BF16)
            att = lax.dot_general(qg_h, kg[:, ls].astype(BF16), NT_DIMS, preferred_element_type=F32)
            att = jnp.where(causal, att, 0.0)
            st = st_ref[h]
            o = (jnp.dot(att.astype(BF16), v_h, preferred_element_type=F32)
                 + lax.dot_general(qg_h, st.astype(BF16), NT_DIMS, preferred_element_type=F32))
            st_ref[h] = st * dec[:, ls] + lax.dot_general(v_h, kd[:, ls].astype(BF16), TN_DIMS,
                                                          preferred_element_type=F32)
            y = o * lax.rsqrt(jnp.mean(o * o, axis=-1, keepdims=True) + NORM_EPS) * gn
            gr = gr_ref[rows, ls]
            o_ref[rows, ls] = y * (gr * _sigmoid(gr))


def _gla_call(p, wa, ba, gn, B, S, tc):
    nt = S // tc
    row = lambda w: pl.BlockSpec((tc, w), lambda b, t: (b * nt + t, 0))
    full = lambda a: pl.BlockSpec(a.shape, lambda b, t: (0,) * a.ndim)
    width = GLA_HEADS * LANES
    return pl.pallas_call(
        functools.partial(_gla_kernel, tc=tc),
        grid=(B, nt),
        in_specs=[row(width), row(width), row(width), row(LANES), row(width), full(wa), full(ba), full(gn)],
        out_specs=row(width),
        out_shape=jax.ShapeDtypeStruct((B * S, width), F32),
        scratch_shapes=[pltpu.VMEM((GLA_HEADS, GLA_DV, LANES), F32)],
        compiler_params=pltpu.CompilerParams(dimension_semantics=("parallel", "arbitrary"),
                                             vmem_limit_bytes=VMEM_LIMIT),
        name="gla",
    )(p["gq"], p["gk"], p["gv"], p["ga"], p["gr"], wa, ba, gn)


def _out_kernel(on_ref, nzs_ref, og_ref, mgs_ref, x_ref, wn_ref, wg_ref, wo_ref, gp_ref, o_ref):
    d = x_ref.shape[1]
    a = jnp.dot((on_ref[...] * nzs_ref[...]).astype(BF16), wn_ref[...], preferred_element_type=F32)
    b = jnp.dot(og_ref[...].astype(BF16), wg_ref[...], preferred_element_type=F32)
    y = mgs_ref[:, :d] * a + mgs_ref[:, d:] * b
    out = jnp.dot(y.astype(BF16), wo_ref[...], preferred_element_type=F32)
    r = out * lax.rsqrt(jnp.mean(out * out, axis=-1, keepdims=True) + NORM_EPS)
    o_ref[...] = x_ref[...] + r * gp_ref[...]


def _out_call(o_nsa, nzs, o_gla, mgs, xf, wn, wg, wo, gp, tm):
    n, d = xf.shape
    row = lambda w: pl.BlockSpec((tm, w), lambda i: (i, 0))
    full = lambda a: pl.BlockSpec(a.shape, lambda i: (0,) * a.ndim)
    return pl.pallas_call(
        _out_kernel,
        grid=(n // tm,),
        in_specs=[row(o_nsa.shape[1]), row(nzs.shape[1]), row(o_gla.shape[1]), row(mgs.shape[1]), row(d),
                  full(wn), full(wg), full(wo), full(gp)],
        out_specs=row(d),
        out_shape=jax.ShapeDtypeStruct((n, d), F32),
        compiler_params=pltpu.CompilerParams(dimension_semantics=("parallel",), vmem_limit_bytes=VMEM_LIMIT),
        name="merge_out",
    )(o_nsa, nzs, o_gla, mgs, xf, wn, wg, wo, gp)


def _position_tables(S):
    pos = jnp.arange(S, dtype=F32)
    inv_freq = ROPE_THETA ** (-jnp.arange(0, HEAD_DIM, 2, dtype=F32) / HEAD_DIM)
    ang = pos[:, None] * inv_freq[None, :]
    cos, sin = jnp.cos(ang), jnp.sin(ang)
    cosf = jnp.concatenate([cos] * 4, axis=1)
    sinf = jnp.concatenate([-sin, sin, -sin, sin], axis=1)
    blk = np.arange(S) // SLC_BLOCK
    noh = np.zeros((S, LANES), np.float32)
    noh[np.arange(S), HEAD_DIM + blk] = -MASK_BIG
    nb, nc = S // SLC_BLOCK, S // CMP_STRIDE
    cs = CMP_STRIDE * np.arange(nc)
    bs = SLC_BLOCK * np.arange(nb)
    ov = (cs[None, :] < bs[:, None] + SLC_BLOCK) & (cs[None, :] + CMP_BLOCK > bs[:, None])
    ov[:, (S - CMP_BLOCK) // CMP_STRIDE + 1:] = False
    return cosf, sinf, jnp.asarray(noh), jnp.asarray(ov.astype(np.float32)).astype(BF16)


def _compress_weights(pos_k, w1_k, w2_k, pos_v, w1_v, w2_v):
    half = CMP_BLOCK // 2
    w1a = jnp.zeros((half, 2, NSA_GROUPS, HEAD_DIM, 2, NSA_GROUPS, CMP_HIDDEN), F32)
    w1b = jnp.zeros_like(w1a)
    for wi, w1 in enumerate((w1_k, w1_v)):
        for g in range(NSA_GROUPS):
            w1a = w1a.at[:, wi, g, :, wi, g, :].set(w1[:half])
            w1b = w1b.at[:, wi, g, :, wi, g, :].set(w1[half:])
    kdim = half * 2 * NSA_GROUPS * HEAD_DIM
    hdim = 2 * NSA_GROUPS * CMP_HIDDEN
    w1a = w1a.reshape(kdim, hdim).astype(BF16)
    w1b = w1b.reshape(kdim, hdim).astype(BF16)
    w2 = jnp.zeros((2, NSA_GROUPS, CMP_HIDDEN, 2, NSA_GROUPS, LANES), F32)
    for wi, w2_ in enumerate((w2_k, w2_v)):
        for g in range(NSA_GROUPS):
            w2 = w2.at[wi, g, :, wi, g, :HEAD_DIM].set(w2_)
    w2 = w2.reshape(hdim, 2 * NSA_GROUPS * LANES).astype(BF16)

    def pos_row(sl):
        pk = jnp.broadcast_to(pos_k[sl][:, None, :], (half, NSA_GROUPS, HEAD_DIM))
        pv = jnp.broadcast_to(pos_v[sl][:, None, :], (half, NSA_GROUPS, HEAD_DIM))
        return jnp.stack([pk, pv], axis=1).reshape(1, kdim)

    return pos_row(slice(0, half)), pos_row(slice(half, CMP_BLOCK)), w1a, w1b, w2


def _layer(xf, tabs, B, S, g_pre, w_in, cmp_pos_k, cmp_w1_k, cmp_w2_k, cmp_pos_v, cmp_w1_v, cmp_w2_v,
           gla_w_a, gla_b_a, gla_g_norm, w_up_nsa, w_up_gla, w_out, g_post):
    cosf, sinf, noh, ovt = tabs
    d = xf.shape[1]
    w_ext = jnp.concatenate([w_in, jnp.zeros((d, 1), w_in.dtype)], axis=1)
    w_p = jnp.take(w_ext, jnp.asarray(_PIDX), axis=1).astype(BF16)
    p = _proj_call(xf, g_pre.reshape(1, d), w_p, cosf, sinf, noh, S, tm=TILE)

    pa, pb, w1a, w1b, w2 = _compress_weights(cmp_pos_k, cmp_w1_k, cmp_w2_k, cmp_pos_v, cmp_w1_v, cmp_w2_v)
    nc = S // CMP_STRIDE
    r = p["kcvc"].reshape(B, nc, CMP_STRIDE * p["kcvc"].shape[1])
    kc, vct = _cmp_call(r, pa, pb, w1a, w1b, w2)
    o_nsa = _nsa_call(p, kc, vct, ovt, B, S, qt=TILE)

    wa = jnp.zeros((LANES, GLA_HEADS, LANES), F32).at[:GLA_RANK, :, :GLA_DK].set(
        gla_w_a.reshape(GLA_RANK, GLA_HEADS, GLA_DK)).reshape(LANES, GLA_HEADS * LANES)
    ba = jnp.zeros((GLA_HEADS, LANES), F32).at[:, :GLA_DK].set(
        gla_b_a.reshape(GLA_HEADS, GLA_DK)).reshape(1, GLA_HEADS * LANES)
    o_gla = _gla_call(p, wa, ba, gla_g_norm.reshape(1, GLA_DV), B, S, tc=256)

    return _out_call(o_nsa, p["nzs"], o_gla, p["mgs"], xf, w_up_nsa.astype(BF16), w_up_gla.astype(BF16),
                     w_out.astype(BF16), g_post.reshape(1, d), tm=256)


def kernel(x, g_pre, w_in, cmp_pos_k, cmp_w1_k, cmp_w2_k, cmp_pos_v, cmp_w1_v, cmp_w2_v, gla_w_a, gla_b_a,
           gla_g_norm, w_up_nsa, w_up_gla, w_out, g_post):
    B, S, d = x.shape
    assert d == D_MODEL and S % 2048 == 0 and S // SLC_BLOCK <= LANES - HEAD_DIM
    tabs = _position_tables(S)
    xf = x.reshape(B * S, d)
    for l in range(g_pre.shape[0]):
        xf = _layer(xf, tabs, B, S, g_pre[l], w_in[l], cmp_pos_k[l], cmp_w1_k[l], cmp_w2_k[l],
                    cmp_pos_v[l], cmp_w1_v[l], cmp_w2_v[l], gla_w_a[l], gla_b_a[l], gla_g_norm[l],
                    w_up_nsa[l], w_up_gla[l], w_out[l], g_post[l])
    return xf.reshape(B, S, d)
```

```python
import functools

import numpy as np
import jax
import jax.numpy as jnp
from jax import lax
from jax.experimental import pallas as pl
from jax.experimental.pallas import tpu as pltpu

F32 = jnp.float32
BF16 = jnp.bfloat16

D_MODEL = 1024
NSA_HEADS = 8
NSA_GROUPS = 2
HPG = NSA_HEADS // NSA_GROUPS
HEAD_DIM = 64
CMP_BLOCK = 32
CMP_STRIDE = 16
CMP_HIDDEN = 128
SLC_BLOCK = 64
SLC_TOPK = 16
WINDOW = 512
GLA_HEADS = 4
GLA_DK = 64
GLA_DV = 128
GLA_RANK = 16
GLA_TAU = 16.0
GLA_CHUNK = 64
ROPE_THETA = 10000.0
NORM_EPS = 1e-6
QK_SCALE = HEAD_DIM ** -0.5
Q_PRESCALE = QK_SCALE * 1.4426950408889634
GLA_SCALE = GLA_DK ** -0.5

LANES = 128
BF16_SUBLANES = 16
TILE = 256
GATE_ROWS = 32
V_ROWS = HEAD_DIM + BF16_SUBLANES
MASK_BIG = 1e30
VMEM_LIMIT = 56 * 1024 * 1024

NT_DIMS = (((1,), (1,)), ((), ()))
TN_DIMS = (((0,), (0,)), ((), ()))

_SPLITS = (("q", 512), ("kcvc", 256), ("ks", 128), ("vs", 128), ("kw", 128), ("vw", 128),
           ("ng", 24), ("nz", 512), ("gq", 256), ("gk", 256), ("gv", 512), ("ga", 16), ("gr", 512),
           ("mg", 2048))


def _proj_layout():
    src, pofs, o_src, o_dst = [], {}, 0, 0
    for name, width in _SPLITS:
        padded = -(-width // LANES) * LANES
        src.append((o_src, width, padded - width))
        pofs[name] = (o_dst, padded)
        o_src += width
        o_dst += padded
    return src, pofs, o_dst


_PSRC, _POFS, NP_COLS = _proj_layout()


def _pad_proj_weight(w):
    parts = []
    for start, width, pad in _PSRC:
        parts.append(w[:, start:start + width])
        if pad:
            parts.append(jnp.zeros((w.shape[0], pad), w.dtype))
    return jnp.concatenate(parts, axis=1).astype(BF16)


def _sigmoid(v):
    return 1.0 / (1.0 + jnp.exp(-v))


def _proj_kernel(x_ref, g_ref, w_ref, cos_ref, sin_ref, noh_ref,
                 qn_ref, qr_ref, kcr_ref, vcr_ref, ksa_ref, kwr_ref, vst_ref, vwt_ref, gate_ref, nzs_ref,
                 gq_ref, gk_ref, gv_ref, ga_ref, gr_ref, mgs_ref):
    x = x_ref[...]
    y = x * lax.rsqrt(jnp.mean(x * x, axis=-1, keepdims=True) + NORM_EPS)
    h = (y * g_ref[...]).astype(BF16)
    cos = cos_ref[...]
    sin = sin_ref[...]
    tm = x.shape[0]
    lane = lax.broadcasted_iota(jnp.int32, (tm, LANES), 1)
    lo_half = (lane & (HEAD_DIM - 1)) < (HEAD_DIM // 2)
    first_head = lane < HEAD_DIM

    def rope(v):
        rot = jnp.where(lo_half, pltpu.roll(v, LANES - HEAD_DIM // 2, 1), pltpu.roll(v, HEAD_DIM // 2, 1))
        return v * cos + rot * sin

    def chunks(name):
        start, width = _POFS[name]
        step = 256 if width % 256 == 0 else LANES
        for c0 in range(0, width, step):
            v = jnp.dot(h, w_ref[:, start + c0:start + c0 + step], preferred_element_type=F32)
            for k in range(step // LANES):
                yield (c0 + k * LANES) // LANES, v[:, k * LANES:(k + 1) * LANES]

    for pr, v in chunks("q"):
        n_t = (v * Q_PRESCALE).T.astype(BF16)
        r_t = (rope(v) * Q_PRESCALE).T.astype(BF16)
        for e in range(2):
            qn_ref[0, 2 * pr + e] = n_t[e * HEAD_DIM:(e + 1) * HEAD_DIM]
            qr_ref[0, 2 * pr + e] = r_t[e * HEAD_DIM:(e + 1) * HEAD_DIM]
    for (_, v), ref in zip(chunks("kcvc"), (kcr_ref, vcr_ref)):
        ref[...] = v
    noh = noh_ref[...]
    for name, ref, upper in (("ks", ksa_ref, noh), ("kw", kwr_ref, jnp.zeros_like(noh))):
        for _, v in chunks(name):
            r = rope(v)
            ref[:, :LANES] = jnp.where(first_head, r, upper).astype(BF16)
            ref[:, LANES:] = jnp.where(first_head, pltpu.roll(r, HEAD_DIM, 1), upper).astype(BF16)
    row16 = lax.broadcasted_iota(jnp.int32, (BF16_SUBLANES, tm), 0)
    ones_rows = jnp.where(row16 == 0, 1.0, 0.0)
    for name, ref in (("vs", vst_ref), ("vw", vwt_ref)):
        for _, v in chunks(name):
            v_t = v.T
            for g in range(NSA_GROUPS):
                ref[g, 0] = jnp.concatenate([v_t[g * HEAD_DIM:(g + 1) * HEAD_DIM], ones_rows], axis=0).astype(BF16)
    for _, v in chunks("ng"):
        gate_ref[0] = _sigmoid(v).T[:GATE_ROWS]
    for c, v in chunks("nz"):
        nzs_ref[:, c * LANES:(c + 1) * LANES] = (v * _sigmoid(v)).astype(BF16)
    for name, ref in (("gq", gq_ref), ("gk", gk_ref), ("gv", gv_ref), ("gr", gr_ref)):
        for c, v in chunks(name):
            ref[:, c * LANES:(c + 1) * LANES] = v.astype(BF16)
    for _, v in chunks("ga"):
        ga_ref[...] = v
    for c, v in chunks("mg"):
        mgs_ref[:, c * LANES:(c + 1) * LANES] = _sigmoid(v).astype(BF16)


def _proj_call(xf, g_pre, w_p, cosf, sinf, noh, S, tm):
    n, d = xf.shape
    nt = n // tm
    spt = S // tm
    row = lambda width: pl.BlockSpec((tm, width), lambda i: (i, 0))
    pos = pl.BlockSpec((tm, LANES), lambda i: (i % spt, 0))
    qt_shape = ((nt, NSA_HEADS, HEAD_DIM, tm), (1, NSA_HEADS, HEAD_DIM, tm), lambda i: (i, 0, 0, 0))
    vt_shape = ((NSA_GROUPS, nt, V_ROWS, tm), (NSA_GROUPS, 1, V_ROWS, tm), lambda i: (0, i, 0, 0))
    gt_shape = ((nt, GATE_ROWS, tm), (1, GATE_ROWS, tm), lambda i: (i, 0, 0))
    outs = [("qn", qt_shape, BF16), ("qr", qt_shape, BF16), ("kcr", 128, F32), ("vcr", 128, F32),
            ("ksa", 256, BF16),
            ("kwr", 256, BF16), ("vst", vt_shape, BF16), ("vwt", vt_shape, BF16), ("gate", gt_shape, F32),
            ("nzs", 512, BF16), ("gq", 256, BF16), ("gk", 256, BF16), ("gv", 512, BF16),
            ("ga", 128, F32), ("gr", 512, BF16), ("mgs", 2048, BF16)]
    out_specs, out_shape = [], []
    for _, sh, dt in outs:
        if isinstance(sh, int):
            out_specs.append(row(sh))
            out_shape.append(jax.ShapeDtypeStruct((n, sh), dt))
        else:
            out_specs.append(pl.BlockSpec(sh[1], sh[2]))
            out_shape.append(jax.ShapeDtypeStruct(sh[0], dt))
    res = pl.pallas_call(
        _proj_kernel,
        grid=(nt,),
        in_specs=[row(d),
                  pl.BlockSpec((1, d), lambda i: (0, 0)),
                  pl.BlockSpec((d, NP_COLS), lambda i: (0, 0), pipeline_mode=pl.Buffered(1)),
                  pos, pos, pos],
        out_specs=out_specs,
        out_shape=out_shape,
        compiler_params=pltpu.CompilerParams(dimension_semantics=("parallel",),
                                             vmem_limit_bytes=VMEM_LIMIT),
        name="proj",
    )(xf, g_pre, w_p, cosf, sinf, noh)
    return {nm: r for (nm, _, _), r in zip(outs, res)}


def _cmp_kernel(xk_ref, xv_ref, pa_ref, pb_ref, w1a_ref, w1b_ref, w2_ref, kc_ref, vct_ref):
    nc = xk_ref.shape[0] // CMP_STRIDE
    width = xk_ref.shape[1] + xv_ref.shape[1]
    a = None
    b = None
    for l in range(CMP_STRIDE):
        tok = pl.ds(l, nc, stride=CMP_STRIDE)
        x = jnp.concatenate([xk_ref[tok, :], xv_ref[tok, :]], axis=1)
        wsl = slice(l * width, (l + 1) * width)
        da = jnp.dot((x + pa_ref[l:l + 1, :]).astype(BF16), w1a_ref[wsl, :], preferred_element_type=F32)
        db = jnp.dot((x + pb_ref[l:l + 1, :]).astype(BF16), w1b_ref[wsl, :], preferred_element_type=F32)
        a = da if a is None else a + da
        b = db if b is None else b + db
    hid = a + pltpu.roll(b, nc - 1, 0)
    hid = hid * _sigmoid(hid)
    out = jnp.dot(hid.astype(BF16), w2_ref[...], preferred_element_type=F32)
    kw = NSA_GROUPS * LANES
    kc_ref[0] = out[:, :kw].astype(BF16)
    v_t = out[:, kw:].T
    for g in range(NSA_GROUPS):
        vct_ref[0, g] = v_t[g * HEAD_DIM:(g + 1) * HEAD_DIM].astype(BF16)


def _cmp_call(kcr, vcr, pa, pb, w1a, w1b, w2, B, S):
    nc = S // CMP_STRIDE
    full = lambda a: pl.BlockSpec(a.shape, lambda b: (0,) * a.ndim)
    kw = NSA_GROUPS * LANES
    return pl.pallas_call(
        _cmp_kernel,
        grid=(B,),
        in_specs=[pl.BlockSpec((S, LANES), lambda b: (b, 0)), pl.BlockSpec((S, LANES), lambda b: (b, 0)),
                  full(pa), full(pb), full(w1a), full(w1b), full(w2)],
        out_specs=[pl.BlockSpec((1, nc, kw), lambda b: (b, 0, 0)),
                   pl.BlockSpec((1, NSA_GROUPS, HEAD_DIM, nc), lambda b: (b, 0, 0, 0))],
        out_shape=[jax.ShapeDtypeStruct((B, nc, kw), BF16),
                   jax.ShapeDtypeStruct((B, NSA_GROUPS, HEAD_DIM, nc), BF16)],
        compiler_params=pltpu.CompilerParams(dimension_semantics=("parallel",), vmem_limit_bytes=VMEM_LIMIT),
        name="compress",
    )(kcr, vcr, pa, pb, w1a, w1b, w2)


def _nsa_kernel(qn_ref, qr_ref, gate_ref, kc_ref, vct_ref, ksa_ref, vst_ref, kw_ref, vwt_ref, ovt_ref, o_ref,
                qa_sc, acc_sc, m_sc, out_sc, *, qt, nb, nc):
    i = pl.program_id(1)
    s0 = i * qt
    hd = HEAD_DIM
    heads = range(NSA_HEADS)
    grp_of = lambda h: h // HPG

    def gate_row(branch, h):
        return gate_ref[0, h * 3 + branch:h * 3 + branch + 1, :]

    def lanes_of(g):
        return slice(g * LANES, (g + 1) * LANES)

    r_i = lax.broadcasted_iota(jnp.int32, (qt, qt), 0)
    c_i = lax.broadcasted_iota(jnp.int32, (qt, qt), 1)

    def reset():
        acc_sc[...] = jnp.zeros(acc_sc.shape, F32)
        m_sc[...] = jnp.full(m_sc.shape, -jnp.inf, F32)

    def scores(k, q_of, h):
        return jnp.dot(k, q_of(h), preferred_element_type=F32)

    def fold(h, s, vt, mask):
        if mask is not None:
            s = jnp.where(mask, s, -MASK_BIG)
        m_old = m_sc[h]
        m_new = jnp.maximum(m_old, jnp.max(s, axis=0, keepdims=True))
        alpha = jnp.exp2(m_old - m_new)
        pe = jnp.exp2(s - m_new).astype(BF16)
        acc_sc[h] = alpha * acc_sc[h] + jnp.dot(vt, pe, preferred_element_type=F32)
        m_sc[h] = m_new

    def tile(pending, vt, q_of, mask, k_next):
        nxt = []
        for h in heads:
            if k_next is not None:
                nxt.append(scores(k_next[grp_of(h)], q_of, h))
            fold(h, pending[h], vt[grp_of(h)], mask)
        return tuple(nxt)

    def first_scores(k, q_of):
        return tuple(scores(k[grp_of(h)], q_of, h) for h in heads)

    def finish(branch):
        for h in heads:
            scale = gate_row(branch, h) * (1.0 / acc_sc[h, hd:hd + 1, :])
            out_sc[h * hd:(h + 1) * hd, :] += acc_sc[h, :hd, :] * scale

    def rows(j):
        return pl.ds(pl.multiple_of(j * qt, qt), qt)

    jj = lax.broadcasted_iota(jnp.int32, (nc, qt), 0)
    tt = s0 + lax.broadcasted_iota(jnp.int32, (nc, qt), 1)
    cmask = (CMP_STRIDE * jj + (CMP_BLOCK - 1)) <= tt
    kc = [kc_ref[0, :, g * LANES:g * LANES + hd] for g in range(NSA_GROUPS)]
    has_key = (s0 + lax.broadcasted_iota(jnp.int32, (1, qt), 1)) >= (CMP_BLOCK - 1)
    psum = [None] * NSA_GROUPS
    s_all = [jnp.dot(kc[grp_of(h)], qn_ref[0, h], preferred_element_type=F32) for h in heads]
    for h in heads:
        g = grp_of(h)
        s = jnp.where(cmask, s_all[h], -MASK_BIG)
        e = jnp.exp2(s - jnp.max(s, axis=0, keepdims=True))
        inv = jnp.where(has_key, 1.0 / jnp.sum(e, axis=0, keepdims=True), 0.0)
        p = e * inv
        o = jnp.dot(vct_ref[0, g], p.astype(BF16), preferred_element_type=F32)
        out_sc[h * hd:(h + 1) * hd, :] = gate_row(0, h) * o
        psum[g] = p if psum[g] is None else psum[g] + p

    ovt = ovt_ref[...]
    imp_t = []
    for g in range(NSA_GROUPS):
        hi = psum[g].astype(BF16)
        r1 = psum[g] - hi.astype(F32)
        mid = r1.astype(BF16)
        lo = (r1 - mid.astype(F32)).astype(BF16)
        imp_t.append(jnp.dot(ovt, hi, preferred_element_type=F32) + jnp.dot(ovt, mid, preferred_element_type=F32)
                     + jnp.dot(ovt, lo, preferred_element_type=F32))

    reset()
    qr_of = lambda h: qr_ref[0, h]
    d_i = r_i - c_i
    mask_a = d_i > jnp.where(i >= 2, 0, qt)
    mask_b = d_i > jnp.where(i >= 1, -qt, qt)
    ja = jnp.maximum(i - 2, 0)
    jb = jnp.maximum(i - 1, 0)
    k_win = lambda j: [kw_ref[rows(j), g * LANES:g * LANES + hd] for g in range(NSA_GROUPS)]
    v_win = lambda j: [vwt_ref[g, j] for g in range(NSA_GROUPS)]
    ka, kb, kc_w = k_win(ja), k_win(jb), k_win(i)
    carry = first_scores(ka, qr_of)
    carry = tile(carry, v_win(ja), qr_of, mask_a, kb)
    carry = tile(carry, v_win(jb), qr_of, mask_b, kc_w)
    tile(carry, v_win(i), qr_of, r_i <= c_i, None)
    finish(2)

    n_idx = lax.broadcasted_iota(jnp.int32, (nb, qt), 0)
    cur = (s0 + lax.broadcasted_iota(jnp.int32, (nb, qt), 1)) // SLC_BLOCK
    forced = (n_idx == 0) | (n_idx == cur) | (n_idx == cur - 1)
    valid = n_idx <= cur
    sub = 8
    for g in range(NSA_GROUPS):
        val = jnp.where(forced, jnp.inf, jnp.where(valid, imp_t[g], -jnp.inf))
        ranks = []
        for gi in range(nb // sub):
            vg = val[gi * sub:(gi + 1) * sub]
            ng = gi * sub + lax.broadcasted_iota(jnp.int32, (sub, qt), 0)
            cnt = jnp.zeros((sub, qt), F32)
            for mb in range(nb):
                row = val[mb:mb + 1, :]
                if mb < gi * sub:
                    beats = row >= vg
                elif mb >= (gi + 1) * sub:
                    beats = row > vg
                else:
                    beats = (row > vg) | ((row == vg) & (ng > mb))
                cnt = cnt + jnp.where(beats, 1.0, 0.0)
            ranks.append(cnt)
        rank = jnp.concatenate(ranks, axis=0)
        sel = (rank < float(min(SLC_TOPK, nb))) & valid
        notsel = jnp.where(sel, 0.0, 1.0).astype(BF16)
        if nb < LANES - hd:
            notsel = jnp.concatenate([notsel, jnp.zeros((LANES - hd - nb, qt), BF16)], axis=0)
        for h in range(g * HPG, (g + 1) * HPG):
            qa_sc[h, :hd, :] = qr_ref[0, h]
            qa_sc[h, hd:, :] = notsel

    reset()
    qa_of = lambda h: qa_sc[h]
    k_slc = lambda j: [ksa_ref[rows(j), lanes_of(g)] for g in range(NSA_GROUPS)]
    v_slc = lambda j: [vst_ref[g, j] for g in range(NSA_GROUPS)]

    def slc_body(j, carry):
        return tile(carry, v_slc(j), qa_of, None, k_slc(j + 1))

    carry = lax.fori_loop(0, i, slc_body, first_scores(k_slc(0), qa_of))
    tile(carry, v_slc(i), qa_of, r_i <= c_i, None)
    finish(1)

    o_ref[...] = out_sc[...].T.astype(o_ref.dtype)


def _nsa_call(p, kc, vct, ovt, B, S, qt):
    assert WINDOW == 2 * qt
    nq = S // qt
    nb = S // SLC_BLOCK
    nc = S // CMP_STRIDE
    qspec = pl.BlockSpec((1, NSA_HEADS, HEAD_DIM, qt), lambda b, i: (b * nq + i, 0, 0, 0))
    kspec = pl.BlockSpec((S, NSA_GROUPS * LANES), lambda b, i: (b, 0))
    vspec = pl.BlockSpec((NSA_GROUPS, nq, V_ROWS, qt), lambda b, i: (0, b, 0, 0))
    kernel = functools.partial(_nsa_kernel, qt=qt, nb=nb, nc=nc)
    return pl.pallas_call(
        kernel,
        grid=(B, nq),
        in_specs=[qspec, qspec,
                  pl.BlockSpec((1, GATE_ROWS, qt), lambda b, i: (b * nq + i, 0, 0)),
                  pl.BlockSpec((1, nc, NSA_GROUPS * LANES), lambda b, i: (b, 0, 0)),
                  pl.BlockSpec((1, NSA_GROUPS, HEAD_DIM, nc), lambda b, i: (b, 0, 0, 0)),
                  kspec, vspec, kspec, vspec,
                  pl.BlockSpec(ovt.shape, lambda b, i: (0, 0))],
        out_specs=pl.BlockSpec((qt, NSA_HEADS * HEAD_DIM), lambda b, i: (b * nq + i, 0)),
        out_shape=jax.ShapeDtypeStruct((B * S, NSA_HEADS * HEAD_DIM), BF16),
        scratch_shapes=[pltpu.VMEM((NSA_HEADS, LANES, qt), BF16),
                        pltpu.VMEM((NSA_HEADS, V_ROWS, qt), F32),
                        pltpu.VMEM((NSA_HEADS, 1, qt), F32),
                        pltpu.VMEM((NSA_HEADS * HEAD_DIM, qt), F32)],
        compiler_params=pltpu.CompilerParams(dimension_semantics=("parallel", "arbitrary"),
                                             vmem_limit_bytes=VMEM_LIMIT),
        name="nsa",
    )(p["qn"], p["qr"], p["gate"], kc, vct, p["ksa"], p["vst"], p["kwr"], p["vwt"], ovt)


def _gla_kernel(gq_ref, gk_ref, gv_ref, ga_ref, gr_ref, wa_ref, ba_ref, gn_ref, o_ref, st_ref, *, tc):
    @pl.when(pl.program_id(1) == 0)
    def _():
        st_ref[...] = jnp.zeros(st_ref.shape, F32)

    a = jnp.dot(ga_ref[...], wa_ref[...], preferred_element_type=F32,
                precision=lax.Precision.HIGHEST) + ba_ref[...]
    log_a = (jnp.minimum(a, 0.0) - jnp.log1p(jnp.exp(-jnp.abs(a)))) * (1.0 / GLA_TAU)
    c_sz = GLA_CHUNK
    r_i = lax.broadcasted_iota(jnp.int32, (c_sz, c_sz), 0)
    c_i = lax.broadcasted_iota(jnp.int32, (c_sz, c_sz), 1)
    causal = r_i >= c_i
    tril = jnp.where(causal, 1.0, 0.0)
    gn = gn_ref[...]
    lane = lax.broadcasted_iota(jnp.int32, (1, LANES), 1)
    for c in range(tc // c_sz):
        rows = slice(c * c_sz, (c + 1) * c_sz)
        bcum = jnp.dot(tril, log_a[rows], preferred_element_type=F32, precision=lax.Precision.HIGHEST)
        bl = bcum[c_sz - 1:c_sz]
        gk = gk_ref[rows, :].astype(F32)
        qg = (gq_ref[rows, :].astype(F32) * GLA_SCALE) * jnp.exp(bcum)
        kg = gk * jnp.exp(-bcum)
        kd = gk * jnp.exp(bl - bcum)
        dec = jnp.exp(bl)
        for h in range(GLA_HEADS):
            ks = slice((h // 2) * LANES, (h // 2 + 1) * LANES)
            own = (lane // GLA_DK) == (h % 2)
            vs = slice(h * GLA_DV, (h + 1) * GLA_DV)
            qg_h = jnp.where(own, qg[:, ks], 0.0).astype(BF16)
            v_h = gv_ref[rows, vs]
            att = lax.dot_general(qg_h, kg[:, ks].astype(BF16), NT_DIMS, preferred_element_type=F32)
            att = jnp.where(causal, att, 0.0)
            st = st_ref[h]
            o = (jnp.dot(att.astype(BF16), v_h, preferred_element_type=F32)
                 + lax.dot_general(qg_h, st.astype(BF16), NT_DIMS, preferred_element_type=F32))
            st_ref[h] = st * dec[:, ks] + lax.dot_general(v_h, kd[:, ks].astype(BF16), TN_DIMS,
                                                          preferred_element_type=F32)
            y = o * lax.rsqrt(jnp.mean(o * o, axis=-1, keepdims=True) + NORM_EPS) * gn
            gr = gr_ref[rows, vs].astype(F32)
            o_ref[rows, vs] = (y * (gr * _sigmoid(gr))).astype(BF16)


def _gla_call(p, wa, ba, gn, B, S, tc):
    nt = S // tc
    row = lambda w: pl.BlockSpec((tc, w), lambda b, t: (b * nt + t, 0))
    full = lambda a: pl.BlockSpec(a.shape, lambda b, t: (0,) * a.ndim)
    kwidth = GLA_HEADS * GLA_DK
    vwidth = GLA_HEADS * GLA_DV
    return pl.pallas_call(
        functools.partial(_gla_kernel, tc=tc),
        grid=(B, nt),
        in_specs=[row(kwidth), row(kwidth), row(vwidth), row(LANES), row(vwidth), full(wa), full(ba), full(gn)],
        out_specs=row(vwidth),
        out_shape=jax.ShapeDtypeStruct((B * S, vwidth), BF16),
        scratch_shapes=[pltpu.VMEM((GLA_HEADS, GLA_DV, LANES), F32)],
        compiler_params=pltpu.CompilerParams(dimension_semantics=("parallel", "arbitrary"),
                                             vmem_limit_bytes=VMEM_LIMIT),
        name="gla",
    )(p["gq"], p["gk"], p["gv"], p["ga"], p["gr"], wa, ba, gn)


def _out_kernel(on_ref, nzs_ref, og_ref, mgs_ref, x_ref, wn_ref, wg_ref, wo_ref, gp_ref, o_ref):
    d = x_ref.shape[1]
    gated = on_ref[...].astype(F32) * nzs_ref[...].astype(F32)
    a = jnp.dot(gated.astype(BF16), wn_ref[...], preferred_element_type=F32)
    b = jnp.dot(og_ref[...], wg_ref[...], preferred_element_type=F32)
    y = mgs_ref[:, :d].astype(F32) * a + mgs_ref[:, d:].astype(F32) * b
    out = jnp.dot(y.astype(BF16), wo_ref[...], preferred_element_type=F32)
    r = out * lax.rsqrt(jnp.mean(out * out, axis=-1, keepdims=True) + NORM_EPS)
    o_ref[...] = x_ref[...] + r * gp_ref[...]


def _out_call(o_nsa, nzs, o_gla, mgs, xf, wn, wg, wo, gp, tm):
    n, d = xf.shape
    row = lambda w: pl.BlockSpec((tm, w), lambda i: (i, 0))
    full = lambda a: pl.BlockSpec(a.shape, lambda i: (0,) * a.ndim)
    return pl.pallas_call(
        _out_kernel,
        grid=(n // tm,),
        in_specs=[row(o_nsa.shape[1]), row(nzs.shape[1]), row(o_gla.shape[1]), row(mgs.shape[1]), row(d),
                  full(wn), full(wg), full(wo), full(gp)],
        out_specs=row(d),
        out_shape=jax.ShapeDtypeStruct((n, d), F32),
        compiler_params=pltpu.CompilerParams(dimension_semantics=("parallel",), vmem_limit_bytes=VMEM_LIMIT),
        name="merge_out",
    )(o_nsa, nzs, o_gla, mgs, xf, wn, wg, wo, gp)


def _position_tables(S):
    pos = jnp.arange(S, dtype=F32)
    inv_freq = ROPE_THETA ** (-jnp.arange(0, HEAD_DIM, 2, dtype=F32) / HEAD_DIM)
    ang = pos[:, None] * inv_freq[None, :]
    cos, sin = jnp.cos(ang), jnp.sin(ang)
    cosf = jnp.concatenate([cos] * 4, axis=1)
    sinf = jnp.concatenate([-sin, sin, -sin, sin], axis=1)
    blk = np.arange(S) // SLC_BLOCK
    noh = np.zeros((S, LANES), np.float32)
    noh[np.arange(S), HEAD_DIM + blk] = -MASK_BIG
    nb, nc = S // SLC_BLOCK, S // CMP_STRIDE
    cs = CMP_STRIDE * np.arange(nc)
    bs = SLC_BLOCK * np.arange(nb)
    ov = (cs[None, :] < bs[:, None] + SLC_BLOCK) & (cs[None, :] + CMP_BLOCK > bs[:, None])
    ov[:, (S - CMP_BLOCK) // CMP_STRIDE + 1:] = False
    return cosf, sinf, jnp.asarray(noh), jnp.asarray(ov.astype(np.float32)).astype(BF16)


def _compress_weights(pos_k, w1_k, w2_k, pos_v, w1_v, w2_v):
    half = CMP_BLOCK // 2
    w1a = jnp.zeros((half, 2, NSA_GROUPS, HEAD_DIM, 2, NSA_GROUPS, CMP_HIDDEN), F32)
    w1b = jnp.zeros_like(w1a)
    for wi, w1 in enumerate((w1_k, w1_v)):
        for g in range(NSA_GROUPS):
            w1a = w1a.at[:, wi, g, :, wi, g, :].set(w1[:half])
            w1b = w1b.at[:, wi, g, :, wi, g, :].set(w1[half:])
    kdim = half * 2 * NSA_GROUPS * HEAD_DIM
    hdim = 2 * NSA_GROUPS * CMP_HIDDEN
    w1a = w1a.reshape(kdim, hdim).astype(BF16)
    w1b = w1b.reshape(kdim, hdim).astype(BF16)
    w2 = jnp.zeros((2, NSA_GROUPS, CMP_HIDDEN, 3 * LANES), F32)
    for g in range(NSA_GROUPS):
        w2 = w2.at[0, g, :, g * LANES:g * LANES + HEAD_DIM].set(w2_k)
        w2 = w2.at[1, g, :, 2 * LANES + g * HEAD_DIM:2 * LANES + (g + 1) * HEAD_DIM].set(w2_v)
    w2 = w2.reshape(hdim, 3 * LANES).astype(BF16)

    def pos_rows(sl):
        return jnp.concatenate([pos_k[sl]] * NSA_GROUPS + [pos_v[sl]] * NSA_GROUPS, axis=1)

    return pos_rows(slice(0, half)), pos_rows(slice(half, CMP_BLOCK)), w1a, w1b, w2


def _layer(xf, tabs, B, S, g_pre, w_in, cmp_pos_k, cmp_w1_k, cmp_w2_k, cmp_pos_v, cmp_w1_v, cmp_w2_v,
           gla_w_a, gla_b_a, gla_g_norm, w_up_nsa, w_up_gla, w_out, g_post):
    cosf, sinf, noh, ovt = tabs
    d = xf.shape[1]
    p = _proj_call(xf, g_pre.reshape(1, d), _pad_proj_weight(w_in), cosf, sinf, noh, S, tm=TILE)

    pa, pb, w1a, w1b, w2 = _compress_weights(cmp_pos_k, cmp_w1_k, cmp_w2_k, cmp_pos_v, cmp_w1_v, cmp_w2_v)
    kc, vct = _cmp_call(p["kcr"], p["vcr"], pa, pb, w1a, w1b, w2, B, S)
    o_nsa = _nsa_call(p, kc, vct, ovt, B, S, qt=TILE)

    wa = jnp.zeros((LANES, gla_w_a.shape[1]), F32).at[:GLA_RANK].set(gla_w_a)
    o_gla = _gla_call(p, wa, gla_b_a.reshape(1, -1), gla_g_norm.reshape(1, GLA_DV), B, S, tc=TILE)

    return _out_call(o_nsa, p["nzs"], o_gla, p["mgs"], xf, w_up_nsa.astype(BF16), w_up_gla.astype(BF16),
                     w_out.astype(BF16), g_post.reshape(1, d), tm=TILE)


def kernel(x, g_pre, w_in, cmp_pos_k, cmp_w1_k, cmp_w2_k, cmp_pos_v, cmp_w1_v, cmp_w2_v, gla_w_a, gla_b_a,
           gla_g_norm, w_up_nsa, w_up_gla, w_out, g_post):
    B, S, d = x.shape
    assert d == D_MODEL and S % 2048 == 0 and S // SLC_BLOCK <= LANES - HEAD_DIM
    tabs = _position_tables(S)
    xf = x.reshape(B * S, d)
    for l in range(g_pre.shape[0]):
        xf = _layer(xf, tabs, B, S, g_pre[l], w_in[l], cmp_pos_k[l], cmp_w1_k[l], cmp_w2_k[l],
                    cmp_pos_v[l], cmp_w1_v[l], cmp_w2_v[l], gla_w_a[l], gla_b_a[l], gla_g_norm[l],
                    w_up_nsa[l], w_up_gla[l], w_out[l], g_post[l])
    return xf.reshape(B, S, d)
```

```python
import functools

import numpy as np
import jax
import jax.numpy as jnp
from jax import lax
from jax.experimental import pallas as pl
from jax.experimental.pallas import tpu as pltpu

F32 = jnp.float32
BF16 = jnp.bfloat16

D_MODEL = 1024
NSA_HEADS = 8
NSA_GROUPS = 2
HPG = NSA_HEADS // NSA_GROUPS
HEAD_DIM = 64
CMP_BLOCK = 32
CMP_STRIDE = 16
CMP_HIDDEN = 128
SLC_BLOCK = 64
SLC_TOPK = 16
WINDOW = 512
GLA_HEADS = 4
GLA_DK = 64
GLA_DV = 128
GLA_RANK = 16
GLA_TAU = 16.0
GLA_CHUNK = 64
ROPE_THETA = 10000.0
NORM_EPS = 1e-6
QK_SCALE = HEAD_DIM ** -0.5
Q_PRESCALE = QK_SCALE * 1.4426950408889634
GLA_SCALE = GLA_DK ** -0.5

LANES = 128
MXU_COLS = 256
BF16_SUBLANES = 16
TILE = 256
GATE_ROWS = 32
V_ROWS = HEAD_DIM + BF16_SUBLANES
MASK_BIG = 1e30
VMEM_LIMIT = 56 * 1024 * 1024

NT_DIMS = (((1,), (1,)), ((), ()))
TN_DIMS = (((0,), (0,)), ((), ()))

_SPLITS = (("q", 512), ("kcvc", 256), ("ks", 128), ("vs", 128), ("kw", 128), ("vw", 128),
           ("ng", 24), ("nz", 512), ("gq", 256), ("gk", 256), ("gv", 512), ("ga", 16), ("gr", 512),
           ("mg", 2048))


def _proj_layout():
    src, pofs, o_src, o_dst = [], {}, 0, 0
    for name, width in _SPLITS:
        padded = -(-width // LANES) * LANES
        src.append((o_src, width, padded - width))
        pofs[name] = (o_dst, padded)
        o_src += width
        o_dst += padded
    return src, pofs, o_dst


_PSRC, _POFS, NP_COLS = _proj_layout()


def _pad_proj_weight(w):
    parts = []
    for start, width, pad in _PSRC:
        parts.append(w[:, start:start + width])
        if pad:
            parts.append(jnp.zeros((w.shape[0], pad), w.dtype))
    return jnp.concatenate(parts, axis=1).astype(BF16)


def _sigmoid(v):
    return 1.0 / (1.0 + jnp.exp(-v))


def _proj_kernel(x_ref, g_ref, w_ref, cos_ref, sin_ref, noh_ref,
                 qn_ref, qr_ref, kcr_ref, vcr_ref, ksa_ref, kwr_ref, vst_ref, vwt_ref, gate_ref, nzs_ref,
                 gq_ref, gk_ref, gv_ref, ga_ref, gr_ref, mgs_ref):
    x = x_ref[...]
    y = x * lax.rsqrt(jnp.mean(x * x, axis=-1, keepdims=True) + NORM_EPS)
    h = (y * g_ref[...]).astype(BF16)
    cos = cos_ref[...]
    sin = sin_ref[...]
    tm = x.shape[0]
    lane = lax.broadcasted_iota(jnp.int32, (tm, LANES), 1)
    lo_half = (lane & (HEAD_DIM - 1)) < (HEAD_DIM // 2)
    first_head = lane < HEAD_DIM

    def rope(v):
        rot = jnp.where(lo_half, pltpu.roll(v, LANES - HEAD_DIM // 2, 1), pltpu.roll(v, HEAD_DIM // 2, 1))
        return v * cos + rot * sin

    windows = {}

    def chunks(name):
        start, width = _POFS[name]
        for c in range(width // LANES):
            col = start + c * LANES
            w = col // MXU_COLS
            if w not in windows:
                windows[w] = jnp.dot(h, w_ref[:, w * MXU_COLS:(w + 1) * MXU_COLS], preferred_element_type=F32)
            off = col % MXU_COLS
            yield c, windows[w][:, off:off + LANES]

    for pr, v in chunks("q"):
        n_t = (v * Q_PRESCALE).T.astype(BF16)
        r_t = (rope(v) * Q_PRESCALE).T.astype(BF16)
        for e in range(2):
            qn_ref[0, 2 * pr + e] = n_t[e * HEAD_DIM:(e + 1) * HEAD_DIM]
            qr_ref[0, 2 * pr + e] = r_t[e * HEAD_DIM:(e + 1) * HEAD_DIM]
    for (_, v), ref in zip(chunks("kcvc"), (kcr_ref, vcr_ref)):
        ref[...] = v
    noh = noh_ref[...]
    for name, ref, upper in (("ks", ksa_ref, noh), ("kw", kwr_ref, jnp.zeros_like(noh))):
        for _, v in chunks(name):
            r = rope(v)
            ref[:, :LANES] = jnp.where(first_head, r, upper).astype(BF16)
            ref[:, LANES:] = jnp.where(first_head, pltpu.roll(r, HEAD_DIM, 1), upper).astype(BF16)
    row16 = lax.broadcasted_iota(jnp.int32, (BF16_SUBLANES, tm), 0)
    ones_rows = jnp.where(row16 == 0, 1.0, 0.0)
    for name, ref in (("vs", vst_ref), ("vw", vwt_ref)):
        for _, v in chunks(name):
            v_t = v.T
            for g in range(NSA_GROUPS):
                ref[g, 0] = jnp.concatenate([v_t[g * HEAD_DIM:(g + 1) * HEAD_DIM], ones_rows], axis=0).astype(BF16)
    for _, v in chunks("ng"):
        gate_ref[0] = _sigmoid(v).T[:GATE_ROWS]
    for c, v in chunks("nz"):
        nzs_ref[:, c * LANES:(c + 1) * LANES] = (v * _sigmoid(v)).astype(BF16)
    for name, ref in (("gq", gq_ref), ("gk", gk_ref), ("gv", gv_ref), ("gr", gr_ref)):
        for c, v in chunks(name):
            ref[:, c * LANES:(c + 1) * LANES] = v.astype(BF16)
    for _, v in chunks("ga"):
        ga_ref[...] = v
    for c, v in chunks("mg"):
        mgs_ref[:, c * LANES:(c + 1) * LANES] = _sigmoid(v).astype(BF16)


def _proj_call(xf, g_pre, w_p, cosf, sinf, noh, S, tm):
    n, d = xf.shape
    nt = n // tm
    spt = S // tm
    row = lambda width: pl.BlockSpec((tm, width), lambda i: (i, 0))
    pos = pl.BlockSpec((tm, LANES), lambda i: (i % spt, 0))
    qt_shape = ((nt, NSA_HEADS, HEAD_DIM, tm), (1, NSA_HEADS, HEAD_DIM, tm), lambda i: (i, 0, 0, 0))
    vt_shape = ((NSA_GROUPS, nt, V_ROWS, tm), (NSA_GROUPS, 1, V_ROWS, tm), lambda i: (0, i, 0, 0))
    gt_shape = ((nt, GATE_ROWS, tm), (1, GATE_ROWS, tm), lambda i: (i, 0, 0))
    outs = [("qn", qt_shape, BF16), ("qr", qt_shape, BF16), ("kcr", 128, F32), ("vcr", 128, F32),
            ("ksa", 256, BF16),
            ("kwr", 256, BF16), ("vst", vt_shape, BF16), ("vwt", vt_shape, BF16), ("gate", gt_shape, F32),
            ("nzs", 512, BF16), ("gq", 256, BF16), ("gk", 256, BF16), ("gv", 512, BF16),
            ("ga", 128, F32), ("gr", 512, BF16), ("mgs", 2048, BF16)]
    out_specs, out_shape = [], []
    for _, sh, dt in outs:
        if isinstance(sh, int):
            out_specs.append(row(sh))
            out_shape.append(jax.ShapeDtypeStruct((n, sh), dt))
        else:
            out_specs.append(pl.BlockSpec(sh[1], sh[2]))
            out_shape.append(jax.ShapeDtypeStruct(sh[0], dt))
    res = pl.pallas_call(
        _proj_kernel,
        grid=(nt,),
        in_specs=[row(d),
                  pl.BlockSpec((1, d), lambda i: (0, 0)),
                  pl.BlockSpec((d, NP_COLS), lambda i: (0, 0), pipeline_mode=pl.Buffered(1)),
                  pos, pos, pos],
        out_specs=out_specs,
        out_shape=out_shape,
        compiler_params=pltpu.CompilerParams(dimension_semantics=("parallel",),
                                             vmem_limit_bytes=VMEM_LIMIT),
        name="proj",
    )(xf, g_pre, w_p, cosf, sinf, noh)
    return {nm: r for (nm, _, _), r in zip(outs, res)}


def _cmp_kernel(xk_ref, xv_ref, pa_ref, pb_ref, w1a_ref, w1b_ref, w2_ref, kc_ref, vct_ref):
    nc = xk_ref.shape[0] // CMP_STRIDE
    width = xk_ref.shape[1] + xv_ref.shape[1]
    a = None
    b = None
    for l in range(CMP_STRIDE):
        tok = pl.ds(l, nc, stride=CMP_STRIDE)
        x = jnp.concatenate([xk_ref[tok, :], xv_ref[tok, :]], axis=1)
        wsl = slice(l * width, (l + 1) * width)
        da = jnp.dot((x + pa_ref[l:l + 1, :]).astype(BF16), w1a_ref[wsl, :], preferred_element_type=F32)
        db = jnp.dot((x + pb_ref[l:l + 1, :]).astype(BF16), w1b_ref[wsl, :], preferred_element_type=F32)
        a = da if a is None else a + da
        b = db if b is None else b + db
    hid = a + pltpu.roll(b, nc - 1, 0)
    hid = hid * _sigmoid(hid)
    out = jnp.dot(hid.astype(BF16), w2_ref[...], preferred_element_type=F32)
    kw = NSA_GROUPS * LANES
    kc_ref[0] = out[:, :kw].astype(BF16)
    v_t = out[:, kw:].T
    for g in range(NSA_GROUPS):
        vct_ref[0, g] = v_t[g * HEAD_DIM:(g + 1) * HEAD_DIM].astype(BF16)


def _cmp_call(kcr, vcr, pa, pb, w1a, w1b, w2, B, S):
    nc = S // CMP_STRIDE
    full = lambda a: pl.BlockSpec(a.shape, lambda b: (0,) * a.ndim)
    kw = NSA_GROUPS * LANES
    return pl.pallas_call(
        _cmp_kernel,
        grid=(B,),
        in_specs=[pl.BlockSpec((S, LANES), lambda b: (b, 0)), pl.BlockSpec((S, LANES), lambda b: (b, 0)),
                  full(pa), full(pb), full(w1a), full(w1b), full(w2)],
        out_specs=[pl.BlockSpec((1, nc, kw), lambda b: (b, 0, 0)),
                   pl.BlockSpec((1, NSA_GROUPS, HEAD_DIM, nc), lambda b: (b, 0, 0, 0))],
        out_shape=[jax.ShapeDtypeStruct((B, nc, kw), BF16),
                   jax.ShapeDtypeStruct((B, NSA_GROUPS, HEAD_DIM, nc), BF16)],
        compiler_params=pltpu.CompilerParams(dimension_semantics=("parallel",), vmem_limit_bytes=VMEM_LIMIT),
        name="compress",
    )(kcr, vcr, pa, pb, w1a, w1b, w2)


def _nsa_kernel(qn_ref, qr_ref, gate_ref, kc_ref, vct_ref, ksa_ref, vst_ref, kw_ref, vwt_ref, ovt_ref, o_ref,
                qa_sc, acc_sc, m_sc, out_sc, *, qt, nb, nc):
    i = pl.program_id(1)
    s0 = i * qt
    hd = HEAD_DIM
    heads = range(NSA_HEADS)
    grp_of = lambda h: h // HPG

    def gate_row(branch, h):
        return gate_ref[0, h * 3 + branch:h * 3 + branch + 1, :]

    def lanes_of(g):
        return slice(g * LANES, (g + 1) * LANES)

    r_i = lax.broadcasted_iota(jnp.int32, (qt, qt), 0)
    c_i = lax.broadcasted_iota(jnp.int32, (qt, qt), 1)

    def reset():
        acc_sc[...] = jnp.zeros(acc_sc.shape, F32)
        m_sc[...] = jnp.full(m_sc.shape, -jnp.inf, F32)

    def scores(k, q_of, h):
        return jnp.dot(k, q_of(h), preferred_element_type=F32)

    def fold(h, s, vt, mask):
        if mask is not None:
            s = jnp.where(mask, s, -MASK_BIG)
        m_old = m_sc[h]
        m_new = jnp.maximum(m_old, jnp.max(s, axis=0, keepdims=True))
        alpha = jnp.exp2(m_old - m_new)
        pe = jnp.exp2(s - m_new).astype(BF16)
        acc_sc[h] = alpha * acc_sc[h] + jnp.dot(vt, pe, preferred_element_type=F32)
        m_sc[h] = m_new

    def tile(pending, vt, q_of, mask, k_next):
        nxt = []
        for h in heads:
            if k_next is not None:
                nxt.append(scores(k_next[grp_of(h)], q_of, h))
            fold(h, pending[h], vt[grp_of(h)], mask)
        return tuple(nxt)

    def first_scores(k, q_of):
        return tuple(scores(k[grp_of(h)], q_of, h) for h in heads)

    def finish(branch):
        for h in heads:
            scale = gate_row(branch, h) * (1.0 / acc_sc[h, hd:hd + 1, :])
            out_sc[h * hd:(h + 1) * hd, :] += acc_sc[h, :hd, :] * scale

    def rows(j):
        return pl.ds(pl.multiple_of(j * qt, qt), qt)

    jj = lax.broadcasted_iota(jnp.int32, (nc, qt), 0)
    tt = s0 + lax.broadcasted_iota(jnp.int32, (nc, qt), 1)
    cmask = (CMP_STRIDE * jj + (CMP_BLOCK - 1)) <= tt
    kc = [kc_ref[0, :, g * LANES:g * LANES + hd] for g in range(NSA_GROUPS)]
    has_key = (s0 + lax.broadcasted_iota(jnp.int32, (1, qt), 1)) >= (CMP_BLOCK - 1)
    psum = [None] * NSA_GROUPS
    s_all = [jnp.dot(kc[grp_of(h)], qn_ref[0, h], preferred_element_type=F32) for h in heads]
    for h in heads:
        g = grp_of(h)
        s = jnp.where(cmask, s_all[h], -MASK_BIG)
        e = jnp.exp2(s - jnp.max(s, axis=0, keepdims=True))
        inv = jnp.where(has_key, 1.0 / jnp.sum(e, axis=0, keepdims=True), 0.0)
        p = e * inv
        o = jnp.dot(vct_ref[0, g], p.astype(BF16), preferred_element_type=F32)
        out_sc[h * hd:(h + 1) * hd, :] = gate_row(0, h) * o
        psum[g] = p if psum[g] is None else psum[g] + p

    ovt = ovt_ref[...]
    imp_t = []
    for g in range(NSA_GROUPS):
        hi = psum[g].astype(BF16)
        r1 = psum[g] - hi.astype(F32)
        mid = r1.astype(BF16)
        lo = (r1 - mid.astype(F32)).astype(BF16)
        imp_t.append(jnp.dot(ovt, hi, preferred_element_type=F32) + jnp.dot(ovt, mid, preferred_element_type=F32)
                     + jnp.dot(ovt, lo, preferred_element_type=F32))

    reset()
    qr_of = lambda h: qr_ref[0, h]
    d_i = r_i - c_i
    mask_a = d_i > jnp.where(i >= 2, 0, qt)
    mask_b = d_i > jnp.where(i >= 1, -qt, qt)
    ja = jnp.maximum(i - 2, 0)
    jb = jnp.maximum(i - 1, 0)
    k_win = lambda j: [kw_ref[rows(j), g * LANES:g * LANES + hd] for g in range(NSA_GROUPS)]
    v_win = lambda j: [vwt_ref[g, j] for g in range(NSA_GROUPS)]
    ka, kb, kc_w = k_win(ja), k_win(jb), k_win(i)
    carry = first_scores(ka, qr_of)
    carry = tile(carry, v_win(ja), qr_of, mask_a, kb)
    carry = tile(carry, v_win(jb), qr_of, mask_b, kc_w)
    tile(carry, v_win(i), qr_of, r_i <= c_i, None)
    finish(2)

    n_idx = lax.broadcasted_iota(jnp.int32, (nb, qt), 0)
    n_f = n_idx.astype(F32)
    cur = (s0 + lax.broadcasted_iota(jnp.int32, (nb, qt), 1)) // SLC_BLOCK
    forced = (n_idx == 0) | (n_idx == cur) | (n_idx == cur - 1)
    valid = n_idx <= cur
    free = valid & jnp.logical_not(forced)
    for g in range(NSA_GROUPS):
        val = jnp.where(free, imp_t[g], -jnp.inf)
        picked = jnp.zeros((nb, qt), F32)
        for _ in range(max(min(SLC_TOPK, nb) - 3, 0)):
            top = jnp.max(val, axis=0, keepdims=True)
            first = jnp.min(jnp.where(val == top, n_f, float(nb)), axis=0, keepdims=True)
            pick = n_f == first
            picked = jnp.where(pick, 1.0, picked)
            val = jnp.where(pick, -jnp.inf, val)
        sel = (forced | (picked > 0.0)) & valid
        notsel = jnp.where(sel, 0.0, 1.0).astype(BF16)
        if nb < LANES - hd:
            notsel = jnp.concatenate([notsel, jnp.zeros((LANES - hd - nb, qt), BF16)], axis=0)
        for h in range(g * HPG, (g + 1) * HPG):
            qa_sc[h, :hd, :] = qr_ref[0, h]
            qa_sc[h, hd:, :] = notsel

    reset()
    qa_of = lambda h: qa_sc[h]
    k_slc = lambda j: [ksa_ref[rows(j), lanes_of(g)] for g in range(NSA_GROUPS)]
    v_slc = lambda j: [vst_ref[g, j] for g in range(NSA_GROUPS)]

    def slc_tile(j, carry):
        return tile(carry, v_slc(j), qa_of, None, k_slc(j + 1))

    odd = i % 2
    carry = first_scores(k_slc(0), qa_of)
    carry = lax.cond(odd == 1, lambda c: slc_tile(0, c), lambda c: c, carry)

    def slc_pair(p, carry):
        j = 2 * p + odd
        return slc_tile(j + 1, slc_tile(j, carry))

    carry = lax.fori_loop(0, i // 2, slc_pair, carry)
    tile(carry, v_slc(i), qa_of, r_i <= c_i, None)
    finish(1)

    o_ref[...] = out_sc[...].T.astype(o_ref.dtype)


def _nsa_call(p, kc, vct, ovt, B, S, qt):
    assert WINDOW == 2 * qt
    nq = S // qt
    nb = S // SLC_BLOCK
    nc = S // CMP_STRIDE
    qspec = pl.BlockSpec((1, NSA_HEADS, HEAD_DIM, qt), lambda b, i: (b * nq + i, 0, 0, 0))
    kspec = pl.BlockSpec((S, NSA_GROUPS * LANES), lambda b, i: (b, 0))
    vspec = pl.BlockSpec((NSA_GROUPS, nq, V_ROWS, qt), lambda b, i: (0, b, 0, 0))
    kernel = functools.partial(_nsa_kernel, qt=qt, nb=nb, nc=nc)
    return pl.pallas_call(
        kernel,
        grid=(B, nq),
        in_specs=[qspec, qspec,
                  pl.BlockSpec((1, GATE_ROWS, qt), lambda b, i: (b * nq + i, 0, 0)),
                  pl.BlockSpec((1, nc, NSA_GROUPS * LANES), lambda b, i: (b, 0, 0)),
                  pl.BlockSpec((1, NSA_GROUPS, HEAD_DIM, nc), lambda b, i: (b, 0, 0, 0)),
                  kspec, vspec, kspec, vspec,
                  pl.BlockSpec(ovt.shape, lambda b, i: (0, 0))],
        out_specs=pl.BlockSpec((qt, NSA_HEADS * HEAD_DIM), lambda b, i: (b * nq + i, 0)),
        out_shape=jax.ShapeDtypeStruct((B * S, NSA_HEADS * HEAD_DIM), BF16),
        scratch_shapes=[pltpu.VMEM((NSA_HEADS, LANES, qt), BF16),
                        pltpu.VMEM((NSA_HEADS, V_ROWS, qt), F32),
                        pltpu.VMEM((NSA_HEADS, 1, qt), F32),
                        pltpu.VMEM((NSA_HEADS * HEAD_DIM, qt), F32)],
        compiler_params=pltpu.CompilerParams(dimension_semantics=("parallel", "arbitrary"),
                                             vmem_limit_bytes=VMEM_LIMIT),
        name="nsa",
    )(p["qn"], p["qr"], p["gate"], kc, vct, p["ksa"], p["vst"], p["kwr"], p["vwt"], ovt)


def _gla_kernel(gq_ref, gk_ref, gv_ref, ga_ref, gr_ref, wa_ref, ba_ref, gn_ref, o_ref, st_ref, *, tc):
    @pl.when(pl.program_id(1) == 0)
    def _():
        st_ref[...] = jnp.zeros(st_ref.shape, F32)

    a = jnp.dot(ga_ref[...], wa_ref[...], preferred_element_type=F32,
                precision=lax.Precision.HIGHEST) + ba_ref[...]
    log_a = (jnp.minimum(a, 0.0) - jnp.log1p(jnp.exp(-jnp.abs(a)))) * (1.0 / GLA_TAU)
    c_sz = GLA_CHUNK
    n_chunk = tc // c_sz
    chunks = range(n_chunk)
    heads = range(GLA_HEADS)
    r_t = lax.broadcasted_iota(jnp.int32, (tc, tc), 0)
    c_t = lax.broadcasted_iota(jnp.int32, (tc, tc), 1)
    tril = jnp.where((r_t >= c_t) & ((r_t // c_sz) == (c_t // c_sz)), 1.0, 0.0).astype(BF16)
    hi = log_a.astype(BF16)
    r1 = log_a - hi.astype(F32)
    mid = r1.astype(BF16)
    lo = (r1 - mid.astype(F32)).astype(BF16)
    bcum = (jnp.dot(tril, hi, preferred_element_type=F32) + jnp.dot(tril, mid, preferred_element_type=F32)
            + jnp.dot(tril, lo, preferred_element_type=F32))
    gk = gk_ref[...].astype(F32)
    qg = (gq_ref[...].astype(F32) * GLA_SCALE) * jnp.exp(bcum)
    kg = (gk * jnp.exp(-bcum)).astype(BF16)
    lane = lax.broadcasted_iota(jnp.int32, (1, LANES), 1)
    causal = (lax.broadcasted_iota(jnp.int32, (c_sz, c_sz), 0)
              >= lax.broadcasted_iota(jnp.int32, (c_sz, c_sz), 1))
    rows = lambda c: slice(c * c_sz, (c + 1) * c_sz)
    klanes = lambda h: slice((h // 2) * LANES, (h // 2 + 1) * LANES)
    vlanes = lambda h: slice(h * GLA_DV, (h + 1) * GLA_DV)
    kd, dec = [], []
    for c in chunks:
        bl = bcum[(c + 1) * c_sz - 1:(c + 1) * c_sz]
        kd.append((gk[rows(c)] * jnp.exp(bl - bcum[rows(c)])).astype(BF16))
        dec.append(jnp.exp(bl))
    qg_h, att, upd = {}, {}, {}
    for c in chunks:
        for h in heads:
            own = (lane // GLA_DK) == (h % 2)
            qg_h[c, h] = jnp.where(own, qg[rows(c), klanes(h)], 0.0).astype(BF16)
            att[c, h] = lax.dot_general(qg_h[c, h], kg[rows(c), klanes(h)], NT_DIMS, preferred_element_type=F32)
    for c in chunks:
        for h in heads:
            upd[c, h] = lax.dot_general(gv_ref[rows(c), vlanes(h)], kd[c][:, klanes(h)], TN_DIMS,
                                        preferred_element_type=F32)
    st_before = {}
    for h in heads:
        st = st_ref[h]
        for c in chunks:
            st_before[c, h] = st.astype(BF16)
            st = st * dec[c][:, klanes(h)] + upd[c, h]
        st_ref[h] = st
    gn = gn_ref[...]
    for c in chunks:
        for h in heads:
            a_c = jnp.where(causal, att[c, h], 0.0).astype(BF16)
            o = (jnp.dot(a_c, gv_ref[rows(c), vlanes(h)], preferred_element_type=F32)
                 + lax.dot_general(qg_h[c, h], st_before[c, h], NT_DIMS, preferred_element_type=F32))
            y = o * lax.rsqrt(jnp.mean(o * o, axis=-1, keepdims=True) + NORM_EPS) * gn
            gr = gr_ref[rows(c), vlanes(h)].astype(F32)
            o_ref[rows(c), vlanes(h)] = (y * (gr * _sigmoid(gr))).astype(BF16)


def _gla_call(p, wa, ba, gn, B, S, tc):
    nt = S // tc
    row = lambda w: pl.BlockSpec((tc, w), lambda b, t: (b * nt + t, 0))
    full = lambda a: pl.BlockSpec(a.shape, lambda b, t: (0,) * a.ndim)
    kwidth = GLA_HEADS * GLA_DK
    vwidth = GLA_HEADS * GLA_DV
    return pl.pallas_call(
        functools.partial(_gla_kernel, tc=tc),
        grid=(B, nt),
        in_specs=[row(kwidth), row(kwidth), row(vwidth), row(LANES), row(vwidth), full(wa), full(ba), full(gn)],
        out_specs=row(vwidth),
        out_shape=jax.ShapeDtypeStruct((B * S, vwidth), BF16),
        scratch_shapes=[pltpu.VMEM((GLA_HEADS, GLA_DV, LANES), F32)],
        compiler_params=pltpu.CompilerParams(dimension_semantics=("parallel", "arbitrary"),
                                             vmem_limit_bytes=VMEM_LIMIT),
        name="gla",
    )(p["gq"], p["gk"], p["gv"], p["ga"], p["gr"], wa, ba, gn)


def _out_kernel(on_ref, nzs_ref, og_ref, mgs_ref, x_ref, wn_ref, wg_ref, wo_ref, gp_ref, o_ref):
    d = x_ref.shape[1]
    gated = on_ref[...].astype(F32) * nzs_ref[...].astype(F32)
    a = jnp.dot(gated.astype(BF16), wn_ref[...], preferred_element_type=F32)
    b = jnp.dot(og_ref[...], wg_ref[...], preferred_element_type=F32)
    y = mgs_ref[:, :d].astype(F32) * a + mgs_ref[:, d:].astype(F32) * b
    out = jnp.dot(y.astype(BF16), wo_ref[...], preferred_element_type=F32)
    r = out * lax.rsqrt(jnp.mean(out * out, axis=-1, keepdims=True) + NORM_EPS)
    o_ref[...] = x_ref[...] + r * gp_ref[...]


def _out_call(o_nsa, nzs, o_gla, mgs, xf, wn, wg, wo, gp, tm):
    n, d = xf.shape
    row = lambda w: pl.BlockSpec((tm, w), lambda i: (i, 0))
    full = lambda a: pl.BlockSpec(a.shape, lambda i: (0,) * a.ndim)
    return pl.pallas_call(
        _out_kernel,
        grid=(n // tm,),
        in_specs=[row(o_nsa.shape[1]), row(nzs.shape[1]), row(o_gla.shape[1]), row(mgs.shape[1]), row(d),
                  full(wn), full(wg), full(wo), full(gp)],
        out_specs=row(d),
        out_shape=jax.ShapeDtypeStruct((n, d), F32),
        compiler_params=pltpu.CompilerParams(dimension_semantics=("parallel",), vmem_limit_bytes=VMEM_LIMIT),
        name="merge_out",
    )(o_nsa, nzs, o_gla, mgs, xf, wn, wg, wo, gp)


def _position_tables(S):
    pos = jnp.arange(S, dtype=F32)
    inv_freq = ROPE_THETA ** (-jnp.arange(0, HEAD_DIM, 2, dtype=F32) / HEAD_DIM)
    ang = pos[:, None] * inv_freq[None, :]
    cos, sin = jnp.cos(ang), jnp.sin(ang)
    cosf = jnp.concatenate([cos] * 4, axis=1)
    sinf = jnp.concatenate([-sin, sin, -sin, sin], axis=1)
    blk = np.arange(S) // SLC_BLOCK
    noh = np.zeros((S, LANES), np.float32)
    noh[np.arange(S), HEAD_DIM + blk] = -MASK_BIG
    nb, nc = S // SLC_BLOCK, S // CMP_STRIDE
    cs = CMP_STRIDE * np.arange(nc)
    bs = SLC_BLOCK * np.arange(nb)
    ov = (cs[None, :] < bs[:, None] + SLC_BLOCK) & (cs[None, :] + CMP_BLOCK > bs[:, None])
    ov[:, (S - CMP_BLOCK) // CMP_STRIDE + 1:] = False
    return cosf, sinf, jnp.asarray(noh), jnp.asarray(ov.astype(np.float32)).astype(BF16)


def _compress_weights(pos_k, w1_k, w2_k, pos_v, w1_v, w2_v):
    half = CMP_BLOCK // 2
    w1a = jnp.zeros((half, 2, NSA_GROUPS, HEAD_DIM, 2, NSA_GROUPS, CMP_HIDDEN), F32)
    w1b = jnp.zeros_like(w1a)
    for wi, w1 in enumerate((w1_k, w1_v)):
        for g in range(NSA_GROUPS):
            w1a = w1a.at[:, wi, g, :, wi, g, :].set(w1[:half])
            w1b = w1b.at[:, wi, g, :, wi, g, :].set(w1[half:])
    kdim = half * 2 * NSA_GROUPS * HEAD_DIM
    hdim = 2 * NSA_GROUPS * CMP_HIDDEN
    w1a = w1a.reshape(kdim, hdim).astype(BF16)
    w1b = w1b.reshape(kdim, hdim).astype(BF16)
    w2 = jnp.zeros((2, NSA_GROUPS, CMP_HIDDEN, 3 * LANES), F32)
    for g in range(NSA_GROUPS):
        w2 = w2.at[0, g, :, g * LANES:g * LANES + HEAD_DIM].set(w2_k)
        w2 = w2.at[1, g, :, 2 * LANES + g * HEAD_DIM:2 * LANES + (g + 1) * HEAD_DIM].set(w2_v)
    w2 = w2.reshape(hdim, 3 * LANES).astype(BF16)

    def pos_rows(sl):
        return jnp.concatenate([pos_k[sl]] * NSA_GROUPS + [pos_v[sl]] * NSA_GROUPS, axis=1)

    return pos_rows(slice(0, half)), pos_rows(slice(half, CMP_BLOCK)), w1a, w1b, w2


def _layer(xf, tabs, B, S, g_pre, w_in, cmp_pos_k, cmp_w1_k, cmp_w2_k, cmp_pos_v, cmp_w1_v, cmp_w2_v,
           gla_w_a, gla_b_a, gla_g_norm, w_up_nsa, w_up_gla, w_out, g_post):
    cosf, sinf, noh, ovt = tabs
    d = xf.shape[1]
    p = _proj_call(xf, g_pre.reshape(1, d), _pad_proj_weight(w_in), cosf, sinf, noh, S, tm=TILE)

    pa, pb, w1a, w1b, w2 = _compress_weights(cmp_pos_k, cmp_w1_k, cmp_w2_k, cmp_pos_v, cmp_w1_v, cmp_w2_v)
    kc, vct = _cmp_call(p["kcr"], p["vcr"], pa, pb, w1a, w1b, w2, B, S)
    o_nsa = _nsa_call(p, kc, vct, ovt, B, S, qt=TILE)

    wa = jnp.zeros((LANES, gla_w_a.shape[1]), F32).at[:GLA_RANK].set(gla_w_a)
    o_gla = _gla_call(p, wa, gla_b_a.reshape(1, -1), gla_g_norm.reshape(1, GLA_DV), B, S, tc=TILE)

    return _out_call(o_nsa, p["nzs"], o_gla, p["mgs"], xf, w_up_nsa.astype(BF16), w_up_gla.astype(BF16),
                     w_out.astype(BF16), g_post.reshape(1, d), tm=TILE)


def kernel(x, g_pre, w_in, cmp_pos_k, cmp_w1_k, cmp_w2_k, cmp_pos_v, cmp_w1_v, cmp_w2_v, gla_w_a, gla_b_a,
           gla_g_norm, w_up_nsa, w_up_gla, w_out, g_post):
    B, S, d = x.shape
    assert d == D_MODEL and S % 2048 == 0 and S // SLC_BLOCK <= LANES - HEAD_DIM
    tabs = _position_tables(S)
    xf = x.reshape(B * S, d)
    for l in range(g_pre.shape[0]):
        xf = _layer(xf, tabs, B, S, g_pre[l], w_in[l], cmp_pos_k[l], cmp_w1_k[l], cmp_w2_k[l],
                    cmp_pos_v[l], cmp_w1_v[l], cmp_w2_v[l], gla_w_a[l], gla_b_a[l], gla_g_norm[l],
                    w_up_nsa[l], w_up_gla[l], w_out[l], g_post[l])
    return xf.reshape(B, S, d)
```

```python
import functools

import numpy as np
import jax
import jax.numpy as jnp
from jax import lax
from jax.experimental import pallas as pl
from jax.experimental.pallas import tpu as pltpu

F32 = jnp.float32
BF16 = jnp.bfloat16

D_MODEL = 1024
NSA_HEADS = 8
NSA_GROUPS = 2
HPG = NSA_HEADS // NSA_GROUPS
HEAD_DIM = 64
CMP_BLOCK = 32
CMP_STRIDE = 16
CMP_HIDDEN = 128
SLC_BLOCK = 64
SLC_TOPK = 16
WINDOW = 512
GLA_HEADS = 4
GLA_DK = 64
GLA_DV = 128
GLA_RANK = 16
GLA_TAU = 16.0
GLA_CHUNK = 64
ROPE_THETA = 10000.0
NORM_EPS = 1e-6
QK_SCALE = HEAD_DIM ** -0.5
Q_PRESCALE = QK_SCALE * 1.4426950408889634
GLA_SCALE = GLA_DK ** -0.5

LANES = 128
MXU_COLS = 256
BF16_SUBLANES = 16
TILE = 256
ROW_BLOCK = 2 * TILE
GATE_ROWS = 32
V_ROWS = HEAD_DIM + BF16_SUBLANES
MASK_BIG = 1e30
VMEM_LIMIT = 56 * 1024 * 1024

NT_DIMS = (((1,), (1,)), ((), ()))
TN_DIMS = (((0,), (0,)), ((), ()))

_SPLITS = (("q", 512), ("kcvc", 256), ("ks", 128), ("vs", 128), ("kw", 128), ("vw", 128),
           ("ng", 24), ("nz", 512), ("gq", 256), ("gk", 256), ("gv", 512), ("ga", 16), ("gr", 512),
           ("mg", 2048))


def _proj_layout():
    src, pofs, o_src, o_dst = [], {}, 0, 0
    for name, width in _SPLITS:
        padded = -(-width // LANES) * LANES
        src.append((o_src, width, padded - width))
        pofs[name] = (o_dst, padded)
        o_src += width
        o_dst += padded
    return src, pofs, o_dst


_PSRC, _POFS, NP_COLS = _proj_layout()


def _pad_proj_weight(w):
    parts = []
    for start, width, pad in _PSRC:
        parts.append(w[:, start:start + width])
        if pad:
            parts.append(jnp.zeros((w.shape[0], pad), w.dtype))
    return jnp.concatenate(parts, axis=1).astype(BF16)


def _sigmoid(v):
    return 1.0 / (1.0 + jnp.exp(-v))


def _proj_kernel(x_ref, g_ref, w_ref, cos_ref, sin_ref, noh_ref,
                 qn_ref, qr_ref, kcr_ref, vcr_ref, ksa_ref, kwr_ref, vst_ref, vwt_ref, gate_ref, nzs_ref,
                 gq_ref, gk_ref, gv_ref, ga_ref, gr_ref, mgs_ref):
    for blk in range(x_ref.shape[0] // TILE):
        _proj_block(blk, slice(blk * TILE, (blk + 1) * TILE), x_ref, g_ref, w_ref, cos_ref, sin_ref, noh_ref,
                    qn_ref, qr_ref, kcr_ref, vcr_ref, ksa_ref, kwr_ref, vst_ref, vwt_ref, gate_ref, nzs_ref,
                    gq_ref, gk_ref, gv_ref, ga_ref, gr_ref, mgs_ref)


def _proj_block(blk, rs, x_ref, g_ref, w_ref, cos_ref, sin_ref, noh_ref,
                qn_ref, qr_ref, kcr_ref, vcr_ref, ksa_ref, kwr_ref, vst_ref, vwt_ref, gate_ref, nzs_ref,
                gq_ref, gk_ref, gv_ref, ga_ref, gr_ref, mgs_ref):
    x = x_ref[rs, :]
    y = x * lax.rsqrt(jnp.mean(x * x, axis=-1, keepdims=True) + NORM_EPS)
    h = (y * g_ref[...]).astype(BF16)
    cos = cos_ref[rs, :]
    sin = sin_ref[rs, :]
    tm = TILE
    lane = lax.broadcasted_iota(jnp.int32, (tm, LANES), 1)
    lo_half = (lane & (HEAD_DIM - 1)) < (HEAD_DIM // 2)
    first_head = lane < HEAD_DIM

    def rope(v):
        rot = jnp.where(lo_half, pltpu.roll(v, LANES - HEAD_DIM // 2, 1), pltpu.roll(v, HEAD_DIM // 2, 1))
        return v * cos + rot * sin

    windows = {}

    def chunks(name):
        start, width = _POFS[name]
        for c in range(width // LANES):
            col = start + c * LANES
            w = col // MXU_COLS
            if w not in windows:
                windows[w] = jnp.dot(h, w_ref[:, w * MXU_COLS:(w + 1) * MXU_COLS], preferred_element_type=F32)
            off = col % MXU_COLS
            yield c, windows[w][:, off:off + LANES]

    for pr, v in chunks("q"):
        n_t = (v * Q_PRESCALE).T.astype(BF16)
        r_t = (rope(v) * Q_PRESCALE).T.astype(BF16)
        for e in range(2):
            qn_ref[blk, 2 * pr + e] = n_t[e * HEAD_DIM:(e + 1) * HEAD_DIM]
            qr_ref[blk, 2 * pr + e] = r_t[e * HEAD_DIM:(e + 1) * HEAD_DIM]
    for (_, v), ref in zip(chunks("kcvc"), (kcr_ref, vcr_ref)):
        ref[rs, :] = v
    noh = noh_ref[rs, :]
    for name, ref, upper in (("ks", ksa_ref, noh), ("kw", kwr_ref, jnp.zeros_like(noh))):
        for _, v in chunks(name):
            r = rope(v)
            ref[rs, :LANES] = jnp.where(first_head, r, upper).astype(BF16)
            ref[rs, LANES:] = jnp.where(first_head, pltpu.roll(r, HEAD_DIM, 1), upper).astype(BF16)
    row16 = lax.broadcasted_iota(jnp.int32, (BF16_SUBLANES, tm), 0)
    ones_rows = jnp.where(row16 == 0, 1.0, 0.0)
    for name, ref in (("vs", vst_ref), ("vw", vwt_ref)):
        for _, v in chunks(name):
            v_t = v.T
            for g in range(NSA_GROUPS):
                ref[g, blk] = jnp.concatenate([v_t[g * HEAD_DIM:(g + 1) * HEAD_DIM], ones_rows], axis=0).astype(BF16)
    for _, v in chunks("ng"):
        gate_ref[blk] = _sigmoid(v).T[:GATE_ROWS]
    for c, v in chunks("nz"):
        nzs_ref[rs, c * LANES:(c + 1) * LANES] = (v * _sigmoid(v)).astype(BF16)
    for name, ref in (("gq", gq_ref), ("gk", gk_ref), ("gv", gv_ref), ("gr", gr_ref)):
        for c, v in chunks(name):
            ref[rs, c * LANES:(c + 1) * LANES] = v.astype(BF16)
    for _, v in chunks("ga"):
        ga_ref[rs, :] = v
    for c, v in chunks("mg"):
        mgs_ref[rs, c * LANES:(c + 1) * LANES] = _sigmoid(v).astype(BF16)


def _proj_call(xf, g_pre, w_p, cosf, sinf, noh, S, tm):
    n, d = xf.shape
    nt = n // tm
    spt = S // tm
    row = lambda width: pl.BlockSpec((tm, width), lambda i: (i, 0))
    pos = pl.BlockSpec((tm, LANES), lambda i: (i % spt, 0))
    nb_ = tm // TILE
    ntile = n // TILE
    qt_shape = ((ntile, NSA_HEADS, HEAD_DIM, TILE), (nb_, NSA_HEADS, HEAD_DIM, TILE), lambda i: (i, 0, 0, 0))
    vt_shape = ((NSA_GROUPS, ntile, V_ROWS, TILE), (NSA_GROUPS, nb_, V_ROWS, TILE), lambda i: (0, i, 0, 0))
    gt_shape = ((ntile, GATE_ROWS, TILE), (nb_, GATE_ROWS, TILE), lambda i: (i, 0, 0))
    outs = [("qn", qt_shape, BF16), ("qr", qt_shape, BF16), ("kcr", 128, F32), ("vcr", 128, F32),
            ("ksa", 256, BF16),
            ("kwr", 256, BF16), ("vst", vt_shape, BF16), ("vwt", vt_shape, BF16), ("gate", gt_shape, F32),
            ("nzs", 512, BF16), ("gq", 256, BF16), ("gk", 256, BF16), ("gv", 512, BF16),
            ("ga", 128, F32), ("gr", 512, BF16), ("mgs", 2048, BF16)]
    out_specs, out_shape = [], []
    for _, sh, dt in outs:
        if isinstance(sh, int):
            out_specs.append(row(sh))
            out_shape.append(jax.ShapeDtypeStruct((n, sh), dt))
        else:
            out_specs.append(pl.BlockSpec(sh[1], sh[2]))
            out_shape.append(jax.ShapeDtypeStruct(sh[0], dt))
    res = pl.pallas_call(
        _proj_kernel,
        grid=(nt,),
        in_specs=[row(d),
                  pl.BlockSpec((1, d), lambda i: (0, 0)),
                  pl.BlockSpec((d, NP_COLS), lambda i: (0, 0), pipeline_mode=pl.Buffered(1)),
                  pos, pos, pos],
        out_specs=out_specs,
        out_shape=out_shape,
        compiler_params=pltpu.CompilerParams(dimension_semantics=("parallel",),
                                             vmem_limit_bytes=VMEM_LIMIT),
        name="proj",
    )(xf, g_pre, w_p, cosf, sinf, noh)
    return {nm: r for (nm, _, _), r in zip(outs, res)}


def _cmp_kernel(xk_ref, xv_ref, pa_ref, pb_ref, w1a_ref, w1b_ref, w2_ref, kc_ref, vct_ref):
    nc = xk_ref.shape[0] // CMP_STRIDE
    width = xk_ref.shape[1] + xv_ref.shape[1]
    a = None
    b = None
    for l in range(CMP_STRIDE):
        tok = pl.ds(l, nc, stride=CMP_STRIDE)
        x = jnp.concatenate([xk_ref[tok, :], xv_ref[tok, :]], axis=1)
        wsl = slice(l * width, (l + 1) * width)
        da = jnp.dot((x + pa_ref[l:l + 1, :]).astype(BF16), w1a_ref[wsl, :], preferred_element_type=F32)
        db = jnp.dot((x + pb_ref[l:l + 1, :]).astype(BF16), w1b_ref[wsl, :], preferred_element_type=F32)
        a = da if a is None else a + da
        b = db if b is None else b + db
    hid = a + pltpu.roll(b, nc - 1, 0)
    hid = hid * _sigmoid(hid)
    out = jnp.dot(hid.astype(BF16), w2_ref[...], preferred_element_type=F32)
    kw = NSA_GROUPS * LANES
    kc_ref[0] = out[:, :kw].astype(BF16)
    v_t = out[:, kw:].T
    for g in range(NSA_GROUPS):
        vct_ref[0, g] = v_t[g * HEAD_DIM:(g + 1) * HEAD_DIM].astype(BF16)


def _cmp_call(kcr, vcr, pa, pb, w1a, w1b, w2, B, S):
    nc = S // CMP_STRIDE
    full = lambda a: pl.BlockSpec(a.shape, lambda b: (0,) * a.ndim)
    kw = NSA_GROUPS * LANES
    return pl.pallas_call(
        _cmp_kernel,
        grid=(B,),
        in_specs=[pl.BlockSpec((S, LANES), lambda b: (b, 0)), pl.BlockSpec((S, LANES), lambda b: (b, 0)),
                  full(pa), full(pb), full(w1a), full(w1b), full(w2)],
        out_specs=[pl.BlockSpec((1, nc, kw), lambda b: (b, 0, 0)),
                   pl.BlockSpec((1, NSA_GROUPS, HEAD_DIM, nc), lambda b: (b, 0, 0, 0))],
        out_shape=[jax.ShapeDtypeStruct((B, nc, kw), BF16),
                   jax.ShapeDtypeStruct((B, NSA_GROUPS, HEAD_DIM, nc), BF16)],
        compiler_params=pltpu.CompilerParams(dimension_semantics=("parallel",), vmem_limit_bytes=VMEM_LIMIT),
        name="compress",
    )(kcr, vcr, pa, pb, w1a, w1b, w2)


def _nsa_kernel(qn_ref, qr_ref, gate_ref, kc_ref, vct_ref, ksa_ref, vst_ref, kw_ref, vwt_ref, ovt_ref, o_ref,
                qa_sc, acc_sc, m_sc, out_sc, s_sc, smax_sc, *, qt, nb, nc):
    i = pl.program_id(1)
    s0 = i * qt
    hd = HEAD_DIM
    heads = range(NSA_HEADS)
    grp_of = lambda h: h // HPG

    def gate_row(branch, h):
        return gate_ref[0, h * 3 + branch:h * 3 + branch + 1, :]

    def lanes_of(g):
        return slice(g * LANES, (g + 1) * LANES)

    r_i = lax.broadcasted_iota(jnp.int32, (qt, qt), 0)
    c_i = lax.broadcasted_iota(jnp.int32, (qt, qt), 1)

    def reset():
        acc_sc[...] = jnp.zeros(acc_sc.shape, F32)
        m_sc[...] = jnp.full(m_sc.shape, -jnp.inf, F32)

    def scores(k, q_of, h):
        s = jnp.dot(k, q_of(h), preferred_element_type=F32)
        return s, jnp.max(s, axis=0, keepdims=True)

    def fold(h, s_and_max, vt, mask):
        s, s_max = s_and_max
        if mask is not None:
            s = jnp.where(mask, s, -MASK_BIG)
            s_max = jnp.max(s, axis=0, keepdims=True)
        m_old = m_sc[h]
        m_new = jnp.maximum(m_old, s_max)
        alpha = jnp.exp2(m_old - m_new)
        pe = jnp.exp2(s - m_new).astype(BF16)
        acc_sc[h] = alpha * acc_sc[h] + jnp.dot(vt, pe, preferred_element_type=F32)
        m_sc[h] = m_new

    def tile(pending, vt, q_of, mask, k_next):
        nxt = []
        for h in heads:
            if k_next is not None:
                nxt.append(scores(k_next[grp_of(h)], q_of, h))
            fold(h, pending[h], vt[grp_of(h)], mask)
        return tuple(nxt)

    def first_scores(k, q_of):
        return tuple(scores(k[grp_of(h)], q_of, h) for h in heads)

    def finish(branch):
        for h in heads:
            scale = gate_row(branch, h) * (1.0 / acc_sc[h, hd:hd + 1, :])
            out_sc[h * hd:(h + 1) * hd, :] += acc_sc[h, :hd, :] * scale

    def rows(j):
        return pl.ds(pl.multiple_of(j * qt, qt), qt)

    jj = lax.broadcasted_iota(jnp.int32, (nc, qt), 0)
    tt = s0 + lax.broadcasted_iota(jnp.int32, (nc, qt), 1)
    cmask = (CMP_STRIDE * jj + (CMP_BLOCK - 1)) <= tt
    kc = [kc_ref[0, :, g * LANES:g * LANES + hd] for g in range(NSA_GROUPS)]
    has_key = (s0 + lax.broadcasted_iota(jnp.int32, (1, qt), 1)) >= (CMP_BLOCK - 1)
    psum = [None] * NSA_GROUPS
    s_all = [jnp.dot(kc[grp_of(h)], qn_ref[0, h], preferred_element_type=F32) for h in heads]
    for h in heads:
        g = grp_of(h)
        s = jnp.where(cmask, s_all[h], -MASK_BIG)
        e = jnp.exp2(s - jnp.max(s, axis=0, keepdims=True))
        inv = jnp.where(has_key, 1.0 / jnp.sum(e, axis=0, keepdims=True), 0.0)
        p = e * inv
        o = jnp.dot(vct_ref[0, g], p.astype(BF16), preferred_element_type=F32)
        out_sc[h * hd:(h + 1) * hd, :] = gate_row(0, h) * o
        psum[g] = p if psum[g] is None else psum[g] + p

    ovt = ovt_ref[...]
    imp_t = []
    for g in range(NSA_GROUPS):
        hi = psum[g].astype(BF16)
        r1 = psum[g] - hi.astype(F32)
        mid = r1.astype(BF16)
        lo = (r1 - mid.astype(F32)).astype(BF16)
        imp_t.append(jnp.dot(ovt, hi, preferred_element_type=F32) + jnp.dot(ovt, mid, preferred_element_type=F32)
                     + jnp.dot(ovt, lo, preferred_element_type=F32))

    reset()
    qr_of = lambda h: qr_ref[0, h]
    d_i = r_i - c_i
    mask_a = d_i > jnp.where(i >= 2, 0, qt)
    mask_b = d_i > jnp.where(i >= 1, -qt, qt)
    ja = jnp.maximum(i - 2, 0)
    jb = jnp.maximum(i - 1, 0)
    k_win = lambda j: [kw_ref[rows(j), g * LANES:g * LANES + hd] for g in range(NSA_GROUPS)]
    v_win = lambda j: [vwt_ref[g, j] for g in range(NSA_GROUPS)]
    ka, kb, kc_w = k_win(ja), k_win(jb), k_win(i)
    carry = first_scores(ka, qr_of)
    carry = tile(carry, v_win(ja), qr_of, mask_a, kb)
    carry = tile(carry, v_win(jb), qr_of, mask_b, kc_w)
    tile(carry, v_win(i), qr_of, r_i <= c_i, None)
    finish(2)

    n_idx = lax.broadcasted_iota(jnp.int32, (nb, qt), 0)
    n_f = n_idx.astype(F32)
    cur = (s0 + lax.broadcasted_iota(jnp.int32, (nb, qt), 1)) // SLC_BLOCK
    forced = (n_idx == 0) | (n_idx == cur) | (n_idx == cur - 1)
    valid = n_idx <= cur
    free = valid & jnp.logical_not(forced)
    for g in range(NSA_GROUPS):
        val = jnp.where(free, imp_t[g], -jnp.inf)
        picked = jnp.zeros((nb, qt), F32)
        for _ in range(max(min(SLC_TOPK, nb) - 3, 0)):
            top = jnp.max(val, axis=0, keepdims=True)
            first = jnp.min(jnp.where(val == top, n_f, float(nb)), axis=0, keepdims=True)
            pick = n_f == first
            picked = jnp.where(pick, 1.0, picked)
            val = jnp.where(pick, -jnp.inf, val)
        sel = (forced | (picked > 0.0)) & valid
        notsel = jnp.where(sel, 0.0, 1.0).astype(BF16)
        if nb < LANES - hd:
            notsel = jnp.concatenate([notsel, jnp.zeros((LANES - hd - nb, qt), BF16)], axis=0)
        for h in range(g * HPG, (g + 1) * HPG):
            qa_sc[h, :hd, :] = qr_ref[0, h]
            qa_sc[h, hd:, :] = notsel

    reset()
    qa_of = lambda h: qa_sc[h]
    k_slc = lambda j: [ksa_ref[rows(j), lanes_of(g)] for g in range(NSA_GROUPS)]
    v_slc = lambda j: [vst_ref[g, j] for g in range(NSA_GROUPS)]

    def slc_scores(j, dst):
        ks = k_slc(j)
        for h in heads:
            s, s_max = scores(ks[grp_of(h)], qa_of, h)
            s_sc[dst, h] = s
            smax_sc[dst, h] = s_max

    def slc_tile(j, src, dst, mask=None):
        ks = k_slc(j + 1) if dst is not None else None
        vt = v_slc(j)
        for h in heads:
            if dst is not None:
                s, s_max = scores(ks[grp_of(h)], qa_of, h)
                s_sc[dst, h] = s
                smax_sc[dst, h] = s_max
            fold(h, (s_sc[src, h], smax_sc[src, h]), vt[grp_of(h)], mask)

    slc_scores(0, 0)
    n_pairs = i // 2

    def slc_path(first_slot, peel):
        a, b = first_slot, 1 - first_slot

        def pair(p, carry):
            j = 2 * p + peel
            slc_tile(j, a, b)
            slc_tile(j + 1, b, a)
            return carry

        lax.fori_loop(0, n_pairs, pair, 0)
        slc_tile(i, a, None, r_i <= c_i)

    @pl.when(i % 2 == 1)
    def _():
        slc_tile(0, 0, 1)
        slc_path(1, 1)

    @pl.when(i % 2 == 0)
    def _():
        slc_path(0, 0)

    finish(1)

    o_ref[...] = out_sc[...].T.astype(o_ref.dtype)


def _nsa_call(p, kc, vct, ovt, B, S, qt):
    assert WINDOW == 2 * qt
    nq = S // qt
    nb = S // SLC_BLOCK
    nc = S // CMP_STRIDE
    qspec = pl.BlockSpec((1, NSA_HEADS, HEAD_DIM, qt), lambda b, i: (b * nq + i, 0, 0, 0))
    kspec = pl.BlockSpec((S, NSA_GROUPS * LANES), lambda b, i: (b, 0))
    vspec = pl.BlockSpec((NSA_GROUPS, nq, V_ROWS, qt), lambda b, i: (0, b, 0, 0))
    kernel = functools.partial(_nsa_kernel, qt=qt, nb=nb, nc=nc)
    return pl.pallas_call(
        kernel,
        grid=(B, nq),
        in_specs=[qspec, qspec,
                  pl.BlockSpec((1, GATE_ROWS, qt), lambda b, i: (b * nq + i, 0, 0)),
                  pl.BlockSpec((1, nc, NSA_GROUPS * LANES), lambda b, i: (b, 0, 0)),
                  pl.BlockSpec((1, NSA_GROUPS, HEAD_DIM, nc), lambda b, i: (b, 0, 0, 0)),
                  kspec, vspec, kspec, vspec,
                  pl.BlockSpec(ovt.shape, lambda b, i: (0, 0))],
        out_specs=pl.BlockSpec((qt, NSA_HEADS * HEAD_DIM), lambda b, i: (b * nq + i, 0)),
        out_shape=jax.ShapeDtypeStruct((B * S, NSA_HEADS * HEAD_DIM), BF16),
        scratch_shapes=[pltpu.VMEM((NSA_HEADS, LANES, qt), BF16),
                        pltpu.VMEM((NSA_HEADS, V_ROWS, qt), F32),
                        pltpu.VMEM((NSA_HEADS, 1, qt), F32),
                        pltpu.VMEM((NSA_HEADS * HEAD_DIM, qt), F32),
                        pltpu.VMEM((2, NSA_HEADS, qt, qt), F32),
                        pltpu.VMEM((2, NSA_HEADS, 1, qt), F32)],
        compiler_params=pltpu.CompilerParams(dimension_semantics=("parallel", "arbitrary"),
                                             vmem_limit_bytes=VMEM_LIMIT),
        name="nsa",
    )(p["qn"], p["qr"], p["gate"], kc, vct, p["ksa"], p["vst"], p["kwr"], p["vwt"], ovt)


def _gla_kernel(gq_ref, gk_ref, gv_ref, ga_ref, gr_ref, wa_ref, ba_ref, gn_ref, o_ref, st_ref, *, tc):
    @pl.when(pl.program_id(1) == 0)
    def _():
        st_ref[...] = jnp.zeros(st_ref.shape, F32)

    a = jnp.dot(ga_ref[...], wa_ref[...], preferred_element_type=F32,
                precision=lax.Precision.HIGHEST) + ba_ref[...]
    log_a = (jnp.minimum(a, 0.0) - jnp.log1p(jnp.exp(-jnp.abs(a)))) * (1.0 / GLA_TAU)
    c_sz = GLA_CHUNK
    n_chunk = tc // c_sz
    chunks = range(n_chunk)
    heads = range(GLA_HEADS)
    r_t = lax.broadcasted_iota(jnp.int32, (tc, tc), 0)
    c_t = lax.broadcasted_iota(jnp.int32, (tc, tc), 1)
    tril = jnp.where((r_t >= c_t) & ((r_t // c_sz) == (c_t // c_sz)), 1.0, 0.0).astype(BF16)
    hi = log_a.astype(BF16)
    r1 = log_a - hi.astype(F32)
    mid = r1.astype(BF16)
    lo = (r1 - mid.astype(F32)).astype(BF16)
    bcum = (jnp.dot(tril, hi, preferred_element_type=F32) + jnp.dot(tril, mid, preferred_element_type=F32)
            + jnp.dot(tril, lo, preferred_element_type=F32))
    gk = gk_ref[...].astype(F32)
    qg = (gq_ref[...].astype(F32) * GLA_SCALE) * jnp.exp(bcum)
    kg = (gk * jnp.exp(-bcum)).astype(BF16)
    lane = lax.broadcasted_iota(jnp.int32, (1, LANES), 1)
    causal = (lax.broadcasted_iota(jnp.int32, (c_sz, c_sz), 0)
              >= lax.broadcasted_iota(jnp.int32, (c_sz, c_sz), 1))
    rows = lambda c: slice(c * c_sz, (c + 1) * c_sz)
    klanes = lambda h: slice((h // 2) * LANES, (h // 2 + 1) * LANES)
    vlanes = lambda h: slice(h * GLA_DV, (h + 1) * GLA_DV)
    kd, dec = [], []
    for c in chunks:
        bl = bcum[(c + 1) * c_sz - 1:(c + 1) * c_sz]
        kd.append((gk[rows(c)] * jnp.exp(bl - bcum[rows(c)])).astype(BF16))
        dec.append(jnp.exp(bl))
    qg_h, att, upd = {}, {}, {}
    for c in chunks:
        for h in heads:
            own = (lane // GLA_DK) == (h % 2)
            qg_h[c, h] = jnp.where(own, qg[rows(c), klanes(h)], 0.0).astype(BF16)
            att[c, h] = lax.dot_general(qg_h[c, h], kg[rows(c), klanes(h)], NT_DIMS, preferred_element_type=F32)
    for c in chunks:
        for h in heads:
            upd[c, h] = lax.dot_general(gv_ref[rows(c), vlanes(h)], kd[c][:, klanes(h)], TN_DIMS,
                                        preferred_element_type=F32)
    st_before = {}
    for h in heads:
        st = st_ref[h]
        for c in chunks:
            st_before[c, h] = st.astype(BF16)
            st = st * dec[c][:, klanes(h)] + upd[c, h]
        st_ref[h] = st
    gn = gn_ref[...]
    for c in chunks:
        for h in heads:
            a_c = jnp.where(causal, att[c, h], 0.0).astype(BF16)
            o = (jnp.dot(a_c, gv_ref[rows(c), vlanes(h)], preferred_element_type=F32)
                 + lax.dot_general(qg_h[c, h], st_before[c, h], NT_DIMS, preferred_element_type=F32))
            y = o * lax.rsqrt(jnp.mean(o * o, axis=-1, keepdims=True) + NORM_EPS) * gn
            gr = gr_ref[rows(c), vlanes(h)].astype(F32)
            o_ref[rows(c), vlanes(h)] = (y * (gr * _sigmoid(gr))).astype(BF16)


def _gla_call(p, wa, ba, gn, B, S, tc):
    nt = S // tc
    row = lambda w: pl.BlockSpec((tc, w), lambda b, t: (b * nt + t, 0))
    full = lambda a: pl.BlockSpec(a.shape, lambda b, t: (0,) * a.ndim)
    kwidth = GLA_HEADS * GLA_DK
    vwidth = GLA_HEADS * GLA_DV
    return pl.pallas_call(
        functools.partial(_gla_kernel, tc=tc),
        grid=(B, nt),
        in_specs=[row(kwidth), row(kwidth), row(vwidth), row(LANES), row(vwidth), full(wa), full(ba), full(gn)],
        out_specs=row(vwidth),
        out_shape=jax.ShapeDtypeStruct((B * S, vwidth), BF16),
        scratch_shapes=[pltpu.VMEM((GLA_HEADS, GLA_DV, LANES), F32)],
        compiler_params=pltpu.CompilerParams(dimension_semantics=("parallel", "arbitrary"),
                                             vmem_limit_bytes=VMEM_LIMIT),
        name="gla",
    )(p["gq"], p["gk"], p["gv"], p["ga"], p["gr"], wa, ba, gn)


def _out_kernel(on_ref, nzs_ref, og_ref, mgs_ref, x_ref, wn_ref, wg_ref, wo_ref, gp_ref, o_ref):
    d = x_ref.shape[1]
    blocks = [slice(r0, r0 + TILE) for r0 in range(0, x_ref.shape[0], TILE)]
    ups = []
    for rs in blocks:
        gated = on_ref[rs, :].astype(F32) * nzs_ref[rs, :].astype(F32)
        ups.append((jnp.dot(gated.astype(BF16), wn_ref[...], preferred_element_type=F32),
                    jnp.dot(og_ref[rs, :], wg_ref[...], preferred_element_type=F32)))
    for rs, (a, b) in zip(blocks, ups):
        y = mgs_ref[rs, :d].astype(F32) * a + mgs_ref[rs, d:].astype(F32) * b
        out = jnp.dot(y.astype(BF16), wo_ref[...], preferred_element_type=F32)
        r = out * lax.rsqrt(jnp.mean(out * out, axis=-1, keepdims=True) + NORM_EPS)
        o_ref[rs, :] = x_ref[rs, :] + r * gp_ref[...]


def _out_call(o_nsa, nzs, o_gla, mgs, xf, wn, wg, wo, gp, tm):
    n, d = xf.shape
    row = lambda w: pl.BlockSpec((tm, w), lambda i: (i, 0))
    full = lambda a: pl.BlockSpec(a.shape, lambda i: (0,) * a.ndim)
    return pl.pallas_call(
        _out_kernel,
        grid=(n // tm,),
        in_specs=[row(o_nsa.shape[1]), row(nzs.shape[1]), row(o_gla.shape[1]), row(mgs.shape[1]), row(d),
                  full(wn), full(wg), full(wo), full(gp)],
        out_specs=row(d),
        out_shape=jax.ShapeDtypeStruct((n, d), F32),
        compiler_params=pltpu.CompilerParams(dimension_semantics=("parallel",), vmem_limit_bytes=VMEM_LIMIT),
        name="merge_out",
    )(o_nsa, nzs, o_gla, mgs, xf, wn, wg, wo, gp)


def _position_tables(S):
    pos = jnp.arange(S, dtype=F32)
    inv_freq = ROPE_THETA ** (-jnp.arange(0, HEAD_DIM, 2, dtype=F32) / HEAD_DIM)
    ang = pos[:, None] * inv_freq[None, :]
    cos, sin = jnp.cos(ang), jnp.sin(ang)
    cosf = jnp.concatenate([cos] * 4, axis=1)
    sinf = jnp.concatenate([-sin, sin, -sin, sin], axis=1)
    blk = np.arange(S) // SLC_BLOCK
    noh = np.zeros((S, LANES), np.float32)
    noh[np.arange(S), HEAD_DIM + blk] = -MASK_BIG
    nb, nc = S // SLC_BLOCK, S // CMP_STRIDE
    cs = CMP_STRIDE * np.arange(nc)
    bs = SLC_BLOCK * np.arange(nb)
    ov = (cs[None, :] < bs[:, None] + SLC_BLOCK) & (cs[None, :] + CMP_BLOCK > bs[:, None])
    ov[:, (S - CMP_BLOCK) // CMP_STRIDE + 1:] = False
    return cosf, sinf, jnp.asarray(noh), jnp.asarray(ov.astype(np.float32)).astype(BF16)


def _compress_weights(pos_k, w1_k, w2_k, pos_v, w1_v, w2_v):
    half = CMP_BLOCK // 2
    w1a = jnp.zeros((half, 2, NSA_GROUPS, HEAD_DIM, 2, NSA_GROUPS, CMP_HIDDEN), F32)
    w1b = jnp.zeros_like(w1a)
    for wi, w1 in enumerate((w1_k, w1_v)):
        for g in range(NSA_GROUPS):
            w1a = w1a.at[:, wi, g, :, wi, g, :].set(w1[:half])
            w1b = w1b.at[:, wi, g, :, wi, g, :].set(w1[half:])
    kdim = half * 2 * NSA_GROUPS * HEAD_DIM
    hdim = 2 * NSA_GROUPS * CMP_HIDDEN
    w1a = w1a.reshape(kdim, hdim).astype(BF16)
    w1b = w1b.reshape(kdim, hdim).astype(BF16)
    w2 = jnp.zeros((2, NSA_GROUPS, CMP_HIDDEN, 3 * LANES), F32)
    for g in range(NSA_GROUPS):
        w2 = w2.at[0, g, :, g * LANES:g * LANES + HEAD_DIM].set(w2_k)
        w2 = w2.at[1, g, :, 2 * LANES + g * HEAD_DIM:2 * LANES + (g + 1) * HEAD_DIM].set(w2_v)
    w2 = w2.reshape(hdim, 3 * LANES).astype(BF16)

    def pos_rows(sl):
        return jnp.concatenate([pos_k[sl]] * NSA_GROUPS + [pos_v[sl]] * NSA_GROUPS, axis=1)

    return pos_rows(slice(0, half)), pos_rows(slice(half, CMP_BLOCK)), w1a, w1b, w2


def _layer(xf, tabs, B, S, g_pre, w_in, cmp_pos_k, cmp_w1_k, cmp_w2_k, cmp_pos_v, cmp_w1_v, cmp_w2_v,
           gla_w_a, gla_b_a, gla_g_norm, w_up_nsa, w_up_gla, w_out, g_post):
    cosf, sinf, noh, ovt = tabs
    d = xf.shape[1]
    p = _proj_call(xf, g_pre.reshape(1, d), _pad_proj_weight(w_in), cosf, sinf, noh, S, tm=ROW_BLOCK)

    pa, pb, w1a, w1b, w2 = _compress_weights(cmp_pos_k, cmp_w1_k, cmp_w2_k, cmp_pos_v, cmp_w1_v, cmp_w2_v)
    kc, vct = _cmp_call(p["kcr"], p["vcr"], pa, pb, w1a, w1b, w2, B, S)
    o_nsa = _nsa_call(p, kc, vct, ovt, B, S, qt=TILE)

    wa = jnp.zeros((LANES, gla_w_a.shape[1]), F32).at[:GLA_RANK].set(gla_w_a)
    o_gla = _gla_call(p, wa, gla_b_a.reshape(1, -1), gla_g_norm.reshape(1, GLA_DV), B, S, tc=TILE)

    return _out_call(o_nsa, p["nzs"], o_gla, p["mgs"], xf, w_up_nsa.astype(BF16), w_up_gla.astype(BF16),
                     w_out.astype(BF16), g_post.reshape(1, d), tm=ROW_BLOCK)


def kernel(x, g_pre, w_in, cmp_pos_k, cmp_w1_k, cmp_w2_k, cmp_pos_v, cmp_w1_v, cmp_w2_v, gla_w_a, gla_b_a,
           gla_g_norm, w_up_nsa, w_up_gla, w_out, g_post):
    B, S, d = x.shape
    assert d == D_MODEL and S % 2048 == 0 and S // SLC_BLOCK <= LANES - HEAD_DIM
    tabs = _position_tables(S)
    xf = x.reshape(B * S, d)
    for l in range(g_pre.shape[0]):
        xf = _layer(xf, tabs, B, S, g_pre[l], w_in[l], cmp_pos_k[l], cmp_w1_k[l], cmp_w2_k[l],
                    cmp_pos_v[l], cmp_w1_v[l], cmp_w2_v[l], gla_w_a[l], gla_b_a[l], gla_g_norm[l],
                    w_up_nsa[l], w_up_gla[l], w_out[l], g_post[l])
    return xf.reshape(B, S, d)
```

```python
import functools

import numpy as np
import jax
import jax.numpy as jnp
from jax import lax
from jax.experimental import pallas as pl
from jax.experimental.pallas import tpu as pltpu

F32 = jnp.float32
BF16 = jnp.bfloat16

D_MODEL = 1024
NSA_HEADS = 8
NSA_GROUPS = 2
HPG = NSA_HEADS // NSA_GROUPS
HEAD_DIM = 64
CMP_BLOCK = 32
CMP_STRIDE = 16
CMP_HIDDEN = 128
SLC_BLOCK = 64
SLC_TOPK = 16
WINDOW = 512
GLA_HEADS = 4
GLA_DK = 64
GLA_DV = 128
GLA_RANK = 16
GLA_TAU = 16.0
GLA_CHUNK = 64
ROPE_THETA = 10000.0
NORM_EPS = 1e-6
QK_SCALE = HEAD_DIM ** -0.5
Q_PRESCALE = QK_SCALE * 1.4426950408889634
GLA_SCALE = GLA_DK ** -0.5

LANES = 128
MXU_COLS = 256
BF16_SUBLANES = 16
TILE = 256
ROW_BLOCK = 2 * TILE
GATE_ROWS = 32
V_ROWS = HEAD_DIM + BF16_SUBLANES
MASK_BIG = 1e30
VMEM_LIMIT = 56 * 1024 * 1024

NT_DIMS = (((1,), (1,)), ((), ()))
TN_DIMS = (((0,), (0,)), ((), ()))

_SPLITS = (("q", 512), ("kcvc", 256), ("ks", 128), ("vs", 128), ("kw", 128), ("vw", 128),
           ("ng", 24), ("nz", 512), ("gq", 256), ("gk", 256), ("gv", 512), ("ga", 16), ("gr", 512),
           ("mg", 2048))


def _proj_layout():
    src, pofs, o_src, o_dst = [], {}, 0, 0
    for name, width in _SPLITS:
        padded = -(-width // LANES) * LANES
        src.append((o_src, width, padded - width))
        pofs[name] = (o_dst, padded)
        o_src += width
        o_dst += padded
    return src, pofs, o_dst


_PSRC, _POFS, NP_COLS = _proj_layout()


def _pad_proj_weight(w):
    parts = []
    for start, width, pad in _PSRC:
        parts.append(w[:, start:start + width])
        if pad:
            parts.append(jnp.zeros((w.shape[0], pad), w.dtype))
    return jnp.concatenate(parts, axis=1).astype(BF16)


def _sigmoid(v):
    return 1.0 / (1.0 + jnp.exp(-v))


def _proj_kernel(x_ref, g_ref, w_ref, cos_ref, sin_ref, noh_ref,
                 qn_ref, qr_ref, kcr_ref, vcr_ref, ksa_ref, kwr_ref, vst_ref, vwt_ref, gate_ref, nzs_ref,
                 gq_ref, gk_ref, gv_ref, ga_ref, gr_ref, mgs_ref):
    for blk in range(x_ref.shape[0] // TILE):
        _proj_block(blk, slice(blk * TILE, (blk + 1) * TILE), x_ref, g_ref, w_ref, cos_ref, sin_ref, noh_ref,
                    qn_ref, qr_ref, kcr_ref, vcr_ref, ksa_ref, kwr_ref, vst_ref, vwt_ref, gate_ref, nzs_ref,
                    gq_ref, gk_ref, gv_ref, ga_ref, gr_ref, mgs_ref)


def _proj_block(blk, rs, x_ref, g_ref, w_ref, cos_ref, sin_ref, noh_ref,
                qn_ref, qr_ref, kcr_ref, vcr_ref, ksa_ref, kwr_ref, vst_ref, vwt_ref, gate_ref, nzs_ref,
                gq_ref, gk_ref, gv_ref, ga_ref, gr_ref, mgs_ref):
    x = x_ref[rs, :]
    y = x * lax.rsqrt(jnp.mean(x * x, axis=-1, keepdims=True) + NORM_EPS)
    h = (y * g_ref[...]).astype(BF16)
    cos = cos_ref[rs, :]
    sin = sin_ref[rs, :]
    tm = TILE
    lane = lax.broadcasted_iota(jnp.int32, (tm, LANES), 1)
    lo_half = (lane & (HEAD_DIM - 1)) < (HEAD_DIM // 2)
    first_head = lane < HEAD_DIM

    def rope(v):
        rot = jnp.where(lo_half, pltpu.roll(v, LANES - HEAD_DIM // 2, 1), pltpu.roll(v, HEAD_DIM // 2, 1))
        return v * cos + rot * sin

    windows = {}

    def chunks(name):
        start, width = _POFS[name]
        for c in range(width // LANES):
            col = start + c * LANES
            w = col // MXU_COLS
            if w not in windows:
                windows[w] = jnp.dot(h, w_ref[:, w * MXU_COLS:(w + 1) * MXU_COLS], preferred_element_type=F32)
            off = col % MXU_COLS
            yield c, windows[w][:, off:off + LANES]

    for pr, v in chunks("q"):
        n_t = (v * Q_PRESCALE).T.astype(BF16)
        r_t = (rope(v) * Q_PRESCALE).T.astype(BF16)
        for e in range(2):
            qn_ref[blk, 2 * pr + e] = n_t[e * HEAD_DIM:(e + 1) * HEAD_DIM]
            qr_ref[blk, 2 * pr + e] = r_t[e * HEAD_DIM:(e + 1) * HEAD_DIM]
    for (_, v), ref in zip(chunks("kcvc"), (kcr_ref, vcr_ref)):
        ref[rs, :] = v
    noh = noh_ref[rs, :]
    for name, ref, upper in (("ks", ksa_ref, noh), ("kw", kwr_ref, jnp.zeros_like(noh))):
        for _, v in chunks(name):
            r = rope(v)
            ref[rs, :LANES] = jnp.where(first_head, r, upper).astype(BF16)
            ref[rs, LANES:] = jnp.where(first_head, pltpu.roll(r, HEAD_DIM, 1), upper).astype(BF16)
    row16 = lax.broadcasted_iota(jnp.int32, (BF16_SUBLANES, tm), 0)
    ones_rows = jnp.where(row16 == 0, 1.0, 0.0)
    for name, ref in (("vs", vst_ref), ("vw", vwt_ref)):
        for _, v in chunks(name):
            v_t = v.T
            for g in range(NSA_GROUPS):
                ref[g, blk] = jnp.concatenate([v_t[g * HEAD_DIM:(g + 1) * HEAD_DIM], ones_rows], axis=0).astype(BF16)
    for _, v in chunks("ng"):
        gate_ref[blk] = _sigmoid(v).T[:GATE_ROWS]
    for c, v in chunks("nz"):
        nzs_ref[rs, c * LANES:(c + 1) * LANES] = (v * _sigmoid(v)).astype(BF16)
    for name, ref in (("gq", gq_ref), ("gk", gk_ref), ("gv", gv_ref), ("gr", gr_ref)):
        for c, v in chunks(name):
            ref[rs, c * LANES:(c + 1) * LANES] = v.astype(BF16)
    for _, v in chunks("ga"):
        ga_ref[rs, :] = v
    for c, v in chunks("mg"):
        mgs_ref[rs, c * LANES:(c + 1) * LANES] = _sigmoid(v).astype(BF16)


def _proj_call(xf, g_pre, w_p, cosf, sinf, noh, S, tm):
    n, d = xf.shape
    nt = n // tm
    spt = S // tm
    row = lambda width: pl.BlockSpec((tm, width), lambda i: (i, 0))
    pos = pl.BlockSpec((tm, LANES), lambda i: (i % spt, 0))
    nb_ = tm // TILE
    ntile = n // TILE
    qt_shape = ((ntile, NSA_HEADS, HEAD_DIM, TILE), (nb_, NSA_HEADS, HEAD_DIM, TILE), lambda i: (i, 0, 0, 0))
    vt_shape = ((NSA_GROUPS, ntile, V_ROWS, TILE), (NSA_GROUPS, nb_, V_ROWS, TILE), lambda i: (0, i, 0, 0))
    gt_shape = ((ntile, GATE_ROWS, TILE), (nb_, GATE_ROWS, TILE), lambda i: (i, 0, 0))
    outs = [("qn", qt_shape, BF16), ("qr", qt_shape, BF16), ("kcr", 128, F32), ("vcr", 128, F32),
            ("ksa", 256, BF16),
            ("kwr", 256, BF16), ("vst", vt_shape, BF16), ("vwt", vt_shape, BF16), ("gate", gt_shape, F32),
            ("nzs", 512, BF16), ("gq", 256, BF16), ("gk", 256, BF16), ("gv", 512, BF16),
            ("ga", 128, F32), ("gr", 512, BF16), ("mgs", 2048, BF16)]
    out_specs, out_shape = [], []
    for _, sh, dt in outs:
        if isinstance(sh, int):
            out_specs.append(row(sh))
            out_shape.append(jax.ShapeDtypeStruct((n, sh), dt))
        else:
            out_specs.append(pl.BlockSpec(sh[1], sh[2]))
            out_shape.append(jax.ShapeDtypeStruct(sh[0], dt))
    res = pl.pallas_call(
        _proj_kernel,
        grid=(nt,),
        in_specs=[row(d),
                  pl.BlockSpec((1, d), lambda i: (0, 0)),
                  pl.BlockSpec((d, NP_COLS), lambda i: (0, 0), pipeline_mode=pl.Buffered(1)),
                  pos, pos, pos],
        out_specs=out_specs,
        out_shape=out_shape,
        compiler_params=pltpu.CompilerParams(dimension_semantics=("parallel",),
                                             vmem_limit_bytes=VMEM_LIMIT),
        name="proj",
    )(xf, g_pre, w_p, cosf, sinf, noh)
    return {nm: r for (nm, _, _), r in zip(outs, res)}


def _cmp_kernel(xk_ref, xv_ref, pk_ref, pv_ref, w1k_ref, w1v_ref, w2k_ref, w2v_ref, kc_ref, vct_ref):
    nc = xk_ref.shape[0] // CMP_STRIDE

    def hidden(x_ref, p_ref, w1_ref):
        a = None
        b = None
        for l in range(CMP_STRIDE):
            x = x_ref[pl.ds(l, nc, stride=CMP_STRIDE), :]
            l2 = CMP_STRIDE + l
            da = jnp.dot((x + p_ref[l:l + 1, :]).astype(BF16), w1_ref[l], preferred_element_type=F32)
            db = jnp.dot((x + p_ref[l2:l2 + 1, :]).astype(BF16), w1_ref[l2], preferred_element_type=F32)
            a = da if a is None else a + da
            b = db if b is None else b + db
        hid = a + pltpu.roll(b, nc - 1, 0)
        return (hid * _sigmoid(hid)).astype(BF16)

    kc_ref[0] = jnp.dot(hidden(xk_ref, pk_ref, w1k_ref), w2k_ref[...], preferred_element_type=F32).astype(BF16)
    v_t = jnp.dot(hidden(xv_ref, pv_ref, w1v_ref), w2v_ref[...], preferred_element_type=F32).T
    for g in range(NSA_GROUPS):
        vct_ref[0, g] = v_t[g * HEAD_DIM:(g + 1) * HEAD_DIM].astype(BF16)


def _cmp_call(kcr, vcr, weights, B, S):
    nc = S // CMP_STRIDE
    full = lambda a: pl.BlockSpec(a.shape, lambda b: (0,) * a.ndim)
    kw = NSA_GROUPS * LANES
    return pl.pallas_call(
        _cmp_kernel,
        grid=(B,),
        in_specs=[pl.BlockSpec((S, LANES), lambda b: (b, 0)), pl.BlockSpec((S, LANES), lambda b: (b, 0))]
        + [full(w) for w in weights],
        out_specs=[pl.BlockSpec((1, nc, kw), lambda b: (b, 0, 0)),
                   pl.BlockSpec((1, NSA_GROUPS, HEAD_DIM, nc), lambda b: (b, 0, 0, 0))],
        out_shape=[jax.ShapeDtypeStruct((B, nc, kw), BF16),
                   jax.ShapeDtypeStruct((B, NSA_GROUPS, HEAD_DIM, nc), BF16)],
        compiler_params=pltpu.CompilerParams(dimension_semantics=("parallel",), vmem_limit_bytes=VMEM_LIMIT),
        name="compress",
    )(kcr, vcr, *weights)


def _nsa_kernel(qn_ref, qr_ref, gate_ref, kc_ref, vct_ref, ksa_ref, vst_ref, kw_ref, vwt_ref, ovt_ref, o_ref,
                qa_sc, acc_sc, m_sc, out_sc, s_sc, smax_sc, *, qt, nb, nc):
    i = pl.program_id(1)
    s0 = i * qt
    hd = HEAD_DIM
    heads = range(NSA_HEADS)
    grp_of = lambda h: h // HPG

    def gate_row(branch, h):
        return gate_ref[0, h * 3 + branch:h * 3 + branch + 1, :]

    def lanes_of(g):
        return slice(g * LANES, (g + 1) * LANES)

    r_i = lax.broadcasted_iota(jnp.int32, (qt, qt), 0)
    c_i = lax.broadcasted_iota(jnp.int32, (qt, qt), 1)

    def reset():
        acc_sc[...] = jnp.zeros(acc_sc.shape, F32)
        m_sc[...] = jnp.full(m_sc.shape, -jnp.inf, F32)

    def scores(k, q_of, h):
        s = jnp.dot(k, q_of(h), preferred_element_type=F32)
        return s, jnp.max(s, axis=0, keepdims=True)

    def fold(h, s_and_max, vt, mask, bias=None):
        s, s_max = s_and_max
        if mask is not None:
            s = jnp.where(mask, s, -MASK_BIG)
            s_max = jnp.max(s, axis=0, keepdims=True)
        if bias is not None:
            s_max = s_max + bias
        m_old = m_sc[h]
        m_new = jnp.maximum(m_old, s_max)
        alpha = jnp.exp2(m_old - m_new)
        pe = jnp.exp2(s - (m_new if bias is None else m_new - bias)).astype(BF16)
        acc_sc[h] = alpha * acc_sc[h] + jnp.dot(vt, pe, preferred_element_type=F32)
        m_sc[h] = m_new

    def tile(pending, vt, q_of, mask, k_next, bias=None):
        nxt = []
        for h in heads:
            if k_next is not None:
                nxt.append(scores(k_next[grp_of(h)], q_of, h))
            fold(h, pending[h], vt[grp_of(h)], mask, bias)
        return tuple(nxt)

    def first_scores(k, q_of):
        return tuple(scores(k[grp_of(h)], q_of, h) for h in heads)

    def finish(branch):
        for h in heads:
            scale = gate_row(branch, h) * (1.0 / acc_sc[h, hd:hd + 1, :])
            out_sc[h * hd:(h + 1) * hd, :] += acc_sc[h, :hd, :] * scale

    def rows(j):
        return pl.ds(pl.multiple_of(j * qt, qt), qt)

    jj = lax.broadcasted_iota(jnp.int32, (nc, qt), 0)
    tt = s0 + lax.broadcasted_iota(jnp.int32, (nc, qt), 1)
    cmask = (CMP_STRIDE * jj + (CMP_BLOCK - 1)) <= tt
    kc = [kc_ref[0, :, g * LANES:g * LANES + hd] for g in range(NSA_GROUPS)]
    has_key = (s0 + lax.broadcasted_iota(jnp.int32, (1, qt), 1)) >= (CMP_BLOCK - 1)
    psum = [None] * NSA_GROUPS
    s_all = [jnp.dot(kc[grp_of(h)], qn_ref[0, h], preferred_element_type=F32) for h in heads]
    for h in heads:
        g = grp_of(h)
        s = jnp.where(cmask, s_all[h], -MASK_BIG)
        e = jnp.exp2(s - jnp.max(s, axis=0, keepdims=True))
        inv = jnp.where(has_key, 1.0 / jnp.sum(e, axis=0, keepdims=True), 0.0)
        p = e * inv
        o = jnp.dot(vct_ref[0, g], p.astype(BF16), preferred_element_type=F32)
        out_sc[h * hd:(h + 1) * hd, :] = gate_row(0, h) * o
        psum[g] = p if psum[g] is None else psum[g] + p

    ovt = ovt_ref[...]
    imp_t = []
    for g in range(NSA_GROUPS):
        hi = psum[g].astype(BF16)
        r1 = psum[g] - hi.astype(F32)
        mid = r1.astype(BF16)
        lo = (r1 - mid.astype(F32)).astype(BF16)
        imp_t.append(jnp.dot(ovt, hi, preferred_element_type=F32) + jnp.dot(ovt, mid, preferred_element_type=F32)
                     + jnp.dot(ovt, lo, preferred_element_type=F32))

    reset()
    qr_of = lambda h: qr_ref[0, h]
    mask_a = (r_i - c_i) > jnp.where(i >= 2, 0, qt)
    bias_b = jnp.where(i >= 1, 0.0, -MASK_BIG)
    ja = jnp.maximum(i - 2, 0)
    jb = jnp.maximum(i - 1, 0)
    k_win = lambda j: [kw_ref[rows(j), g * LANES:g * LANES + hd] for g in range(NSA_GROUPS)]
    v_win = lambda j: [vwt_ref[g, j] for g in range(NSA_GROUPS)]
    kb, ka = k_win(jb), k_win(ja)
    carry = first_scores(k_win(i), qr_of)
    carry = tile(carry, v_win(i), qr_of, r_i <= c_i, kb)
    carry = tile(carry, v_win(jb), qr_of, None, ka, bias=bias_b)
    tile(carry, v_win(ja), qr_of, mask_a, None)
    finish(2)

    n_idx = lax.broadcasted_iota(jnp.int32, (nb, qt), 0)
    n_f = n_idx.astype(F32)
    cur = (s0 + lax.broadcasted_iota(jnp.int32, (nb, qt), 1)) // SLC_BLOCK
    forced = (n_idx == 0) | (n_idx == cur) | (n_idx == cur - 1)
    valid = n_idx <= cur
    free = valid & jnp.logical_not(forced)
    topk = min(SLC_TOPK, nb)
    for h in heads:
        qa_sc[h, :hd, :] = qr_ref[0, h]

    def put_selection(g, sel):
        notsel = jnp.where(sel, 0.0, 1.0).astype(BF16)
        if nb < LANES - hd:
            notsel = jnp.concatenate([notsel, jnp.zeros((LANES - hd - nb, qt), BF16)], axis=0)
        for h in range(g * HPG, (g + 1) * HPG):
            qa_sc[h, hd:, :] = notsel

    few_blocks = (s0 + qt) // SLC_BLOCK <= topk

    @pl.when(few_blocks)
    def _():
        for g in range(NSA_GROUPS):
            put_selection(g, valid)

    @pl.when(jnp.logical_not(few_blocks))
    def _():
        for g in range(NSA_GROUPS):
            val = jnp.where(free, imp_t[g], -jnp.inf)
            picked = jnp.zeros((nb, qt), F32)
            for _ in range(max(topk - 3, 0)):
                top = jnp.max(val, axis=0, keepdims=True)
                first = jnp.min(jnp.where(val == top, n_f, float(nb)), axis=0, keepdims=True)
                pick = n_f == first
                picked = jnp.where(pick, 1.0, picked)
                val = jnp.where(pick, -jnp.inf, val)
            put_selection(g, (forced | (picked > 0.0)) & valid)

    reset()
    qa_of = lambda h: qa_sc[h]
    k_slc = lambda j: [ksa_ref[rows(j), lanes_of(g)] for g in range(NSA_GROUPS)]
    v_slc = lambda j: [vst_ref[g, j] for g in range(NSA_GROUPS)]

    def slc_scores(j, dst):
        ks = k_slc(j)
        for h in heads:
            s, s_max = scores(ks[grp_of(h)], qa_of, h)
            s_sc[dst, h] = s
            smax_sc[dst, h] = s_max

    def slc_tile(j, src, dst, mask=None):
        ks = k_slc(j + 1) if dst is not None else None
        vt = v_slc(j)
        for h in heads:
            if dst is not None:
                s, s_max = scores(ks[grp_of(h)], qa_of, h)
                s_sc[dst, h] = s
                smax_sc[dst, h] = s_max
            fold(h, (s_sc[src, h], smax_sc[src, h]), vt[grp_of(h)], mask)

    slc_scores(0, 0)
    n_pairs = i // 2

    def slc_path(first_slot, peel):
        a, b = first_slot, 1 - first_slot

        def pair(p, carry):
            j = 2 * p + peel
            slc_tile(j, a, b)
            slc_tile(j + 1, b, a)
            return carry

        lax.fori_loop(0, n_pairs, pair, 0)
        slc_tile(i, a, None, r_i <= c_i)

    @pl.when(i % 2 == 1)
    def _():
        slc_tile(0, 0, 1)
        slc_path(1, 1)

    @pl.when(i % 2 == 0)
    def _():
        slc_path(0, 0)

    finish(1)

    o_ref[...] = out_sc[...].T.astype(o_ref.dtype)


def _nsa_call(p, kc, vct, ovt, B, S, qt):
    assert WINDOW == 2 * qt
    nq = S // qt
    nb = S // SLC_BLOCK
    nc = S // CMP_STRIDE
    qspec = pl.BlockSpec((1, NSA_HEADS, HEAD_DIM, qt), lambda b, i: (b * nq + i, 0, 0, 0))
    kspec = pl.BlockSpec((S, NSA_GROUPS * LANES), lambda b, i: (b, 0))
    vspec = pl.BlockSpec((NSA_GROUPS, nq, V_ROWS, qt), lambda b, i: (0, b, 0, 0))
    kernel = functools.partial(_nsa_kernel, qt=qt, nb=nb, nc=nc)
    return pl.pallas_call(
        kernel,
        grid=(B, nq),
        in_specs=[qspec, qspec,
                  pl.BlockSpec((1, GATE_ROWS, qt), lambda b, i: (b * nq + i, 0, 0)),
                  pl.BlockSpec((1, nc, NSA_GROUPS * LANES), lambda b, i: (b, 0, 0)),
                  pl.BlockSpec((1, NSA_GROUPS, HEAD_DIM, nc), lambda b, i: (b, 0, 0, 0)),
                  kspec, vspec, kspec, vspec,
                  pl.BlockSpec(ovt.shape, lambda b, i: (0, 0))],
        out_specs=pl.BlockSpec((qt, NSA_HEADS * HEAD_DIM), lambda b, i: (b * nq + i, 0)),
        out_shape=jax.ShapeDtypeStruct((B * S, NSA_HEADS * HEAD_DIM), BF16),
        scratch_shapes=[pltpu.VMEM((NSA_HEADS, LANES, qt), BF16),
                        pltpu.VMEM((NSA_HEADS, V_ROWS, qt), F32),
                        pltpu.VMEM((NSA_HEADS, 1, qt), F32),
                        pltpu.VMEM((NSA_HEADS * HEAD_DIM, qt), F32),
                        pltpu.VMEM((2, NSA_HEADS, qt, qt), F32),
                        pltpu.VMEM((2, NSA_HEADS, 1, qt), F32)],
        compiler_params=pltpu.CompilerParams(dimension_semantics=("parallel", "arbitrary"),
                                             vmem_limit_bytes=VMEM_LIMIT),
        name="nsa",
    )(p["qn"], p["qr"], p["gate"], kc, vct, p["ksa"], p["vst"], p["kwr"], p["vwt"], ovt)


def _gla_kernel(gq_ref, gk_ref, gv_ref, ga_ref, gr_ref, wa_ref, ba_ref, gn_ref, o_ref, st_ref, *, tc):
    @pl.when(pl.program_id(1) == 0)
    def _():
        st_ref[...] = jnp.zeros(st_ref.shape, F32)

    a = jnp.dot(ga_ref[...], wa_ref[...], preferred_element_type=F32,
                precision=lax.Precision.HIGHEST) + ba_ref[...]
    log_a = (jnp.minimum(a, 0.0) - jnp.log1p(jnp.exp(-jnp.abs(a)))) * (1.0 / GLA_TAU)
    c_sz = GLA_CHUNK
    n_chunk = tc // c_sz
    chunks = range(n_chunk)
    heads = range(GLA_HEADS)
    r_t = lax.broadcasted_iota(jnp.int32, (tc, tc), 0)
    c_t = lax.broadcasted_iota(jnp.int32, (tc, tc), 1)
    tril = jnp.where((r_t >= c_t) & ((r_t // c_sz) == (c_t // c_sz)), 1.0, 0.0).astype(BF16)
    hi = log_a.astype(BF16)
    r1 = log_a - hi.astype(F32)
    mid = r1.astype(BF16)
    lo = (r1 - mid.astype(F32)).astype(BF16)
    bcum = (jnp.dot(tril, hi, preferred_element_type=F32) + jnp.dot(tril, mid, preferred_element_type=F32)
            + jnp.dot(tril, lo, preferred_element_type=F32))
    gk = gk_ref[...].astype(F32)
    qg = (gq_ref[...].astype(F32) * GLA_SCALE) * jnp.exp(bcum)
    kg = (gk * jnp.exp(-bcum)).astype(BF16)
    lane = lax.broadcasted_iota(jnp.int32, (1, LANES), 1)
    causal = (lax.broadcasted_iota(jnp.int32, (c_sz, c_sz), 0)
              >= lax.broadcasted_iota(jnp.int32, (c_sz, c_sz), 1))
    rows = lambda c: slice(c * c_sz, (c + 1) * c_sz)
    klanes = lambda h: slice((h // 2) * LANES, (h // 2 + 1) * LANES)
    vlanes = lambda h: slice(h * GLA_DV, (h + 1) * GLA_DV)
    kd, dec = [], []
    for c in chunks:
        bl = bcum[(c + 1) * c_sz - 1:(c + 1) * c_sz]
        kd.append((gk[rows(c)] * jnp.exp(bl - bcum[rows(c)])).astype(BF16))
        dec.append(jnp.exp(bl))
    qg_h, att, upd = {}, {}, {}
    for c in chunks:
        for h in heads:
            own = (lane // GLA_DK) == (h % 2)
            qg_h[c, h] = jnp.where(own, qg[rows(c), klanes(h)], 0.0).astype(BF16)
            att[c, h] = lax.dot_general(qg_h[c, h], kg[rows(c), klanes(h)], NT_DIMS, preferred_element_type=F32)
    for c in chunks:
        for h in heads:
            upd[c, h] = lax.dot_general(gv_ref[rows(c), vlanes(h)], kd[c][:, klanes(h)], TN_DIMS,
                                        preferred_element_type=F32)
    st_before = {}
    for h in heads:
        st = st_ref[h]
        for c in chunks:
            st_before[c, h] = st.astype(BF16)
            st = st * dec[c][:, klanes(h)] + upd[c, h]
        st_ref[h] = st
    gn = gn_ref[...]
    for c in chunks:
        for h in heads:
            a_c = jnp.where(causal, att[c, h], 0.0).astype(BF16)
            o = (jnp.dot(a_c, gv_ref[rows(c), vlanes(h)], preferred_element_type=F32)
                 + lax.dot_general(qg_h[c, h], st_before[c, h], NT_DIMS, preferred_element_type=F32))
            y = o * lax.rsqrt(jnp.mean(o * o, axis=-1, keepdims=True) + NORM_EPS) * gn
            gr = gr_ref[rows(c), vlanes(h)].astype(F32)
            o_ref[rows(c), vlanes(h)] = (y * (gr * _sigmoid(gr))).astype(BF16)


def _gla_call(p, wa, ba, gn, B, S, tc):
    nt = S // tc
    row = lambda w: pl.BlockSpec((tc, w), lambda b, t: (b * nt + t, 0))
    full = lambda a: pl.BlockSpec(a.shape, lambda b, t: (0,) * a.ndim)
    kwidth = GLA_HEADS * GLA_DK
    vwidth = GLA_HEADS * GLA_DV
    return pl.pallas_call(
        functools.partial(_gla_kernel, tc=tc),
        grid=(B, nt),
        in_specs=[row(kwidth), row(kwidth), row(vwidth), row(LANES), row(vwidth), full(wa), full(ba), full(gn)],
        out_specs=row(vwidth),
        out_shape=jax.ShapeDtypeStruct((B * S, vwidth), BF16),
        scratch_shapes=[pltpu.VMEM((GLA_HEADS, GLA_DV, LANES), F32)],
        compiler_params=pltpu.CompilerParams(dimension_semantics=("parallel", "arbitrary"),
                                             vmem_limit_bytes=VMEM_LIMIT),
        name="gla",
    )(p["gq"], p["gk"], p["gv"], p["ga"], p["gr"], wa, ba, gn)


def _out_kernel(on_ref, nzs_ref, og_ref, mgs_ref, x_ref, wn_ref, wg_ref, wo_ref, gp_ref, o_ref):
    d = x_ref.shape[1]
    blocks = [slice(r0, r0 + TILE) for r0 in range(0, x_ref.shape[0], TILE)]
    ups = []
    for rs in blocks:
        gated = on_ref[rs, :].astype(F32) * nzs_ref[rs, :].astype(F32)
        ups.append((jnp.dot(gated.astype(BF16), wn_ref[...], preferred_element_type=F32),
                    jnp.dot(og_ref[rs, :], wg_ref[...], preferred_element_type=F32)))
    for rs, (a, b) in zip(blocks, ups):
        y = mgs_ref[rs, :d].astype(F32) * a + mgs_ref[rs, d:].astype(F32) * b
        out = jnp.dot(y.astype(BF16), wo_ref[...], preferred_element_type=F32)
        r = out * lax.rsqrt(jnp.mean(out * out, axis=-1, keepdims=True) + NORM_EPS)
        o_ref[rs, :] = x_ref[rs, :] + r * gp_ref[...]


def _out_call(o_nsa, nzs, o_gla, mgs, xf, wn, wg, wo, gp, tm):
    n, d = xf.shape
    row = lambda w: pl.BlockSpec((tm, w), lambda i: (i, 0))
    full = lambda a: pl.BlockSpec(a.shape, lambda i: (0,) * a.ndim)
    return pl.pallas_call(
        _out_kernel,
        grid=(n // tm,),
        in_specs=[row(o_nsa.shape[1]), row(nzs.shape[1]), row(o_gla.shape[1]), row(mgs.shape[1]), row(d),
                  full(wn), full(wg), full(wo), full(gp)],
        out_specs=row(d),
        out_shape=jax.ShapeDtypeStruct((n, d), F32),
        compiler_params=pltpu.CompilerParams(dimension_semantics=("parallel",), vmem_limit_bytes=VMEM_LIMIT),
        name="merge_out",
    )(o_nsa, nzs, o_gla, mgs, xf, wn, wg, wo, gp)


def _position_tables(S):
    pos = jnp.arange(S, dtype=F32)
    inv_freq = ROPE_THETA ** (-jnp.arange(0, HEAD_DIM, 2, dtype=F32) / HEAD_DIM)
    ang = pos[:, None] * inv_freq[None, :]
    cos, sin = jnp.cos(ang), jnp.sin(ang)
    cosf = jnp.concatenate([cos] * 4, axis=1)
    sinf = jnp.concatenate([-sin, sin, -sin, sin], axis=1)
    blk = np.arange(S) // SLC_BLOCK
    noh = np.zeros((S, LANES), np.float32)
    noh[np.arange(S), HEAD_DIM + blk] = -MASK_BIG
    nb, nc = S // SLC_BLOCK, S // CMP_STRIDE
    cs = CMP_STRIDE * np.arange(nc)
    bs = SLC_BLOCK * np.arange(nb)
    ov = (cs[None, :] < bs[:, None] + SLC_BLOCK) & (cs[None, :] + CMP_BLOCK > bs[:, None])
    ov[:, (S - CMP_BLOCK) // CMP_STRIDE + 1:] = False
    return cosf, sinf, jnp.asarray(noh), jnp.asarray(ov.astype(np.float32)).astype(BF16)


def _compress_weights(pos_k, w1_k, w2_k, pos_v, w1_v, w2_v):
    assert NSA_GROUPS == 2

    def block_diag2(w):
        z = jnp.zeros_like(w)
        return jnp.concatenate([jnp.concatenate([w, z], axis=-1), jnp.concatenate([z, w], axis=-1)], axis=-2)

    pad_k = jnp.concatenate([w2_k, jnp.zeros((CMP_HIDDEN, LANES - HEAD_DIM), w2_k.dtype)], axis=1)
    return (jnp.concatenate([pos_k, pos_k], axis=1), jnp.concatenate([pos_v, pos_v], axis=1),
            block_diag2(w1_k).astype(BF16), block_diag2(w1_v).astype(BF16),
            block_diag2(pad_k).astype(BF16),
            block_diag2(w2_v).astype(BF16))


def _layer(xf, tabs, B, S, g_pre, w_in, cmp_pos_k, cmp_w1_k, cmp_w2_k, cmp_pos_v, cmp_w1_v, cmp_w2_v,
           gla_w_a, gla_b_a, gla_g_norm, w_up_nsa, w_up_gla, w_out, g_post):
    cosf, sinf, noh, ovt = tabs
    d = xf.shape[1]
    p = _proj_call(xf, g_pre.reshape(1, d), _pad_proj_weight(w_in), cosf, sinf, noh, S, tm=ROW_BLOCK)

    cmp_w = _compress_weights(cmp_pos_k, cmp_w1_k, cmp_w2_k, cmp_pos_v, cmp_w1_v, cmp_w2_v)
    kc, vct = _cmp_call(p["kcr"], p["vcr"], cmp_w, B, S)
    o_nsa = _nsa_call(p, kc, vct, ovt, B, S, qt=TILE)

    wa = jnp.zeros((LANES, gla_w_a.shape[1]), F32).at[:GLA_RANK].set(gla_w_a)
    o_gla = _gla_call(p, wa, gla_b_a.reshape(1, -1), gla_g_norm.reshape(1, GLA_DV), B, S, tc=ROW_BLOCK)

    return _out_call(o_nsa, p["nzs"], o_gla, p["mgs"], xf, w_up_nsa.astype(BF16), w_up_gla.astype(BF16),
                     w_out.astype(BF16), g_post.reshape(1, d), tm=ROW_BLOCK)


def kernel(x, g_pre, w_in, cmp_pos_k, cmp_w1_k, cmp_w2_k, cmp_pos_v, cmp_w1_v, cmp_w2_v, gla_w_a, gla_b_a,
           gla_g_norm, w_up_nsa, w_up_gla, w_out, g_post):
    B, S, d = x.shape
    assert d == D_MODEL and S % 2048 == 0 and S // SLC_BLOCK <= LANES - HEAD_DIM
    tabs = _position_tables(S)
    xf = x.reshape(B * S, d)
    for l in range(g_pre.shape[0]):
        xf = _layer(xf, tabs, B, S, g_pre[l], w_in[l], cmp_pos_k[l], cmp_w1_k[l], cmp_w2_k[l],
                    cmp_pos_v[l], cmp_w1_v[l], cmp_w2_v[l], gla_w_a[l], gla_b_a[l], gla_g_norm[l],
                    w_up_nsa[l], w_up_gla[l], w_out[l], g_post[l])
    return xf.reshape(B, S, d)
```

```python
import functools

import numpy as np
import jax
import jax.numpy as jnp
from jax import lax
from jax.experimental import pallas as pl
from jax.experimental.pallas import tpu as pltpu

F32 = jnp.float32
BF16 = jnp.bfloat16

D_MODEL = 1024
NSA_HEADS = 8
NSA_GROUPS = 2
HPG = NSA_HEADS // NSA_GROUPS
HEAD_DIM = 64
CMP_BLOCK = 32
CMP_STRIDE = 16
CMP_HIDDEN = 128
SLC_BLOCK = 64
SLC_TOPK = 16
WINDOW = 512
GLA_HEADS = 4
GLA_DK = 64
GLA_DV = 128
GLA_RANK = 16
GLA_TAU = 16.0
GLA_CHUNK = 64
ROPE_THETA = 10000.0
NORM_EPS = 1e-6
QK_SCALE = HEAD_DIM ** -0.5
Q_PRESCALE = QK_SCALE * 1.4426950408889634
GLA_SCALE = GLA_DK ** -0.5

LANES = 128
MXU_COLS = 256
BF16_SUBLANES = 16
TILE = 256
ROW_BLOCK = 2 * TILE
GATE_ROWS = 32
V_ROWS = HEAD_DIM + BF16_SUBLANES
MASK_BIG = 1e30
VMEM_LIMIT = 56 * 1024 * 1024

NT_DIMS = (((1,), (1,)), ((), ()))
TN_DIMS = (((0,), (0,)), ((), ()))

_SPLITS = (("q", 512), ("kcvc", 256), ("ks", 128), ("vs", 128), ("kw", 128), ("vw", 128),
           ("ng", 24), ("nz", 512), ("gq", 256), ("gk", 256), ("gv", 512), ("ga", 16), ("gr", 512),
           ("mg", 2048))


_MERGE_SEGS = ("nz", "mg")


def _proj_layout():
    src, pofs, msrc, o_src, o_dst = [], {}, {}, 0, 0
    for name, width in _SPLITS:
        if name in _MERGE_SEGS:
            msrc[name] = (o_src, width)
        else:
            padded = -(-width // LANES) * LANES
            src.append((o_src, width, padded - width))
            pofs[name] = (o_dst, padded)
            o_dst += padded
        o_src += width
    return src, pofs, o_dst, msrc


_PSRC, _POFS, NP_COLS, _MSRC = _proj_layout()


def _pad_proj_weight(w):
    parts = []
    for start, width, pad in _PSRC:
        parts.append(w[:, start:start + width])
        if pad:
            parts.append(jnp.zeros((w.shape[0], pad), w.dtype))
    return jnp.concatenate(parts, axis=1).astype(BF16)


def _merge_gate_weight(w):
    return jnp.concatenate([w[:, s:s + n] for s, n in (_MSRC[name] for name in _MERGE_SEGS)], axis=1).astype(BF16)


def _sigmoid(v):
    return 1.0 / (1.0 + jnp.exp(-v))


def _proj_kernel(x_ref, g_ref, w_ref, cos_ref, sin_ref, noh_ref,
                 qn_ref, qr_ref, kcr_ref, vcr_ref, ksa_ref, kwr_ref, vst_ref, vwt_ref, gate_ref,
                 gq_ref, gk_ref, gv_ref, ga_ref, gr_ref):
    for blk in range(x_ref.shape[0] // TILE):
        _proj_block(blk, slice(blk * TILE, (blk + 1) * TILE), x_ref, g_ref, w_ref, cos_ref, sin_ref, noh_ref,
                    qn_ref, qr_ref, kcr_ref, vcr_ref, ksa_ref, kwr_ref, vst_ref, vwt_ref, gate_ref,
                    gq_ref, gk_ref, gv_ref, ga_ref, gr_ref)


def _proj_block(blk, rs, x_ref, g_ref, w_ref, cos_ref, sin_ref, noh_ref,
                qn_ref, qr_ref, kcr_ref, vcr_ref, ksa_ref, kwr_ref, vst_ref, vwt_ref, gate_ref,
                gq_ref, gk_ref, gv_ref, ga_ref, gr_ref):
    x = x_ref[rs, :]
    y = x * lax.rsqrt(jnp.mean(x * x, axis=-1, keepdims=True) + NORM_EPS)
    h = (y * g_ref[...]).astype(BF16)
    cos = cos_ref[rs, :]
    sin = sin_ref[rs, :]
    tm = TILE
    lane = lax.broadcasted_iota(jnp.int32, (tm, LANES), 1)
    lo_half = (lane & (HEAD_DIM - 1)) < (HEAD_DIM // 2)
    first_head = lane < HEAD_DIM

    def rope(v):
        rot = jnp.where(lo_half, pltpu.roll(v, LANES - HEAD_DIM // 2, 1), pltpu.roll(v, HEAD_DIM // 2, 1))
        return v * cos + rot * sin

    windows = {}

    def chunks(name):
        start, width = _POFS[name]
        for c in range(width // LANES):
            col = start + c * LANES
            w = col // MXU_COLS
            if w not in windows:
                windows[w] = jnp.dot(h, w_ref[:, w * MXU_COLS:(w + 1) * MXU_COLS], preferred_element_type=F32)
            off = col % MXU_COLS
            yield c, windows[w][:, off:off + LANES]

    for pr, v in chunks("q"):
        n_t = (v * Q_PRESCALE).T.astype(BF16)
        r_t = (rope(v) * Q_PRESCALE).T.astype(BF16)
        for e in range(2):
            qn_ref[blk, 2 * pr + e] = n_t[e * HEAD_DIM:(e + 1) * HEAD_DIM]
            qr_ref[blk, 2 * pr + e] = r_t[e * HEAD_DIM:(e + 1) * HEAD_DIM]
    for (_, v), ref in zip(chunks("kcvc"), (kcr_ref, vcr_ref)):
        ref[rs, :] = v
    noh = noh_ref[rs, :]
    for name, ref, upper in (("ks", ksa_ref, noh), ("kw", kwr_ref, jnp.zeros_like(noh))):
        for _, v in chunks(name):
            r = rope(v)
            ref[rs, :LANES] = jnp.where(first_head, r, upper).astype(BF16)
            ref[rs, LANES:] = jnp.where(first_head, pltpu.roll(r, HEAD_DIM, 1), upper).astype(BF16)
    row16 = lax.broadcasted_iota(jnp.int32, (BF16_SUBLANES, tm), 0)
    ones_rows = jnp.where(row16 == 0, 1.0, 0.0)
    for name, ref in (("vs", vst_ref), ("vw", vwt_ref)):
        for _, v in chunks(name):
            v_t = v.T
            for g in range(NSA_GROUPS):
                ref[g, blk] = jnp.concatenate([v_t[g * HEAD_DIM:(g + 1) * HEAD_DIM], ones_rows], axis=0).astype(BF16)
    for _, v in chunks("ng"):
        gate_ref[blk] = _sigmoid(v).T[:GATE_ROWS]
    for name, ref in (("gq", gq_ref), ("gk", gk_ref), ("gv", gv_ref), ("gr", gr_ref)):
        for c, v in chunks(name):
            ref[rs, c * LANES:(c + 1) * LANES] = v.astype(BF16)
    for _, v in chunks("ga"):
        ga_ref[rs, :] = v


def _proj_call(xf, g_pre, w_p, cosf, sinf, noh, S, tm):
    n, d = xf.shape
    nt = n // tm
    spt = S // tm
    row = lambda width: pl.BlockSpec((tm, width), lambda i: (i, 0))
    pos = pl.BlockSpec((tm, LANES), lambda i: (i % spt, 0))
    nb_ = tm // TILE
    ntile = n // TILE
    qt_shape = ((ntile, NSA_HEADS, HEAD_DIM, TILE), (nb_, NSA_HEADS, HEAD_DIM, TILE), lambda i: (i, 0, 0, 0))
    vt_shape = ((NSA_GROUPS, ntile, V_ROWS, TILE), (NSA_GROUPS, nb_, V_ROWS, TILE), lambda i: (0, i, 0, 0))
    gt_shape = ((ntile, GATE_ROWS, TILE), (nb_, GATE_ROWS, TILE), lambda i: (i, 0, 0))
    outs = [("qn", qt_shape, BF16), ("qr", qt_shape, BF16), ("kcr", 128, F32), ("vcr", 128, F32),
            ("ksa", 256, BF16),
            ("kwr", 256, BF16), ("vst", vt_shape, BF16), ("vwt", vt_shape, BF16), ("gate", gt_shape, F32),
            ("gq", 256, BF16), ("gk", 256, BF16), ("gv", 512, BF16), ("ga", 128, F32), ("gr", 512, BF16)]
    out_specs, out_shape = [], []
    for _, sh, dt in outs:
        if isinstance(sh, int):
            out_specs.append(row(sh))
            out_shape.append(jax.ShapeDtypeStruct((n, sh), dt))
        else:
            out_specs.append(pl.BlockSpec(sh[1], sh[2]))
            out_shape.append(jax.ShapeDtypeStruct(sh[0], dt))
    res = pl.pallas_call(
        _proj_kernel,
        grid=(nt,),
        in_specs=[row(d),
                  pl.BlockSpec((1, d), lambda i: (0, 0)),
                  pl.BlockSpec((d, NP_COLS), lambda i: (0, 0), pipeline_mode=pl.Buffered(1)),
                  pos, pos, pos],
        out_specs=out_specs,
        out_shape=out_shape,
        compiler_params=pltpu.CompilerParams(dimension_semantics=("parallel",),
                                             vmem_limit_bytes=VMEM_LIMIT),
        name="proj",
    )(xf, g_pre, w_p, cosf, sinf, noh)
    return {nm: r for (nm, _, _), r in zip(outs, res)}


def _cmp_kernel(xk_ref, xv_ref, pk_ref, pv_ref, w1k_ref, w1v_ref, w2k_ref, w2v_ref, kc_ref, vct_ref):
    nc = xk_ref.shape[0] // CMP_STRIDE

    def hidden(x_ref, p_ref, w1_ref):
        a = None
        b = None
        for l in range(CMP_STRIDE):
            x = x_ref[pl.ds(l, nc, stride=CMP_STRIDE), :]
            l2 = CMP_STRIDE + l
            da = jnp.dot((x + p_ref[l:l + 1, :]).astype(BF16), w1_ref[l], preferred_element_type=F32)
            db = jnp.dot((x + p_ref[l2:l2 + 1, :]).astype(BF16), w1_ref[l2], preferred_element_type=F32)
            a = da if a is None else a + da
            b = db if b is None else b + db
        hid = a + pltpu.roll(b, nc - 1, 0)
        return (hid * _sigmoid(hid)).astype(BF16)

    kc_ref[0] = jnp.dot(hidden(xk_ref, pk_ref, w1k_ref), w2k_ref[...], preferred_element_type=F32).astype(BF16)
    v_t = jnp.dot(hidden(xv_ref, pv_ref, w1v_ref), w2v_ref[...], preferred_element_type=F32).T
    for g in range(NSA_GROUPS):
        vct_ref[0, g] = v_t[g * HEAD_DIM:(g + 1) * HEAD_DIM].astype(BF16)


def _cmp_call(kcr, vcr, weights, B, S):
    nc = S // CMP_STRIDE
    full = lambda a: pl.BlockSpec(a.shape, lambda b: (0,) * a.ndim)
    kw = NSA_GROUPS * LANES
    return pl.pallas_call(
        _cmp_kernel,
        grid=(B,),
        in_specs=[pl.BlockSpec((S, LANES), lambda b: (b, 0)), pl.BlockSpec((S, LANES), lambda b: (b, 0))]
        + [full(w) for w in weights],
        out_specs=[pl.BlockSpec((1, nc, kw), lambda b: (b, 0, 0)),
                   pl.BlockSpec((1, NSA_GROUPS, HEAD_DIM, nc), lambda b: (b, 0, 0, 0))],
        out_shape=[jax.ShapeDtypeStruct((B, nc, kw), BF16),
                   jax.ShapeDtypeStruct((B, NSA_GROUPS, HEAD_DIM, nc), BF16)],
        compiler_params=pltpu.CompilerParams(dimension_semantics=("parallel",), vmem_limit_bytes=VMEM_LIMIT),
        name="compress",
    )(kcr, vcr, *weights)


def _nsa_kernel(qn_ref, qr_ref, gate_ref, kc_ref, vct_ref, ksa_ref, vst_ref, kw_ref, vwt_ref, ovt_ref, o_ref,
                qa_sc, acc_sc, m_sc, out_sc, s_sc, smax_sc, *, qt, nb, nc):
    i = pl.program_id(1)
    s0 = i * qt
    hd = HEAD_DIM
    heads = range(NSA_HEADS)
    grp_of = lambda h: h // HPG

    def gate_row(branch, h):
        return gate_ref[0, h * 3 + branch:h * 3 + branch + 1, :]

    def lanes_of(g):
        return slice(g * LANES, (g + 1) * LANES)

    r_i = lax.broadcasted_iota(jnp.int32, (qt, qt), 0)
    c_i = lax.broadcasted_iota(jnp.int32, (qt, qt), 1)

    def reset():
        acc_sc[...] = jnp.zeros(acc_sc.shape, F32)
        m_sc[...] = jnp.full(m_sc.shape, -jnp.inf, F32)

    def scores(k, q_of, h):
        s = jnp.dot(k, q_of(h), preferred_element_type=F32)
        return s, jnp.max(s, axis=0, keepdims=True)

    def fold(h, s_and_max, vt, mask, bias=None):
        s, s_max = s_and_max
        if mask is not None:
            s = jnp.where(mask, s, -MASK_BIG)
            s_max = jnp.max(s, axis=0, keepdims=True)
        if bias is not None:
            s_max = s_max + bias
        m_old = m_sc[h]
        m_new = jnp.maximum(m_old, s_max)
        alpha = jnp.exp2(m_old - m_new)
        pe = jnp.exp2(s - (m_new if bias is None else m_new - bias)).astype(BF16)
        acc_sc[h] = alpha * acc_sc[h] + jnp.dot(vt, pe, preferred_element_type=F32)
        m_sc[h] = m_new

    def tile(pending, vt, q_of, mask, k_next, bias=None):
        nxt = []
        for h in heads:
            if k_next is not None:
                nxt.append(scores(k_next[grp_of(h)], q_of, h))
            fold(h, pending[h], vt[grp_of(h)], mask, bias)
        return tuple(nxt)

    def first_scores(k, q_of):
        return tuple(scores(k[grp_of(h)], q_of, h) for h in heads)

    def finish(branch):
        for h in heads:
            scale = gate_row(branch, h) * (1.0 / acc_sc[h, hd:hd + 1, :])
            out_sc[h * hd:(h + 1) * hd, :] += acc_sc[h, :hd, :] * scale

    def rows(j):
        return pl.ds(pl.multiple_of(j * qt, qt), qt)

    for h in heads:
        qa_sc[h, :hd, :] = qr_ref[0, h]
    topk = min(SLC_TOPK, nb)
    quarters = 4
    assert nc % (quarters * BF16_SUBLANES) == 0 and nb % (quarters * 8) == 0

    def select_blocks(nc_e, nb_e):
        jj = lax.broadcasted_iota(jnp.int32, (nc_e, qt), 0)
        tt = s0 + lax.broadcasted_iota(jnp.int32, (nc_e, qt), 1)
        cmask = (CMP_STRIDE * jj + (CMP_BLOCK - 1)) <= tt
        kc = [kc_ref[0, :nc_e, g * LANES:g * LANES + hd] for g in range(NSA_GROUPS)]
        has_key = (s0 + lax.broadcasted_iota(jnp.int32, (1, qt), 1)) >= (CMP_BLOCK - 1)
        psum = [None] * NSA_GROUPS
        s_all = [jnp.dot(kc[grp_of(h)], qn_ref[0, h], preferred_element_type=F32) for h in heads]
        for h in heads:
            g = grp_of(h)
            s = jnp.where(cmask, s_all[h], -MASK_BIG)
            e = jnp.exp2(s - jnp.max(s, axis=0, keepdims=True))
            inv = jnp.where(has_key, 1.0 / jnp.sum(e, axis=0, keepdims=True), 0.0)
            p = e * inv
            o = jnp.dot(vct_ref[0, g, :, :nc_e], p.astype(BF16), preferred_element_type=F32)
            out_sc[h * hd:(h + 1) * hd, :] = gate_row(0, h) * o
            psum[g] = p if psum[g] is None else psum[g] + p

        n_idx = lax.broadcasted_iota(jnp.int32, (nb_e, qt), 0)
        n_f = n_idx.astype(F32)
        cur = (s0 + lax.broadcasted_iota(jnp.int32, (nb_e, qt), 1)) // SLC_BLOCK
        forced = (n_idx == 0) | (n_idx == cur) | (n_idx == cur - 1)
        valid = n_idx <= cur
        free = valid & jnp.logical_not(forced)
        ovt = ovt_ref[:nb_e, :nc_e]
        for g in range(NSA_GROUPS):
            if nb_e <= topk:
                sel = valid
            else:
                hi = psum[g].astype(BF16)
                r1 = psum[g] - hi.astype(F32)
                mid = r1.astype(BF16)
                lo = (r1 - mid.astype(F32)).astype(BF16)
                imp_t = (jnp.dot(ovt, hi, preferred_element_type=F32) + jnp.dot(ovt, mid, preferred_element_type=F32)
                         + jnp.dot(ovt, lo, preferred_element_type=F32))
                val = jnp.where(free, imp_t, -jnp.inf)
                picked = jnp.zeros((nb_e, qt), F32)
                for _ in range(max(topk - 3, 0)):
                    top = jnp.max(val, axis=0, keepdims=True)
                    first = jnp.min(jnp.where(val == top, n_f, float(nb_e)), axis=0, keepdims=True)
                    pick = n_f == first
                    picked = jnp.where(pick, 1.0, picked)
                    val = jnp.where(pick, -jnp.inf, val)
                sel = (forced | (picked > 0.0)) & valid
            notsel = jnp.where(sel, 0.0, 1.0).astype(BF16)
            if nb_e < LANES - hd:
                notsel = jnp.concatenate([notsel, jnp.zeros((LANES - hd - nb_e, qt), BF16)], axis=0)
            for h in range(g * HPG, (g + 1) * HPG):
                qa_sc[h, hd:, :] = notsel

    quarter = (i * quarters) // (nc * CMP_STRIDE // qt)
    for qq in range(quarters):
        pl.when(quarter == qq)(functools.partial(select_blocks, nc * (qq + 1) // quarters, nb * (qq + 1) // quarters))


    reset()
    qr_of = lambda h: qr_ref[0, h]
    mask_a = (r_i - c_i) > jnp.where(i >= 2, 0, qt)
    bias_b = jnp.where(i >= 1, 0.0, -MASK_BIG)
    ja = jnp.maximum(i - 2, 0)
    jb = jnp.maximum(i - 1, 0)
    k_win = lambda j: [kw_ref[rows(j), g * LANES:g * LANES + hd] for g in range(NSA_GROUPS)]
    v_win = lambda j: [vwt_ref[g, j] for g in range(NSA_GROUPS)]
    kb, ka = k_win(jb), k_win(ja)
    carry = first_scores(k_win(i), qr_of)
    carry = tile(carry, v_win(i), qr_of, r_i <= c_i, kb)
    carry = tile(carry, v_win(jb), qr_of, None, ka, bias=bias_b)
    tile(carry, v_win(ja), qr_of, mask_a, None)
    finish(2)

    reset()
    qa_of = lambda h: qa_sc[h]
    k_slc = lambda j: [ksa_ref[rows(j), lanes_of(g)] for g in range(NSA_GROUPS)]
    v_slc = lambda j: [vst_ref[g, j] for g in range(NSA_GROUPS)]

    def slc_scores(j, dst):
        ks = k_slc(j)
        for h in heads:
            s, s_max = scores(ks[grp_of(h)], qa_of, h)
            s_sc[dst, h] = s
            smax_sc[dst, h] = s_max

    def slc_tile(j, src, dst, mask=None):
        ks = k_slc(j + 1) if dst is not None else None
        vt = v_slc(j)
        for h in heads:
            if dst is not None:
                s, s_max = scores(ks[grp_of(h)], qa_of, h)
                s_sc[dst, h] = s
                smax_sc[dst, h] = s_max
            fold(h, (s_sc[src, h], smax_sc[src, h]), vt[grp_of(h)], mask)

    slc_scores(0, 0)
    n_pairs = i // 2

    def slc_path(first_slot, peel):
        a, b = first_slot, 1 - first_slot

        def pair(p, carry):
            j = 2 * p + peel
            slc_tile(j, a, b)
            slc_tile(j + 1, b, a)
            return carry

        lax.fori_loop(0, n_pairs, pair, 0)
        slc_tile(i, a, None, r_i <= c_i)

    @pl.when(i % 2 == 1)
    def _():
        slc_tile(0, 0, 1)
        slc_path(1, 1)

    @pl.when(i % 2 == 0)
    def _():
        slc_path(0, 0)

    finish(1)

    o_ref[...] = out_sc[...].T.astype(o_ref.dtype)


def _nsa_call(p, kc, vct, ovt, B, S, qt):
    assert WINDOW == 2 * qt
    nq = S // qt
    nb = S // SLC_BLOCK
    nc = S // CMP_STRIDE
    qspec = pl.BlockSpec((1, NSA_HEADS, HEAD_DIM, qt), lambda b, i: (b * nq + i, 0, 0, 0))
    kspec = pl.BlockSpec((S, NSA_GROUPS * LANES), lambda b, i: (b, 0))
    vspec = pl.BlockSpec((NSA_GROUPS, nq, V_ROWS, qt), lambda b, i: (0, b, 0, 0))
    kernel = functools.partial(_nsa_kernel, qt=qt, nb=nb, nc=nc)
    return pl.pallas_call(
        kernel,
        grid=(B, nq),
        in_specs=[qspec, qspec,
                  pl.BlockSpec((1, GATE_ROWS, qt), lambda b, i: (b * nq + i, 0, 0)),
                  pl.BlockSpec((1, nc, NSA_GROUPS * LANES), lambda b, i: (b, 0, 0)),
                  pl.BlockSpec((1, NSA_GROUPS, HEAD_DIM, nc), lambda b, i: (b, 0, 0, 0)),
                  kspec, vspec, kspec, vspec,
                  pl.BlockSpec(ovt.shape, lambda b, i: (0, 0))],
        out_specs=pl.BlockSpec((qt, NSA_HEADS * HEAD_DIM), lambda b, i: (b * nq + i, 0)),
        out_shape=jax.ShapeDtypeStruct((B * S, NSA_HEADS * HEAD_DIM), BF16),
        scratch_shapes=[pltpu.VMEM((NSA_HEADS, LANES, qt), BF16),
                        pltpu.VMEM((NSA_HEADS, V_ROWS, qt), F32),
                        pltpu.VMEM((NSA_HEADS, 1, qt), F32),
                        pltpu.VMEM((NSA_HEADS * HEAD_DIM, qt), F32),
                        pltpu.VMEM((2, NSA_HEADS, qt, qt), F32),
                        pltpu.VMEM((2, NSA_HEADS, 1, qt), F32)],
        compiler_params=pltpu.CompilerParams(dimension_semantics=("parallel", "arbitrary"),
                                             vmem_limit_bytes=VMEM_LIMIT),
        name="nsa",
    )(p["qn"], p["qr"], p["gate"], kc, vct, p["ksa"], p["vst"], p["kwr"], p["vwt"], ovt)


def _gla_kernel(gq_ref, gk_ref, gv_ref, ga_ref, gr_ref, wa_ref, ba_ref, gn_ref, o_ref, st_ref, *, tc):
    @pl.when(pl.program_id(1) == 0)
    def _():
        st_ref[...] = jnp.zeros(st_ref.shape, F32)

    a = jnp.dot(ga_ref[...], wa_ref[...], preferred_element_type=F32,
                precision=lax.Precision.HIGHEST) + ba_ref[...]
    log_a = (jnp.minimum(a, 0.0) - jnp.log1p(jnp.exp(-jnp.abs(a)))) * (1.0 / GLA_TAU)
    c_sz = GLA_CHUNK
    n_chunk = tc // c_sz
    chunks = range(n_chunk)
    heads = range(GLA_HEADS)
    r_t = lax.broadcasted_iota(jnp.int32, (tc, tc), 0)
    c_t = lax.broadcasted_iota(jnp.int32, (tc, tc), 1)
    tril = jnp.where((r_t >= c_t) & ((r_t // c_sz) == (c_t // c_sz)), 1.0, 0.0).astype(BF16)
    hi = log_a.astype(BF16)
    r1 = log_a - hi.astype(F32)
    mid = r1.astype(BF16)
    lo = (r1 - mid.astype(F32)).astype(BF16)
    bcum = (jnp.dot(tril, hi, preferred_element_type=F32) + jnp.dot(tril, mid, preferred_element_type=F32)
            + jnp.dot(tril, lo, preferred_element_type=F32))
    gk = gk_ref[...].astype(F32)
    qg = (gq_ref[...].astype(F32) * GLA_SCALE) * jnp.exp(bcum)
    kg = (gk * jnp.exp(-bcum)).astype(BF16)
    lane = lax.broadcasted_iota(jnp.int32, (1, LANES), 1)
    causal = (lax.broadcasted_iota(jnp.int32, (c_sz, c_sz), 0)
              >= lax.broadcasted_iota(jnp.int32, (c_sz, c_sz), 1))
    rows = lambda c: slice(c * c_sz, (c + 1) * c_sz)
    klanes = lambda h: slice((h // 2) * LANES, (h // 2 + 1) * LANES)
    vlanes = lambda h: slice(h * GLA_DV, (h + 1) * GLA_DV)
    kd, dec = [], []
    for c in chunks:
        bl = bcum[(c + 1) * c_sz - 1:(c + 1) * c_sz]
        kd.append((gk[rows(c)] * jnp.exp(bl - bcum[rows(c)])).astype(BF16))
        dec.append(jnp.exp(bl))
    qg_h, att, upd = {}, {}, {}
    for c in chunks:
        for h in heads:
            own = (lane // GLA_DK) == (h % 2)
            qg_h[c, h] = jnp.where(own, qg[rows(c), klanes(h)], 0.0).astype(BF16)
            att[c, h] = lax.dot_general(qg_h[c, h], kg[rows(c), klanes(h)], NT_DIMS, preferred_element_type=F32)
    for c in chunks:
        for h in heads:
            upd[c, h] = lax.dot_general(gv_ref[rows(c), vlanes(h)], kd[c][:, klanes(h)], TN_DIMS,
                                        preferred_element_type=F32)
    st_before = {}
    for h in heads:
        st = st_ref[h]
        for c in chunks:
            st_before[c, h] = st.astype(BF16)
            st = st * dec[c][:, klanes(h)] + upd[c, h]
        st_ref[h] = st
    gn = gn_ref[...]
    for c in chunks:
        for h in heads:
            a_c = jnp.where(causal, att[c, h], 0.0).astype(BF16)
            o = (jnp.dot(a_c, gv_ref[rows(c), vlanes(h)], preferred_element_type=F32)
                 + lax.dot_general(qg_h[c, h], st_before[c, h], NT_DIMS, preferred_element_type=F32))
            y = o * lax.rsqrt(jnp.mean(o * o, axis=-1, keepdims=True) + NORM_EPS) * gn
            gr = gr_ref[rows(c), vlanes(h)].astype(F32)
            o_ref[rows(c), vlanes(h)] = (y * (gr * _sigmoid(gr))).astype(BF16)


def _gla_call(p, wa, ba, gn, B, S, tc):
    nt = S // tc
    row = lambda w: pl.BlockSpec((tc, w), lambda b, t: (b * nt + t, 0))
    full = lambda a: pl.BlockSpec(a.shape, lambda b, t: (0,) * a.ndim)
    kwidth = GLA_HEADS * GLA_DK
    vwidth = GLA_HEADS * GLA_DV
    return pl.pallas_call(
        functools.partial(_gla_kernel, tc=tc),
        grid=(B, nt),
        in_specs=[row(kwidth), row(kwidth), row(vwidth), row(LANES), row(vwidth), full(wa), full(ba), full(gn)],
        out_specs=row(vwidth),
        out_shape=jax.ShapeDtypeStruct((B * S, vwidth), BF16),
        scratch_shapes=[pltpu.VMEM((GLA_HEADS, GLA_DV, LANES), F32)],
        compiler_params=pltpu.CompilerParams(dimension_semantics=("parallel", "arbitrary"),
                                             vmem_limit_bytes=VMEM_LIMIT),
        name="gla",
    )(p["gq"], p["gk"], p["gv"], p["ga"], p["gr"], wa, ba, gn)


def _out_kernel(on_ref, og_ref, x_ref, gpre_ref, wz_ref, wn_ref, wg_ref, wo_ref, gp_ref, o_ref):
    d = x_ref.shape[1]
    nz_w = _MSRC["nz"][1]
    blocks = [slice(r0, r0 + TILE) for r0 in range(0, x_ref.shape[0], TILE)]
    ups = []
    for rs in blocks:
        x = x_ref[rs, :]
        h = ((x * lax.rsqrt(jnp.mean(x * x, axis=-1, keepdims=True) + NORM_EPS)) * gpre_ref[...]).astype(BF16)
        nz = jnp.dot(h, wz_ref[:, :nz_w], preferred_element_type=F32)
        mg = [jnp.dot(h, wz_ref[:, nz_w + c * d:nz_w + (c + 1) * d], preferred_element_type=F32) for c in range(2)]
        gated = on_ref[rs, :].astype(F32) * (nz * _sigmoid(nz))
        ups.append((jnp.dot(gated.astype(BF16), wn_ref[...], preferred_element_type=F32),
                    jnp.dot(og_ref[rs, :], wg_ref[...], preferred_element_type=F32), mg))
    for rs, (a, b, mg) in zip(blocks, ups):
        y = _sigmoid(mg[0]) * a + _sigmoid(mg[1]) * b
        out = jnp.dot(y.astype(BF16), wo_ref[...], preferred_element_type=F32)
        r = out * lax.rsqrt(jnp.mean(out * out, axis=-1, keepdims=True) + NORM_EPS)
        o_ref[rs, :] = x_ref[rs, :] + r * gp_ref[...]


def _out_call(o_nsa, o_gla, xf, g_pre, wz, wn, wg, wo, gp, tm):
    n, d = xf.shape
    row = lambda w: pl.BlockSpec((tm, w), lambda i: (i, 0))
    full = lambda a: pl.BlockSpec(a.shape, lambda i: (0,) * a.ndim)
    return pl.pallas_call(
        _out_kernel,
        grid=(n // tm,),
        in_specs=[row(o_nsa.shape[1]), row(o_gla.shape[1]), row(d),
                  full(g_pre), full(wz), full(wn), full(wg), full(wo), full(gp)],
        out_specs=row(d),
        out_shape=jax.ShapeDtypeStruct((n, d), F32),
        compiler_params=pltpu.CompilerParams(dimension_semantics=("parallel",), vmem_limit_bytes=VMEM_LIMIT),
        name="merge_out",
    )(o_nsa, o_gla, xf, g_pre, wz, wn, wg, wo, gp)


def _position_tables(S):
    pos = jnp.arange(S, dtype=F32)
    inv_freq = ROPE_THETA ** (-jnp.arange(0, HEAD_DIM, 2, dtype=F32) / HEAD_DIM)
    ang = pos[:, None] * inv_freq[None, :]
    cos, sin = jnp.cos(ang), jnp.sin(ang)
    cosf = jnp.concatenate([cos] * 4, axis=1)
    sinf = jnp.concatenate([-sin, sin, -sin, sin], axis=1)
    blk = np.arange(S) // SLC_BLOCK
    noh = np.zeros((S, LANES), np.float32)
    noh[np.arange(S), HEAD_DIM + blk] = -MASK_BIG
    nb, nc = S // SLC_BLOCK, S // CMP_STRIDE
    cs = CMP_STRIDE * np.arange(nc)
    bs = SLC_BLOCK * np.arange(nb)
    ov = (cs[None, :] < bs[:, None] + SLC_BLOCK) & (cs[None, :] + CMP_BLOCK > bs[:, None])
    ov[:, (S - CMP_BLOCK) // CMP_STRIDE + 1:] = False
    return cosf, sinf, jnp.asarray(noh), jnp.asarray(ov.astype(np.float32)).astype(BF16)


def _compress_weights(pos_k, w1_k, w2_k, pos_v, w1_v, w2_v):
    assert NSA_GROUPS == 2

    def block_diag2(w):
        z = jnp.zeros_like(w)
        return jnp.concatenate([jnp.concatenate([w, z], axis=-1), jnp.concatenate([z, w], axis=-1)], axis=-2)

    pad_k = jnp.concatenate([w2_k, jnp.zeros((CMP_HIDDEN, LANES - HEAD_DIM), w2_k.dtype)], axis=1)
    return (jnp.concatenate([pos_k, pos_k], axis=1), jnp.concatenate([pos_v, pos_v], axis=1),
            block_diag2(w1_k).astype(BF16), block_diag2(w1_v).astype(BF16),
            block_diag2(pad_k).astype(BF16),
            block_diag2(w2_v).astype(BF16))


def _layer(xf, tabs, B, S, g_pre, w_in, cmp_pos_k, cmp_w1_k, cmp_w2_k, cmp_pos_v, cmp_w1_v, cmp_w2_v,
           gla_w_a, gla_b_a, gla_g_norm, w_up_nsa, w_up_gla, w_out, g_post):
    cosf, sinf, noh, ovt = tabs
    d = xf.shape[1]
    p = _proj_call(xf, g_pre.reshape(1, d), _pad_proj_weight(w_in), cosf, sinf, noh, S, tm=ROW_BLOCK)

    cmp_w = _compress_weights(cmp_pos_k, cmp_w1_k, cmp_w2_k, cmp_pos_v, cmp_w1_v, cmp_w2_v)
    kc, vct = _cmp_call(p["kcr"], p["vcr"], cmp_w, B, S)
    o_nsa = _nsa_call(p, kc, vct, ovt, B, S, qt=TILE)

    wa = jnp.zeros((LANES, gla_w_a.shape[1]), F32).at[:GLA_RANK].set(gla_w_a)
    o_gla = _gla_call(p, wa, gla_b_a.reshape(1, -1), gla_g_norm.reshape(1, GLA_DV), B, S, tc=ROW_BLOCK)

    return _out_call(o_nsa, o_gla, xf, g_pre.reshape(1, d), _merge_gate_weight(w_in), w_up_nsa.astype(BF16),
                     w_up_gla.astype(BF16), w_out.astype(BF16), g_post.reshape(1, d), tm=ROW_BLOCK)


def kernel(x, g_pre, w_in, cmp_pos_k, cmp_w1_k, cmp_w2_k, cmp_pos_v, cmp_w1_v, cmp_w2_v, gla_w_a, gla_b_a,
           gla_g_norm, w_up_nsa, w_up_gla, w_out, g_post):
    B, S, d = x.shape
    assert d == D_MODEL and S % 2048 == 0 and S // SLC_BLOCK <= LANES - HEAD_DIM
    tabs = _position_tables(S)
    xf = x.reshape(B * S, d)
    for l in range(g_pre.shape[0]):
        xf = _layer(xf, tabs, B, S, g_pre[l], w_in[l], cmp_pos_k[l], cmp_w1_k[l], cmp_w2_k[l],
                    cmp_pos_v[l], cmp_w1_v[l], cmp_w2_v[l], gla_w_a[l], gla_b_a[l], gla_g_norm[l],
                    w_up_nsa[l], w_up_gla[l], w_out[l], g_post[l])
    return xf.reshape(B, S, d)
```

```python
import functools

import numpy as np
import jax
import jax.numpy as jnp
from jax import lax
from jax.experimental import pallas as pl
from jax.experimental.pallas import tpu as pltpu

F32 = jnp.float32
BF16 = jnp.bfloat16

D_MODEL = 1024
NSA_HEADS = 8
NSA_GROUPS = 2
HPG = NSA_HEADS // NSA_GROUPS
HEAD_DIM = 64
CMP_BLOCK = 32
CMP_STRIDE = 16
CMP_HIDDEN = 128
SLC_BLOCK = 64
SLC_TOPK = 16
WINDOW = 512
GLA_HEADS = 4
GLA_DK = 64
GLA_DV = 128
GLA_RANK = 16
GLA_TAU = 16.0
GLA_CHUNK = 64
ROPE_THETA = 10000.0
NORM_EPS = 1e-6
QK_SCALE = HEAD_DIM ** -0.5
Q_PRESCALE = QK_SCALE * 1.4426950408889634
GLA_SCALE = GLA_DK ** -0.5

LANES = 128
MXU_COLS = 256
BF16_SUBLANES = 16
TILE = 256
ROW_BLOCK = 2 * TILE
GATE_ROWS = 32
V_ROWS = HEAD_DIM + BF16_SUBLANES
MASK_BIG = 1e30
VMEM_LIMIT = 56 * 1024 * 1024

NT_DIMS = (((1,), (1,)), ((), ()))
TN_DIMS = (((0,), (0,)), ((), ()))

_SPLITS = (("q", 512), ("kcvc", 256), ("ks", 128), ("vs", 128), ("kw", 128), ("vw", 128),
           ("ng", 24), ("nz", 512), ("gq", 256), ("gk", 256), ("gv", 512), ("ga", 16), ("gr", 512),
           ("mg", 2048))


_MERGE_SEGS = ("nz", "mg")


def _proj_layout():
    src, pofs, msrc, o_src, o_dst = [], {}, {}, 0, 0
    for name, width in _SPLITS:
        if name in _MERGE_SEGS:
            msrc[name] = (o_src, width)
        else:
            padded = -(-width // LANES) * LANES
            src.append((o_src, width, padded - width))
            pofs[name] = (o_dst, padded)
            o_dst += padded
        o_src += width
    return src, pofs, o_dst, msrc


_PSRC, _POFS, NP_COLS, _MSRC = _proj_layout()


def _pad_proj_weight(w):
    parts = []
    for start, width, pad in _PSRC:
        parts.append(w[:, start:start + width])
        if pad:
            parts.append(jnp.zeros((w.shape[0], pad), w.dtype))
    return jnp.concatenate(parts, axis=1).astype(BF16)


def _merge_gate_weight(w):
    return jnp.concatenate([w[:, s:s + n] for s, n in (_MSRC[name] for name in _MERGE_SEGS)], axis=1).astype(BF16)


def _sigmoid(v):
    return 1.0 / (1.0 + jnp.exp(-v))


def _proj_kernel(x_ref, g_ref, w_ref, cos_ref, sin_ref, noh_ref,
                 qn_ref, qr_ref, kcr_ref, vcr_ref, ksa_ref, kwr_ref, vst_ref, vwt_ref, gate_ref,
                 gq_ref, gk_ref, gv_ref, ga_ref, gr_ref):
    for blk in range(x_ref.shape[0] // TILE):
        _proj_block(blk, slice(blk * TILE, (blk + 1) * TILE), x_ref, g_ref, w_ref, cos_ref, sin_ref, noh_ref,
                    qn_ref, qr_ref, kcr_ref, vcr_ref, ksa_ref, kwr_ref, vst_ref, vwt_ref, gate_ref,
                    gq_ref, gk_ref, gv_ref, ga_ref, gr_ref)


def _proj_block(blk, rs, x_ref, g_ref, w_ref, cos_ref, sin_ref, noh_ref,
                qn_ref, qr_ref, kcr_ref, vcr_ref, ksa_ref, kwr_ref, vst_ref, vwt_ref, gate_ref,
                gq_ref, gk_ref, gv_ref, ga_ref, gr_ref):
    x = x_ref[rs, :]
    y = x * lax.rsqrt(jnp.mean(x * x, axis=-1, keepdims=True) + NORM_EPS)
    h = (y * g_ref[...]).astype(BF16)
    cos = cos_ref[rs, :]
    sin = sin_ref[rs, :]
    tm = TILE
    lane = lax.broadcasted_iota(jnp.int32, (tm, LANES), 1)
    lo_half = (lane & (HEAD_DIM - 1)) < (HEAD_DIM // 2)
    first_head = lane < HEAD_DIM

    def rope(v):
        rot = jnp.where(lo_half, pltpu.roll(v, LANES - HEAD_DIM // 2, 1), pltpu.roll(v, HEAD_DIM // 2, 1))
        return v * cos + rot * sin

    windows = {}

    def chunks(name):
        start, width = _POFS[name]
        for c in range(width // LANES):
            col = start + c * LANES
            w = col // MXU_COLS
            if w not in windows:
                windows[w] = jnp.dot(h, w_ref[:, w * MXU_COLS:(w + 1) * MXU_COLS], preferred_element_type=F32)
            off = col % MXU_COLS
            yield c, windows[w][:, off:off + LANES]

    for pr, v in chunks("q"):
        n_t = (v * Q_PRESCALE).T.astype(BF16)
        r_t = (rope(v) * Q_PRESCALE).T.astype(BF16)
        for e in range(2):
            qn_ref[blk, 2 * pr + e] = n_t[e * HEAD_DIM:(e + 1) * HEAD_DIM]
            qr_ref[blk, 2 * pr + e] = r_t[e * HEAD_DIM:(e + 1) * HEAD_DIM]
    for (_, v), ref in zip(chunks("kcvc"), (kcr_ref, vcr_ref)):
        ref[rs, :] = v
    noh = noh_ref[rs, :]
    for name, ref, upper in (("ks", ksa_ref, noh), ("kw", kwr_ref, jnp.zeros_like(noh))):
        for _, v in chunks(name):
            r = rope(v)
            ref[rs, :LANES] = jnp.where(first_head, r, upper).astype(BF16)
            ref[rs, LANES:] = jnp.where(first_head, pltpu.roll(r, HEAD_DIM, 1), upper).astype(BF16)
    row16 = lax.broadcasted_iota(jnp.int32, (BF16_SUBLANES, tm), 0)
    ones_rows = jnp.where(row16 == 0, 1.0, 0.0)
    for name, ref in (("vs", vst_ref), ("vw", vwt_ref)):
        for _, v in chunks(name):
            v_t = v.T
            for g in range(NSA_GROUPS):
                ref[g, blk] = jnp.concatenate([v_t[g * HEAD_DIM:(g + 1) * HEAD_DIM], ones_rows], axis=0).astype(BF16)
    for _, v in chunks("ng"):
        gate_ref[blk] = _sigmoid(v).T[:GATE_ROWS]
    for name, ref in (("gq", gq_ref), ("gk", gk_ref), ("gv", gv_ref), ("gr", gr_ref)):
        for c, v in chunks(name):
            ref[rs, c * LANES:(c + 1) * LANES] = v.astype(BF16)
    for _, v in chunks("ga"):
        ga_ref[rs, :] = v


def _proj_call(xf, g_pre, w_p, cosf, sinf, noh, S, tm):
    n, d = xf.shape
    nt = n // tm
    spt = S // tm
    row = lambda width: pl.BlockSpec((tm, width), lambda i: (i, 0))
    pos = pl.BlockSpec((tm, LANES), lambda i: (i % spt, 0))
    nb_ = tm // TILE
    ntile = n // TILE
    qt_shape = ((ntile, NSA_HEADS, HEAD_DIM, TILE), (nb_, NSA_HEADS, HEAD_DIM, TILE), lambda i: (i, 0, 0, 0))
    vt_shape = ((NSA_GROUPS, ntile, V_ROWS, TILE), (NSA_GROUPS, nb_, V_ROWS, TILE), lambda i: (0, i, 0, 0))
    gt_shape = ((ntile, GATE_ROWS, TILE), (nb_, GATE_ROWS, TILE), lambda i: (i, 0, 0))
    outs = [("qn", qt_shape, BF16), ("qr", qt_shape, BF16), ("kcr", 128, F32), ("vcr", 128, F32),
            ("ksa", 256, BF16),
            ("kwr", 256, BF16), ("vst", vt_shape, BF16), ("vwt", vt_shape, BF16), ("gate", gt_shape, F32),
            ("gq", 256, BF16), ("gk", 256, BF16), ("gv", 512, BF16), ("ga", 128, F32), ("gr", 512, BF16)]
    out_specs, out_shape = [], []
    for _, sh, dt in outs:
        if isinstance(sh, int):
            out_specs.append(row(sh))
            out_shape.append(jax.ShapeDtypeStruct((n, sh), dt))
        else:
            out_specs.append(pl.BlockSpec(sh[1], sh[2]))
            out_shape.append(jax.ShapeDtypeStruct(sh[0], dt))
    res = pl.pallas_call(
        _proj_kernel,
        grid=(nt,),
        in_specs=[row(d),
                  pl.BlockSpec((1, d), lambda i: (0, 0)),
                  pl.BlockSpec((d, NP_COLS), lambda i: (0, 0), pipeline_mode=pl.Buffered(1)),
                  pos, pos, pos],
        out_specs=out_specs,
        out_shape=out_shape,
        compiler_params=pltpu.CompilerParams(dimension_semantics=("parallel",),
                                             vmem_limit_bytes=VMEM_LIMIT),
        name="proj",
    )(xf, g_pre, w_p, cosf, sinf, noh)
    return {nm: r for (nm, _, _), r in zip(outs, res)}


def _cmp_kernel(xk_ref, xv_ref, pk_ref, pv_ref, w1k_ref, w1v_ref, w2k_ref, w2v_ref, kc_ref, vct_ref):
    nc = xk_ref.shape[0] // CMP_STRIDE

    def hidden(x_ref, p_ref, w1_ref):
        a = None
        b = None
        for l in range(CMP_STRIDE):
            x = x_ref[pl.ds(l, nc, stride=CMP_STRIDE), :]
            l2 = CMP_STRIDE + l
            da = jnp.dot((x + p_ref[l:l + 1, :]).astype(BF16), w1_ref[l], preferred_element_type=F32)
            db = jnp.dot((x + p_ref[l2:l2 + 1, :]).astype(BF16), w1_ref[l2], preferred_element_type=F32)
            a = da if a is None else a + da
            b = db if b is None else b + db
        hid = a + pltpu.roll(b, nc - 1, 0)
        return (hid * _sigmoid(hid)).astype(BF16)

    kc_ref[0] = jnp.dot(hidden(xk_ref, pk_ref, w1k_ref), w2k_ref[...], preferred_element_type=F32).astype(BF16)
    v_t = jnp.dot(hidden(xv_ref, pv_ref, w1v_ref), w2v_ref[...], preferred_element_type=F32).T
    for g in range(NSA_GROUPS):
        vct_ref[0, g] = v_t[g * HEAD_DIM:(g + 1) * HEAD_DIM].astype(BF16)


def _cmp_call(kcr, vcr, weights, B, S):
    nc = S // CMP_STRIDE
    full = lambda a: pl.BlockSpec(a.shape, lambda b: (0,) * a.ndim)
    kw = NSA_GROUPS * LANES
    return pl.pallas_call(
        _cmp_kernel,
        grid=(B,),
        in_specs=[pl.BlockSpec((S, LANES), lambda b: (b, 0)), pl.BlockSpec((S, LANES), lambda b: (b, 0))]
        + [full(w) for w in weights],
        out_specs=[pl.BlockSpec((1, nc, kw), lambda b: (b, 0, 0)),
                   pl.BlockSpec((1, NSA_GROUPS, HEAD_DIM, nc), lambda b: (b, 0, 0, 0))],
        out_shape=[jax.ShapeDtypeStruct((B, nc, kw), BF16),
                   jax.ShapeDtypeStruct((B, NSA_GROUPS, HEAD_DIM, nc), BF16)],
        compiler_params=pltpu.CompilerParams(dimension_semantics=("parallel",), vmem_limit_bytes=VMEM_LIMIT),
        name="compress",
    )(kcr, vcr, *weights)


def _nsa_kernel(qn_ref, qr_ref, gate_ref, kc_ref, vct_ref, ksa_ref, vst_ref, kw_ref, vwt_ref, ovt_ref, o_ref,
                qa_sc, acc_sc, m_sc, out_sc, s_sc, smax_sc, *, qt, nb, nc):
    i = pl.program_id(1)
    s0 = i * qt
    hd = HEAD_DIM
    heads = range(NSA_HEADS)
    grp_of = lambda h: h // HPG

    def gate_row(branch, h):
        return gate_ref[0, h * 3 + branch:h * 3 + branch + 1, :]

    def lanes_of(g):
        return slice(g * LANES, (g + 1) * LANES)

    r_i = lax.broadcasted_iota(jnp.int32, (qt, qt), 0)
    c_i = lax.broadcasted_iota(jnp.int32, (qt, qt), 1)

    def reset():
        acc_sc[...] = jnp.zeros(acc_sc.shape, F32)
        m_sc[...] = jnp.full(m_sc.shape, -jnp.inf, F32)

    def scores(k, q_of, h):
        s = jnp.dot(k, q_of(h), preferred_element_type=F32)
        return s, jnp.max(s, axis=0, keepdims=True)

    def fold(h, s_and_max, vt, mask, bias=None):
        s, s_max = s_and_max
        if mask is not None:
            s = jnp.where(mask, s, -MASK_BIG)
            s_max = jnp.max(s, axis=0, keepdims=True)
        if bias is not None:
            s_max = s_max + bias
        m_old = m_sc[h]
        m_new = jnp.maximum(m_old, s_max)
        alpha = jnp.exp2(m_old - m_new)
        pe = jnp.exp2(s - (m_new if bias is None else m_new - bias)).astype(BF16)
        acc_sc[h] = alpha * acc_sc[h] + jnp.dot(vt, pe, preferred_element_type=F32)
        m_sc[h] = m_new

    def tile(pending, vt, q_of, mask, k_next, bias=None):
        nxt = []
        for h in heads:
            if k_next is not None:
                nxt.append(scores(k_next[grp_of(h)], q_of, h))
            fold(h, pending[h], vt[grp_of(h)], mask, bias)
        return tuple(nxt)

    def first_scores(k, q_of):
        return tuple(scores(k[grp_of(h)], q_of, h) for h in heads)

    def finish(branch):
        for h in heads:
            scale = gate_row(branch, h) * (1.0 / acc_sc[h, hd:hd + 1, :])
            out_sc[h * hd:(h + 1) * hd, :] += acc_sc[h, :hd, :] * scale

    def rows(j):
        return pl.ds(pl.multiple_of(j * qt, qt), qt)

    for h in heads:
        qa_sc[h, :hd, :] = qr_ref[0, h]
    topk = min(SLC_TOPK, nb)
    quarters = 4
    assert nc % (quarters * BF16_SUBLANES) == 0 and nb % (quarters * 8) == 0

    def select_blocks(nc_e, nb_e):
        jj = lax.broadcasted_iota(jnp.int32, (nc_e, qt), 0)
        tt = s0 + lax.broadcasted_iota(jnp.int32, (nc_e, qt), 1)
        cmask = (CMP_STRIDE * jj + (CMP_BLOCK - 1)) <= tt
        kc = [kc_ref[0, :nc_e, g * LANES:g * LANES + hd] for g in range(NSA_GROUPS)]
        has_key = (s0 + lax.broadcasted_iota(jnp.int32, (1, qt), 1)) >= (CMP_BLOCK - 1)
        psum = [None] * NSA_GROUPS
        s_all = [jnp.dot(kc[grp_of(h)], qn_ref[0, h], preferred_element_type=F32) for h in heads]
        for h in heads:
            g = grp_of(h)
            s = jnp.where(cmask, s_all[h], -MASK_BIG)
            e = jnp.exp2(s - jnp.max(s, axis=0, keepdims=True))
            inv = jnp.where(has_key, 1.0 / jnp.sum(e, axis=0, keepdims=True), 0.0)
            p = e * inv
            o = jnp.dot(vct_ref[0, g, :, :nc_e], p.astype(BF16), preferred_element_type=F32)
            out_sc[h * hd:(h + 1) * hd, :] = gate_row(0, h) * o
            psum[g] = p if psum[g] is None else psum[g] + p

        n_idx = lax.broadcasted_iota(jnp.int32, (nb_e, qt), 0)
        n_f = n_idx.astype(F32)
        cur = (s0 + lax.broadcasted_iota(jnp.int32, (nb_e, qt), 1)) // SLC_BLOCK
        forced = (n_idx == 0) | (n_idx == cur) | (n_idx == cur - 1)
        valid = n_idx <= cur
        free = valid & jnp.logical_not(forced)
        ovt = ovt_ref[:nb_e, :nc_e]
        for g in range(NSA_GROUPS):
            if nb_e <= topk:
                sel = valid
            else:
                hi = psum[g].astype(BF16)
                r1 = psum[g] - hi.astype(F32)
                mid = r1.astype(BF16)
                lo = (r1 - mid.astype(F32)).astype(BF16)
                imp_t = (jnp.dot(ovt, hi, preferred_element_type=F32) + jnp.dot(ovt, mid, preferred_element_type=F32)
                         + jnp.dot(ovt, lo, preferred_element_type=F32))
                val = jnp.where(free, imp_t, -jnp.inf)
                picked = jnp.zeros((nb_e, qt), F32)
                for _ in range(max(topk - 3, 0)):
                    top = jnp.max(val, axis=0, keepdims=True)
                    first = jnp.min(jnp.where(val == top, n_f, float(nb_e)), axis=0, keepdims=True)
                    pick = n_f == first
                    picked = jnp.where(pick, 1.0, picked)
                    val = jnp.where(pick, -jnp.inf, val)
                sel = (forced | (picked > 0.0)) & valid
            notsel = jnp.where(sel, 0.0, 1.0).astype(BF16)
            if nb_e < LANES - hd:
                notsel = jnp.concatenate([notsel, jnp.zeros((LANES - hd - nb_e, qt), BF16)], axis=0)
            for h in range(g * HPG, (g + 1) * HPG):
                qa_sc[h, hd:, :] = notsel

    quarter = (i * quarters) // (nc * CMP_STRIDE // qt)
    for qq in range(quarters):
        pl.when(quarter == qq)(functools.partial(select_blocks, nc * (qq + 1) // quarters, nb * (qq + 1) // quarters))


    reset()
    qr_of = lambda h: qr_ref[0, h]
    mask_a = (r_i - c_i) > jnp.where(i >= 2, 0, qt)
    bias_b = jnp.where(i >= 1, 0.0, -MASK_BIG)
    ja = jnp.maximum(i - 2, 0)
    jb = jnp.maximum(i - 1, 0)
    k_win = lambda j: [kw_ref[rows(j), g * LANES:g * LANES + hd] for g in range(NSA_GROUPS)]
    v_win = lambda j: [vwt_ref[g, j] for g in range(NSA_GROUPS)]
    kb, ka = k_win(jb), k_win(ja)
    carry = first_scores(k_win(i), qr_of)
    carry = tile(carry, v_win(i), qr_of, r_i <= c_i, kb)
    carry = tile(carry, v_win(jb), qr_of, None, ka, bias=bias_b)
    tile(carry, v_win(ja), qr_of, mask_a, None)
    finish(2)

    reset()
    qa_of = lambda h: qa_sc[h]
    k_slc = lambda j: [ksa_ref[rows(j), lanes_of(g)] for g in range(NSA_GROUPS)]
    v_slc = lambda j: [vst_ref[g, j] for g in range(NSA_GROUPS)]

    def slc_scores(j, dst):
        ks = k_slc(j)
        for h in heads:
            s, s_max = scores(ks[grp_of(h)], qa_of, h)
            s_sc[dst, h] = s
            smax_sc[dst, h] = s_max

    def slc_tile(j, src, dst, mask=None):
        ks = k_slc(j + 1) if dst is not None else None
        vt = v_slc(j)
        for h in heads:
            if dst is not None:
                s, s_max = scores(ks[grp_of(h)], qa_of, h)
                s_sc[dst, h] = s
                smax_sc[dst, h] = s_max
            fold(h, (s_sc[src, h], smax_sc[src, h]), vt[grp_of(h)], mask)

    slc_scores(0, 0)
    n_pairs = i // 2

    def slc_path(first_slot, peel):
        a, b = first_slot, 1 - first_slot

        def pair(p, carry):
            j = 2 * p + peel
            slc_tile(j, a, b)
            slc_tile(j + 1, b, a)
            return carry

        lax.fori_loop(0, n_pairs, pair, 0)
        slc_tile(i, a, None, r_i <= c_i)

    @pl.when(i % 2 == 1)
    def _():
        slc_tile(0, 0, 1)
        slc_path(1, 1)

    @pl.when(i % 2 == 0)
    def _():
        slc_path(0, 0)

    finish(1)

    o_ref[...] = out_sc[...].T.astype(o_ref.dtype)


def _nsa_call(p, kc, vct, ovt, B, S, qt):
    assert WINDOW == 2 * qt
    nq = S // qt
    nb = S // SLC_BLOCK
    nc = S // CMP_STRIDE
    qspec = pl.BlockSpec((1, NSA_HEADS, HEAD_DIM, qt), lambda b, i: (b * nq + i, 0, 0, 0))
    kspec = pl.BlockSpec((S, NSA_GROUPS * LANES), lambda b, i: (b, 0))
    vspec = pl.BlockSpec((NSA_GROUPS, nq, V_ROWS, qt), lambda b, i: (0, b, 0, 0))
    kernel = functools.partial(_nsa_kernel, qt=qt, nb=nb, nc=nc)
    return pl.pallas_call(
        kernel,
        grid=(B, nq),
        in_specs=[qspec, qspec,
                  pl.BlockSpec((1, GATE_ROWS, qt), lambda b, i: (b * nq + i, 0, 0)),
                  pl.BlockSpec((1, nc, NSA_GROUPS * LANES), lambda b, i: (b, 0, 0)),
                  pl.BlockSpec((1, NSA_GROUPS, HEAD_DIM, nc), lambda b, i: (b, 0, 0, 0)),
                  kspec, vspec, kspec, vspec,
                  pl.BlockSpec(ovt.shape, lambda b, i: (0, 0))],
        out_specs=pl.BlockSpec((qt, NSA_HEADS * HEAD_DIM), lambda b, i: (b * nq + i, 0)),
        out_shape=jax.ShapeDtypeStruct((B * S, NSA_HEADS * HEAD_DIM), BF16),
        scratch_shapes=[pltpu.VMEM((NSA_HEADS, LANES, qt), BF16),
                        pltpu.VMEM((NSA_HEADS, V_ROWS, qt), F32),
                        pltpu.VMEM((NSA_HEADS, 1, qt), F32),
                        pltpu.VMEM((NSA_HEADS * HEAD_DIM, qt), F32),
                        pltpu.VMEM((2, NSA_HEADS, qt, qt), F32),
                        pltpu.VMEM((2, NSA_HEADS, 1, qt), F32)],
        compiler_params=pltpu.CompilerParams(dimension_semantics=("parallel", "arbitrary"),
                                             vmem_limit_bytes=VMEM_LIMIT),
        name="nsa",
    )(p["qn"], p["qr"], p["gate"], kc, vct, p["ksa"], p["vst"], p["kwr"], p["vwt"], ovt)


def _gla_kernel(gq_ref, gk_ref, gv_ref, ga_ref, gr_ref, wa_ref, ba_ref, gn_ref, o_ref, st_ref, *, tc):
    @pl.when(pl.program_id(1) == 0)
    def _():
        st_ref[...] = jnp.zeros(st_ref.shape, F32)

    def split2(v):
        hi = v.astype(BF16)
        return hi, (v - hi.astype(F32)).astype(BF16)

    g_hi, g_lo = split2(ga_ref[...])
    w_hi, w_lo = split2(wa_ref[...])
    a = (jnp.dot(g_hi, w_hi, preferred_element_type=F32) + jnp.dot(g_hi, w_lo, preferred_element_type=F32)
         + jnp.dot(g_lo, w_hi, preferred_element_type=F32)) + ba_ref[...]
    log_a = (jnp.minimum(a, 0.0) - jnp.log1p(jnp.exp(-jnp.abs(a)))) * (1.0 / GLA_TAU)
    c_sz = GLA_CHUNK
    n_chunk = tc // c_sz
    chunks = range(n_chunk)
    heads = range(GLA_HEADS)
    r_t = lax.broadcasted_iota(jnp.int32, (MXU_COLS, MXU_COLS), 0)
    c_t = lax.broadcasted_iota(jnp.int32, (MXU_COLS, MXU_COLS), 1)
    tril = jnp.where((r_t >= c_t) & ((r_t // c_sz) == (c_t // c_sz)), 1.0, 0.0).astype(BF16)
    hi = log_a.astype(BF16)
    r1 = log_a - hi.astype(F32)
    mid = r1.astype(BF16)
    lo = (r1 - mid.astype(F32)).astype(BF16)
    bcum = jnp.concatenate(
        [jnp.dot(tril, hi[r0:r0 + MXU_COLS], preferred_element_type=F32)
         + jnp.dot(tril, mid[r0:r0 + MXU_COLS], preferred_element_type=F32)
         + jnp.dot(tril, lo[r0:r0 + MXU_COLS], preferred_element_type=F32)
         for r0 in range(0, tc, MXU_COLS)], axis=0)
    gk = gk_ref[...].astype(F32)
    qg = (gq_ref[...].astype(F32) * GLA_SCALE) * jnp.exp(bcum)
    kg = (gk * jnp.exp(-bcum)).astype(BF16)
    lane = lax.broadcasted_iota(jnp.int32, (1, LANES), 1)
    causal = (lax.broadcasted_iota(jnp.int32, (c_sz, c_sz), 0)
              >= lax.broadcasted_iota(jnp.int32, (c_sz, c_sz), 1))
    rows = lambda c: slice(c * c_sz, (c + 1) * c_sz)
    klanes = lambda h: slice((h // 2) * LANES, (h // 2 + 1) * LANES)
    vlanes = lambda h: slice(h * GLA_DV, (h + 1) * GLA_DV)
    kd, dec = [], []
    for c in chunks:
        bl = bcum[(c + 1) * c_sz - 1:(c + 1) * c_sz]
        kd.append((gk[rows(c)] * jnp.exp(bl - bcum[rows(c)])).astype(BF16))
        dec.append(jnp.exp(bl))
    qg_h, att, upd = {}, {}, {}
    pair = 2
    for c in chunks:
        for h0 in range(0, GLA_HEADS, pair):
            for h in range(h0, h0 + pair):
                own = (lane // GLA_DK) == (h % pair)
                qg_h[c, h] = jnp.where(own, qg[rows(c), klanes(h)], 0.0).astype(BF16)
            both = lax.dot_general(jnp.concatenate([qg_h[c, h0 + e] for e in range(pair)], axis=0),
                                   kg[rows(c), klanes(h0)], NT_DIMS, preferred_element_type=F32)
            for e in range(pair):
                att[c, h0 + e] = both[e * c_sz:(e + 1) * c_sz]
    for c in chunks:
        for h0 in range(0, GLA_HEADS, pair):
            both = lax.dot_general(gv_ref[rows(c), h0 * GLA_DV:(h0 + pair) * GLA_DV], kd[c][:, klanes(h0)], TN_DIMS,
                                   preferred_element_type=F32)
            for e in range(pair):
                upd[c, h0 + e] = both[e * GLA_DV:(e + 1) * GLA_DV]
    st_before = {}
    for h in heads:
        st = st_ref[h]
        for c in chunks:
            st_before[c, h] = st.astype(BF16)
            st = st * dec[c][:, klanes(h)] + upd[c, h]
        st_ref[h] = st
    gn = gn_ref[...]
    for c in chunks:
        for h in heads:
            a_c = jnp.where(causal, att[c, h], 0.0).astype(BF16)
            o = (jnp.dot(a_c, gv_ref[rows(c), vlanes(h)], preferred_element_type=F32)
                 + lax.dot_general(qg_h[c, h], st_before[c, h], NT_DIMS, preferred_element_type=F32))
            y = o * lax.rsqrt(jnp.mean(o * o, axis=-1, keepdims=True) + NORM_EPS) * gn
            gr = gr_ref[rows(c), vlanes(h)].astype(F32)
            o_ref[rows(c), vlanes(h)] = (y * (gr * _sigmoid(gr))).astype(BF16)


def _gla_call(p, wa, ba, gn, B, S, tc):
    nt = S // tc
    row = lambda w: pl.BlockSpec((tc, w), lambda b, t: (b * nt + t, 0))
    full = lambda a: pl.BlockSpec(a.shape, lambda b, t: (0,) * a.ndim)
    kwidth = GLA_HEADS * GLA_DK
    vwidth = GLA_HEADS * GLA_DV
    return pl.pallas_call(
        functools.partial(_gla_kernel, tc=tc),
        grid=(B, nt),
        in_specs=[row(kwidth), row(kwidth), row(vwidth), row(LANES), row(vwidth), full(wa), full(ba), full(gn)],
        out_specs=row(vwidth),
        out_shape=jax.ShapeDtypeStruct((B * S, vwidth), BF16),
        scratch_shapes=[pltpu.VMEM((GLA_HEADS, GLA_DV, LANES), F32)],
        compiler_params=pltpu.CompilerParams(dimension_semantics=("parallel", "arbitrary"),
                                             vmem_limit_bytes=VMEM_LIMIT),
        name="gla",
    )(p["gq"], p["gk"], p["gv"], p["ga"], p["gr"], wa, ba, gn)


def _out_kernel(on_ref, og_ref, x_ref, gpre_ref, wz_ref, wn_ref, wg_ref, wo_ref, gp_ref, o_ref):
    d = x_ref.shape[1]
    nz_w = _MSRC["nz"][1]
    blocks = [slice(r0, r0 + TILE) for r0 in range(0, x_ref.shape[0], TILE)]
    ups = []
    for rs in blocks:
        x = x_ref[rs, :]
        h = ((x * lax.rsqrt(jnp.mean(x * x, axis=-1, keepdims=True) + NORM_EPS)) * gpre_ref[...]).astype(BF16)
        nz = jnp.dot(h, wz_ref[:, :nz_w], preferred_element_type=F32)
        mg = [jnp.dot(h, wz_ref[:, nz_w + c * d:nz_w + (c + 1) * d], preferred_element_type=F32) for c in range(2)]
        gated = on_ref[rs, :].astype(F32) * (nz * _sigmoid(nz))
        ups.append((jnp.dot(gated.astype(BF16), wn_ref[...], preferred_element_type=F32),
                    jnp.dot(og_ref[rs, :], wg_ref[...], preferred_element_type=F32), mg))
    for rs, (a, b, mg) in zip(blocks, ups):
        y = _sigmoid(mg[0]) * a + _sigmoid(mg[1]) * b
        out = jnp.dot(y.astype(BF16), wo_ref[...], preferred_element_type=F32)
        r = out * lax.rsqrt(jnp.mean(out * out, axis=-1, keepdims=True) + NORM_EPS)
        o_ref[rs, :] = x_ref[rs, :] + r * gp_ref[...]


def _out_call(o_nsa, o_gla, xf, g_pre, wz, wn, wg, wo, gp, tm):
    n, d = xf.shape
    row = lambda w: pl.BlockSpec((tm, w), lambda i: (i, 0))
    full = lambda a: pl.BlockSpec(a.shape, lambda i: (0,) * a.ndim)
    return pl.pallas_call(
        _out_kernel,
        grid=(n // tm,),
        in_specs=[row(o_nsa.shape[1]), row(o_gla.shape[1]), row(d),
                  full(g_pre), full(wz), full(wn), full(wg), full(wo), full(gp)],
        out_specs=row(d),
        out_shape=jax.ShapeDtypeStruct((n, d), F32),
        compiler_params=pltpu.CompilerParams(dimension_semantics=("parallel",), vmem_limit_bytes=VMEM_LIMIT),
        name="merge_out",
    )(o_nsa, o_gla, xf, g_pre, wz, wn, wg, wo, gp)


def _position_tables(S):
    inv_freq = ROPE_THETA ** (-jnp.arange(0, HEAD_DIM, 2, dtype=F32) / HEAD_DIM)
    step = SLC_BLOCK
    a_hi = (step * jnp.arange(S // step, dtype=F32))[:, None] * inv_freq[None, :]
    a_lo = jnp.arange(step, dtype=F32)[:, None] * inv_freq[None, :]
    c_hi, s_hi, c_lo, s_lo = jnp.cos(a_hi)[:, None, :], jnp.sin(a_hi)[:, None, :], jnp.cos(a_lo)[None], jnp.sin(a_lo)[None]
    cos = (c_hi * c_lo - s_hi * s_lo).reshape(S, HEAD_DIM // 2)
    sin = (s_hi * c_lo + c_hi * s_lo).reshape(S, HEAD_DIM // 2)
    cosf = jnp.concatenate([cos] * 4, axis=1)
    sinf = jnp.concatenate([-sin, sin, -sin, sin], axis=1)
    blk = np.arange(S) // SLC_BLOCK
    noh = np.zeros((S, LANES), np.float32)
    noh[np.arange(S), HEAD_DIM + blk] = -MASK_BIG
    nb, nc = S // SLC_BLOCK, S // CMP_STRIDE
    cs = CMP_STRIDE * np.arange(nc)
    bs = SLC_BLOCK * np.arange(nb)
    ov = (cs[None, :] < bs[:, None] + SLC_BLOCK) & (cs[None, :] + CMP_BLOCK > bs[:, None])
    ov[:, (S - CMP_BLOCK) // CMP_STRIDE + 1:] = False
    return cosf, sinf, jnp.asarray(noh), jnp.asarray(ov.astype(np.float32)).astype(BF16)


def _compress_weights(pos_k, w1_k, w2_k, pos_v, w1_v, w2_v):
    assert NSA_GROUPS == 2

    def block_diag2(w):
        z = jnp.zeros_like(w)
        return jnp.concatenate([jnp.concatenate([w, z], axis=-1), jnp.concatenate([z, w], axis=-1)], axis=-2)

    pad_k = jnp.concatenate([w2_k, jnp.zeros((CMP_HIDDEN, LANES - HEAD_DIM), w2_k.dtype)], axis=1)
    return (jnp.concatenate([pos_k, pos_k], axis=1), jnp.concatenate([pos_v, pos_v], axis=1),
            block_diag2(w1_k).astype(BF16), block_diag2(w1_v).astype(BF16),
            block_diag2(pad_k).astype(BF16),
            block_diag2(w2_v).astype(BF16))


def _layer(xf, tabs, B, S, g_pre, w_in, cmp_pos_k, cmp_w1_k, cmp_w2_k, cmp_pos_v, cmp_w1_v, cmp_w2_v,
           gla_w_a, gla_b_a, gla_g_norm, w_up_nsa, w_up_gla, w_out, g_post):
    cosf, sinf, noh, ovt = tabs
    d = xf.shape[1]
    p = _proj_call(xf, g_pre.reshape(1, d), _pad_proj_weight(w_in), cosf, sinf, noh, S, tm=ROW_BLOCK)

    cmp_w = _compress_weights(cmp_pos_k, cmp_w1_k, cmp_w2_k, cmp_pos_v, cmp_w1_v, cmp_w2_v)
    kc, vct = _cmp_call(p["kcr"], p["vcr"], cmp_w, B, S)
    o_nsa = _nsa_call(p, kc, vct, ovt, B, S, qt=TILE)

    wa = jnp.zeros((LANES, gla_w_a.shape[1]), F32).at[:GLA_RANK].set(gla_w_a)
    o_gla = _gla_call(p, wa, gla_b_a.reshape(1, -1), gla_g_norm.reshape(1, GLA_DV), B, S, tc=ROW_BLOCK)

    return _out_call(o_nsa, o_gla, xf, g_pre.reshape(1, d), _merge_gate_weight(w_in), w_up_nsa.astype(BF16),
                     w_up_gla.astype(BF16), w_out.astype(BF16), g_post.reshape(1, d), tm=ROW_BLOCK)


def kernel(x, g_pre, w_in, cmp_pos_k, cmp_w1_k, cmp_w2_k, cmp_pos_v, cmp_w1_v, cmp_w2_v, gla_w_a, gla_b_a,
           gla_g_norm, w_up_nsa, w_up_gla, w_out, g_post):
    B, S, d = x.shape
    assert d == D_MODEL and S % 2048 == 0 and S // SLC_BLOCK <= LANES - HEAD_DIM
    tabs = _position_tables(S)
    xf = x.reshape(B * S, d)
    for l in range(g_pre.shape[0]):
        xf = _layer(xf, tabs, B, S, g_pre[l], w_in[l], cmp_pos_k[l], cmp_w1_k[l], cmp_w2_k[l],
                    cmp_pos_v[l], cmp_w1_v[l], cmp_w2_v[l], gla_w_a[l], gla_b_a[l], gla_g_norm[l],
                    w_up_nsa[l], w_up_gla[l], w_out[l], g_post[l])
    return xf.reshape(B, S, d)
```

```python
import functools

import numpy as np
import jax
import jax.numpy as jnp
from jax import lax
from jax.experimental import pallas as pl
from jax.experimental.pallas import tpu as pltpu

F32 = jnp.float32
BF16 = jnp.bfloat16

D_MODEL = 1024
NSA_HEADS = 8
NSA_GROUPS = 2
HPG = NSA_HEADS // NSA_GROUPS
HEAD_DIM = 64
CMP_BLOCK = 32
CMP_STRIDE = 16
CMP_HIDDEN = 128
SLC_BLOCK = 64
SLC_TOPK = 16
WINDOW = 512
GLA_HEADS = 4
GLA_DK = 64
GLA_DV = 128
GLA_RANK = 16
GLA_TAU = 16.0
GLA_CHUNK = 64
ROPE_THETA = 10000.0
NORM_EPS = 1e-6
QK_SCALE = HEAD_DIM ** -0.5
Q_PRESCALE = QK_SCALE * 1.4426950408889634
GLA_SCALE = GLA_DK ** -0.5

LANES = 128
MXU_COLS = 256
BF16_SUBLANES = 16
TILE = 256
ROW_BLOCK = 4 * TILE
GATE_ROWS = 32
V_ROWS = HEAD_DIM + BF16_SUBLANES
MASK_BIG = 1e30
VMEM_LIMIT = 56 * 1024 * 1024

NT_DIMS = (((1,), (1,)), ((), ()))
TN_DIMS = (((0,), (0,)), ((), ()))

_SPLITS = (("q", 512), ("kcvc", 256), ("ks", 128), ("vs", 128), ("kw", 128), ("vw", 128),
           ("ng", 24), ("nz", 512), ("gq", 256), ("gk", 256), ("gv", 512), ("ga", 16), ("gr", 512),
           ("mg", 2048))


_MERGE_SEGS = ("nz", "mg")


def _proj_layout():
    src, pofs, msrc, o_src, o_dst = [], {}, {}, 0, 0
    for name, width in _SPLITS:
        if name in _MERGE_SEGS:
            msrc[name] = (o_src, width)
        else:
            padded = -(-width // LANES) * LANES
            src.append((o_src, width, padded - width))
            pofs[name] = (o_dst, padded)
            o_dst += padded
        o_src += width
    return src, pofs, o_dst, msrc


_PSRC, _POFS, NP_COLS, _MSRC = _proj_layout()


def _pad_proj_weight(w):
    parts = []
    for start, width, pad in _PSRC:
        parts.append(w[:, start:start + width])
        if pad:
            parts.append(jnp.zeros((w.shape[0], pad), w.dtype))
    return jnp.concatenate(parts, axis=1).astype(BF16)


def _merge_gate_weight(w):
    return jnp.concatenate([w[:, s:s + n] for s, n in (_MSRC[name] for name in _MERGE_SEGS)], axis=1).astype(BF16)


def _sigmoid(v):
    return 1.0 / (1.0 + jnp.exp(-v))


def _proj_kernel(x_ref, g_ref, w_ref, cos_ref, sin_ref, noh_ref,
                 qn_ref, qr_ref, kcr_ref, vcr_ref, ksa_ref, kwr_ref, vst_ref, vwt_ref, gate_ref,
                 gq_ref, gk_ref, gv_ref, ga_ref, gr_ref):
    for blk in range(x_ref.shape[0] // TILE):
        _proj_block(blk, slice(blk * TILE, (blk + 1) * TILE), x_ref, g_ref, w_ref, cos_ref, sin_ref, noh_ref,
                    qn_ref, qr_ref, kcr_ref, vcr_ref, ksa_ref, kwr_ref, vst_ref, vwt_ref, gate_ref,
                    gq_ref, gk_ref, gv_ref, ga_ref, gr_ref)


def _proj_block(blk, rs, x_ref, g_ref, w_ref, cos_ref, sin_ref, noh_ref,
                qn_ref, qr_ref, kcr_ref, vcr_ref, ksa_ref, kwr_ref, vst_ref, vwt_ref, gate_ref,
                gq_ref, gk_ref, gv_ref, ga_ref, gr_ref):
    x = x_ref[rs, :]
    y = x * lax.rsqrt(jnp.mean(x * x, axis=-1, keepdims=True) + NORM_EPS)
    h = (y * g_ref[...]).astype(BF16)
    cos = cos_ref[rs, :]
    sin = sin_ref[rs, :]
    tm = TILE
    lane = lax.broadcasted_iota(jnp.int32, (tm, LANES), 1)
    lo_half = (lane & (HEAD_DIM - 1)) < (HEAD_DIM // 2)
    first_head = lane < HEAD_DIM

    def rope(v):
        rot = jnp.where(lo_half, pltpu.roll(v, LANES - HEAD_DIM // 2, 1), pltpu.roll(v, HEAD_DIM // 2, 1))
        return v * cos + rot * sin

    windows = {}

    def chunks(name):
        start, width = _POFS[name]
        for c in range(width // LANES):
            col = start + c * LANES
            w = col // MXU_COLS
            if w not in windows:
                windows[w] = jnp.dot(h, w_ref[:, w * MXU_COLS:(w + 1) * MXU_COLS], preferred_element_type=F32)
            off = col % MXU_COLS
            yield c, windows[w][:, off:off + LANES]

    for pr, v in chunks("q"):
        n_t = (v * Q_PRESCALE).T.astype(BF16)
        r_t = (rope(v) * Q_PRESCALE).T.astype(BF16)
        for e in range(2):
            qn_ref[blk, 2 * pr + e] = n_t[e * HEAD_DIM:(e + 1) * HEAD_DIM]
            qr_ref[blk, 2 * pr + e] = r_t[e * HEAD_DIM:(e + 1) * HEAD_DIM]
    for (_, v), ref in zip(chunks("kcvc"), (kcr_ref, vcr_ref)):
        ref[rs, :] = v
    noh = noh_ref[rs, :]
    for name, ref, upper in (("ks", ksa_ref, noh), ("kw", kwr_ref, jnp.zeros_like(noh))):
        for _, v in chunks(name):
            r = rope(v)
            ref[rs, :LANES] = jnp.where(first_head, r, upper).astype(BF16)
            ref[rs, LANES:] = jnp.where(first_head, pltpu.roll(r, HEAD_DIM, 1), upper).astype(BF16)
    row16 = lax.broadcasted_iota(jnp.int32, (BF16_SUBLANES, tm), 0)
    ones_rows = jnp.where(row16 == 0, 1.0, 0.0)
    for name, ref in (("vs", vst_ref), ("vw", vwt_ref)):
        for _, v in chunks(name):
            v_t = v.T
            for g in range(NSA_GROUPS):
                ref[g, blk] = jnp.concatenate([v_t[g * HEAD_DIM:(g + 1) * HEAD_DIM], ones_rows], axis=0).astype(BF16)
    for _, v in chunks("ng"):
        gate_ref[blk] = _sigmoid(v).T[:GATE_ROWS]
    for name, ref in (("gq", gq_ref), ("gk", gk_ref), ("gv", gv_ref), ("gr", gr_ref)):
        for c, v in chunks(name):
            ref[rs, c * LANES:(c + 1) * LANES] = v.astype(BF16)
    for _, v in chunks("ga"):
        ga_ref[rs, :] = v


def _proj_call(xf, g_pre, w_p, cosf, sinf, noh, S, tm):
    n, d = xf.shape
    nt = n // tm
    spt = S // tm
    row = lambda width: pl.BlockSpec((tm, width), lambda i: (i, 0))
    pos = pl.BlockSpec((tm, LANES), lambda i: (i % spt, 0))
    nb_ = tm // TILE
    ntile = n // TILE
    qt_shape = ((ntile, NSA_HEADS, HEAD_DIM, TILE), (nb_, NSA_HEADS, HEAD_DIM, TILE), lambda i: (i, 0, 0, 0))
    vt_shape = ((NSA_GROUPS, ntile, V_ROWS, TILE), (NSA_GROUPS, nb_, V_ROWS, TILE), lambda i: (0, i, 0, 0))
    gt_shape = ((ntile, GATE_ROWS, TILE), (nb_, GATE_ROWS, TILE), lambda i: (i, 0, 0))
    outs = [("qn", qt_shape, BF16), ("qr", qt_shape, BF16), ("kcr", 128, F32), ("vcr", 128, F32),
            ("ksa", 256, BF16),
            ("kwr", 256, BF16), ("vst", vt_shape, BF16), ("vwt", vt_shape, BF16), ("gate", gt_shape, F32),
            ("gq", 256, BF16), ("gk", 256, BF16), ("gv", 512, BF16), ("ga", 128, F32), ("gr", 512, BF16)]
    out_specs, out_shape = [], []
    for _, sh, dt in outs:
        if isinstance(sh, int):
            out_specs.append(row(sh))
            out_shape.append(jax.ShapeDtypeStruct((n, sh), dt))
        else:
            out_specs.append(pl.BlockSpec(sh[1], sh[2]))
            out_shape.append(jax.ShapeDtypeStruct(sh[0], dt))
    res = pl.pallas_call(
        _proj_kernel,
        grid=(nt,),
        in_specs=[row(d),
                  pl.BlockSpec((1, d), lambda i: (0, 0)),
                  pl.BlockSpec((d, NP_COLS), lambda i: (0, 0), pipeline_mode=pl.Buffered(1)),
                  pos, pos, pos],
        out_specs=out_specs,
        out_shape=out_shape,
        compiler_params=pltpu.CompilerParams(dimension_semantics=("parallel",),
                                             vmem_limit_bytes=VMEM_LIMIT),
        name="proj",
    )(xf, g_pre, w_p, cosf, sinf, noh)
    return {nm: r for (nm, _, _), r in zip(outs, res)}


def _cmp_kernel(xk_ref, xv_ref, pk_ref, pv_ref, w1k_ref, w1v_ref, w2k_ref, w2v_ref, kc_ref, vct_ref):
    nc = xk_ref.shape[0] // CMP_STRIDE

    def hidden(x_ref, p_ref, w1_ref):
        a = None
        b = None
        for l in range(CMP_STRIDE):
            x = x_ref[pl.ds(l, nc, stride=CMP_STRIDE), :]
            l2 = CMP_STRIDE + l
            da = jnp.dot((x + p_ref[l:l + 1, :]).astype(BF16), w1_ref[l], preferred_element_type=F32)
            db = jnp.dot((x + p_ref[l2:l2 + 1, :]).astype(BF16), w1_ref[l2], preferred_element_type=F32)
            a = da if a is None else a + da
            b = db if b is None else b + db
        hid = a + pltpu.roll(b, nc - 1, 0)
        return (hid * _sigmoid(hid)).astype(BF16)

    kc_ref[0] = jnp.dot(hidden(xk_ref, pk_ref, w1k_ref), w2k_ref[...], preferred_element_type=F32).astype(BF16)
    v_t = jnp.dot(hidden(xv_ref, pv_ref, w1v_ref), w2v_ref[...], preferred_element_type=F32).T
    for g in range(NSA_GROUPS):
        vct_ref[0, g] = v_t[g * HEAD_DIM:(g + 1) * HEAD_DIM].astype(BF16)


def _cmp_call(kcr, vcr, weights, B, S):
    nc = S // CMP_STRIDE
    full = lambda a: pl.BlockSpec(a.shape, lambda b: (0,) * a.ndim)
    kw = NSA_GROUPS * LANES
    return pl.pallas_call(
        _cmp_kernel,
        grid=(B,),
        in_specs=[pl.BlockSpec((S, LANES), lambda b: (b, 0)), pl.BlockSpec((S, LANES), lambda b: (b, 0))]
        + [full(w) for w in weights],
        out_specs=[pl.BlockSpec((1, nc, kw), lambda b: (b, 0, 0)),
                   pl.BlockSpec((1, NSA_GROUPS, HEAD_DIM, nc), lambda b: (b, 0, 0, 0))],
        out_shape=[jax.ShapeDtypeStruct((B, nc, kw), BF16),
                   jax.ShapeDtypeStruct((B, NSA_GROUPS, HEAD_DIM, nc), BF16)],
        compiler_params=pltpu.CompilerParams(dimension_semantics=("parallel",), vmem_limit_bytes=VMEM_LIMIT),
        name="compress",
    )(kcr, vcr, *weights)


def _nsa_kernel(qn_ref, qr_ref, gate_ref, kc_ref, vct_ref, ksa_ref, vst_ref, kw_ref, vwt_ref, ovt_ref, o_ref,
                qa_sc, acc_sc, m_sc, out_sc, s_sc, smax_sc, *, qt, nb, nc):
    i = pl.program_id(1)
    s0 = i * qt
    hd = HEAD_DIM
    heads = range(NSA_HEADS)
    grp_of = lambda h: h // HPG

    def gate_row(branch, h):
        return gate_ref[0, h * 3 + branch:h * 3 + branch + 1, :]

    def lanes_of(g):
        return slice(g * LANES, (g + 1) * LANES)

    r_i = lax.broadcasted_iota(jnp.int32, (qt, qt), 0)
    c_i = lax.broadcasted_iota(jnp.int32, (qt, qt), 1)

    def reset():
        acc_sc[...] = jnp.zeros(acc_sc.shape, F32)
        m_sc[...] = jnp.full(m_sc.shape, -jnp.inf, F32)

    def scores(k, q_of, h):
        s = jnp.dot(k, q_of(h), preferred_element_type=F32)
        return s, jnp.max(s, axis=0, keepdims=True)

    def fold(h, s_and_max, vt, mask, bias=None):
        s, s_max = s_and_max
        if mask is not None:
            s = jnp.where(mask, s, -MASK_BIG)
            s_max = jnp.max(s, axis=0, keepdims=True)
        if bias is not None:
            s_max = s_max + bias
        m_old = m_sc[h]
        m_new = jnp.maximum(m_old, s_max)
        alpha = jnp.exp2(m_old - m_new)
        pe = jnp.exp2(s - (m_new if bias is None else m_new - bias)).astype(BF16)
        acc_sc[h] = alpha * acc_sc[h] + jnp.dot(vt, pe, preferred_element_type=F32)
        m_sc[h] = m_new

    def tile(pending, vt, q_of, mask, k_next, bias=None):
        nxt = []
        for h in heads:
            if k_next is not None:
                nxt.append(scores(k_next[grp_of(h)], q_of, h))
            fold(h, pending[h], vt[grp_of(h)], mask, bias)
        return tuple(nxt)

    def first_scores(k, q_of):
        return tuple(scores(k[grp_of(h)], q_of, h) for h in heads)

    def finish(branch):
        for h in heads:
            scale = gate_row(branch, h) * (1.0 / acc_sc[h, hd:hd + 1, :])
            out_sc[h * hd:(h + 1) * hd, :] += acc_sc[h, :hd, :] * scale

    def rows(j):
        return pl.ds(pl.multiple_of(j * qt, qt), qt)

    for h in heads:
        qa_sc[h, :hd, :] = qr_ref[0, h]
    topk = min(SLC_TOPK, nb)
    quarters = 4
    assert nc % (quarters * BF16_SUBLANES) == 0 and nb % (quarters * 8) == 0

    def select_blocks(nc_e, nb_e):
        jj = lax.broadcasted_iota(jnp.int32, (nc_e, qt), 0)
        tt = s0 + lax.broadcasted_iota(jnp.int32, (nc_e, qt), 1)
        cmask = (CMP_STRIDE * jj + (CMP_BLOCK - 1)) <= tt
        kc = [kc_ref[0, :nc_e, g * LANES:g * LANES + hd] for g in range(NSA_GROUPS)]
        has_key = (s0 + lax.broadcasted_iota(jnp.int32, (1, qt), 1)) >= (CMP_BLOCK - 1)
        psum = [None] * NSA_GROUPS
        s_all = [jnp.dot(kc[grp_of(h)], qn_ref[0, h], preferred_element_type=F32) for h in heads]
        for h in heads:
            g = grp_of(h)
            s = jnp.where(cmask, s_all[h], -MASK_BIG)
            e = jnp.exp2(s - jnp.max(s, axis=0, keepdims=True))
            inv = jnp.where(has_key, 1.0 / jnp.sum(e, axis=0, keepdims=True), 0.0)
            p = e * inv
            o = jnp.dot(vct_ref[0, g, :, :nc_e], p.astype(BF16), preferred_element_type=F32)
            out_sc[h * hd:(h + 1) * hd, :] = gate_row(0, h) * o
            psum[g] = p if psum[g] is None else psum[g] + p

        n_idx = lax.broadcasted_iota(jnp.int32, (nb_e, qt), 0)
        n_f = n_idx.astype(F32)
        cur = (s0 + lax.broadcasted_iota(jnp.int32, (nb_e, qt), 1)) // SLC_BLOCK
        forced = (n_idx == 0) | (n_idx == cur) | (n_idx == cur - 1)
        valid = n_idx <= cur
        free = valid & jnp.logical_not(forced)
        ovt = ovt_ref[:nb_e, :nc_e]
        for g in range(NSA_GROUPS):
            if nb_e <= topk:
                sel = valid
            else:
                hi = psum[g].astype(BF16)
                r1 = psum[g] - hi.astype(F32)
                mid = r1.astype(BF16)
                lo = (r1 - mid.astype(F32)).astype(BF16)
                imp_t = (jnp.dot(ovt, hi, preferred_element_type=F32) + jnp.dot(ovt, mid, preferred_element_type=F32)
                         + jnp.dot(ovt, lo, preferred_element_type=F32))
                val = jnp.where(free, imp_t, -jnp.inf)
                picked = jnp.zeros((nb_e, qt), F32)
                for _ in range(max(topk - 3, 0)):
                    top = jnp.max(val, axis=0, keepdims=True)
                    first = jnp.min(jnp.where(val == top, n_f, float(nb_e)), axis=0, keepdims=True)
                    pick = n_f == first
                    picked = jnp.where(pick, 1.0, picked)
                    val = jnp.where(pick, -jnp.inf, val)
                sel = (forced | (picked > 0.0)) & valid
            notsel = jnp.where(sel, 0.0, 1.0).astype(BF16)
            if nb_e < LANES - hd:
                notsel = jnp.concatenate([notsel, jnp.zeros((LANES - hd - nb_e, qt), BF16)], axis=0)
            for h in range(g * HPG, (g + 1) * HPG):
                qa_sc[h, hd:, :] = notsel

    quarter = (i * quarters) // (nc * CMP_STRIDE // qt)
    for qq in range(quarters):
        pl.when(quarter == qq)(functools.partial(select_blocks, nc * (qq + 1) // quarters, nb * (qq + 1) // quarters))


    reset()
    qr_of = lambda h: qr_ref[0, h]
    mask_a = (r_i - c_i) > jnp.where(i >= 2, 0, qt)
    bias_b = jnp.where(i >= 1, 0.0, -MASK_BIG)
    ja = jnp.maximum(i - 2, 0)
    jb = jnp.maximum(i - 1, 0)
    k_win = lambda j: [kw_ref[rows(j), g * LANES:g * LANES + hd] for g in range(NSA_GROUPS)]
    v_win = lambda j: [vwt_ref[g, j] for g in range(NSA_GROUPS)]
    kb, ka = k_win(jb), k_win(ja)
    carry = first_scores(k_win(i), qr_of)
    carry = tile(carry, v_win(i), qr_of, r_i <= c_i, kb)
    carry = tile(carry, v_win(jb), qr_of, None, ka, bias=bias_b)
    tile(carry, v_win(ja), qr_of, mask_a, None)
    finish(2)

    reset()
    qa_of = lambda h: qa_sc[h]
    k_slc = lambda j: [ksa_ref[rows(j), lanes_of(g)] for g in range(NSA_GROUPS)]
    v_slc = lambda j: [vst_ref[g, j] for g in range(NSA_GROUPS)]

    def slc_scores(j, dst):
        ks = k_slc(j)
        for h in heads:
            s, s_max = scores(ks[grp_of(h)], qa_of, h)
            s_sc[dst, h] = s
            smax_sc[dst, h] = s_max

    def slc_tile(j, src, dst, mask=None):
        ks = k_slc(j + 1) if dst is not None else None
        vt = v_slc(j)
        for h in heads:
            if dst is not None:
                s, s_max = scores(ks[grp_of(h)], qa_of, h)
                s_sc[dst, h] = s
                smax_sc[dst, h] = s_max
            fold(h, (s_sc[src, h], smax_sc[src, h]), vt[grp_of(h)], mask)

    slc_scores(0, 0)
    n_pairs = i // 2

    def slc_path(first_slot, peel):
        a, b = first_slot, 1 - first_slot

        def pair(p, carry):
            j = 2 * p + peel
            slc_tile(j, a, b)
            slc_tile(j + 1, b, a)
            return carry

        lax.fori_loop(0, n_pairs, pair, 0)
        slc_tile(i, a, None, r_i <= c_i)

    @pl.when(i % 2 == 1)
    def _():
        slc_tile(0, 0, 1)
        slc_path(1, 1)

    @pl.when(i % 2 == 0)
    def _():
        slc_path(0, 0)

    finish(1)

    o_ref[...] = out_sc[...].T.astype(o_ref.dtype)


def _nsa_call(p, kc, vct, ovt, B, S, qt):
    assert WINDOW == 2 * qt
    nq = S // qt
    nb = S // SLC_BLOCK
    nc = S // CMP_STRIDE
    qspec = pl.BlockSpec((1, NSA_HEADS, HEAD_DIM, qt), lambda b, i: (b * nq + i, 0, 0, 0))
    kspec = pl.BlockSpec((S, NSA_GROUPS * LANES), lambda b, i: (b, 0))
    vspec = pl.BlockSpec((NSA_GROUPS, nq, V_ROWS, qt), lambda b, i: (0, b, 0, 0))
    kernel = functools.partial(_nsa_kernel, qt=qt, nb=nb, nc=nc)
    return pl.pallas_call(
        kernel,
        grid=(B, nq),
        in_specs=[qspec, qspec,
                  pl.BlockSpec((1, GATE_ROWS, qt), lambda b, i: (b * nq + i, 0, 0)),
                  pl.BlockSpec((1, nc, NSA_GROUPS * LANES), lambda b, i: (b, 0, 0)),
                  pl.BlockSpec((1, NSA_GROUPS, HEAD_DIM, nc), lambda b, i: (b, 0, 0, 0)),
                  kspec, vspec, kspec, vspec,
                  pl.BlockSpec(ovt.shape, lambda b, i: (0, 0))],
        out_specs=pl.BlockSpec((qt, NSA_HEADS * HEAD_DIM), lambda b, i: (b * nq + i, 0)),
        out_shape=jax.ShapeDtypeStruct((B * S, NSA_HEADS * HEAD_DIM), BF16),
        scratch_shapes=[pltpu.VMEM((NSA_HEADS, LANES, qt), BF16),
                        pltpu.VMEM((NSA_HEADS, V_ROWS, qt), F32),
                        pltpu.VMEM((NSA_HEADS, 1, qt), F32),
                        pltpu.VMEM((NSA_HEADS * HEAD_DIM, qt), F32),
                        pltpu.VMEM((2, NSA_HEADS, qt, qt), F32),
                        pltpu.VMEM((2, NSA_HEADS, 1, qt), F32)],
        compiler_params=pltpu.CompilerParams(dimension_semantics=("parallel", "arbitrary"),
                                             vmem_limit_bytes=VMEM_LIMIT),
        name="nsa",
    )(p["qn"], p["qr"], p["gate"], kc, vct, p["ksa"], p["vst"], p["kwr"], p["vwt"], ovt)


def _gla_kernel(gq_ref, gk_ref, gv_ref, ga_ref, gr_ref, wa_ref, ba_ref, gn_ref, o_ref, st_ref, *, tc):
    @pl.when(pl.program_id(1) == 0)
    def _():
        st_ref[...] = jnp.zeros(st_ref.shape, F32)

    def split2(v):
        hi = v.astype(BF16)
        return hi, (v - hi.astype(F32)).astype(BF16)

    g_hi, g_lo = split2(ga_ref[...])
    w_hi, w_lo = split2(wa_ref[...])
    a = (jnp.dot(g_hi, w_hi, preferred_element_type=F32) + jnp.dot(g_hi, w_lo, preferred_element_type=F32)
         + jnp.dot(g_lo, w_hi, preferred_element_type=F32)) + ba_ref[...]
    log_a = (jnp.minimum(a, 0.0) - jnp.log1p(jnp.exp(-jnp.abs(a)))) * (1.0 / GLA_TAU)
    c_sz = GLA_CHUNK
    n_chunk = tc // c_sz
    chunks = range(n_chunk)
    heads = range(GLA_HEADS)
    r_t = lax.broadcasted_iota(jnp.int32, (MXU_COLS, MXU_COLS), 0)
    c_t = lax.broadcasted_iota(jnp.int32, (MXU_COLS, MXU_COLS), 1)
    tril = jnp.where((r_t >= c_t) & ((r_t // c_sz) == (c_t // c_sz)), 1.0, 0.0).astype(BF16)
    hi = log_a.astype(BF16)
    r1 = log_a - hi.astype(F32)
    mid = r1.astype(BF16)
    lo = (r1 - mid.astype(F32)).astype(BF16)
    bcum = jnp.concatenate(
        [jnp.dot(tril, hi[r0:r0 + MXU_COLS], preferred_element_type=F32)
         + jnp.dot(tril, mid[r0:r0 + MXU_COLS], preferred_element_type=F32)
         + jnp.dot(tril, lo[r0:r0 + MXU_COLS], preferred_element_type=F32)
         for r0 in range(0, tc, MXU_COLS)], axis=0)
    gk = gk_ref[...].astype(F32)
    qg = (gq_ref[...].astype(F32) * GLA_SCALE) * jnp.exp(bcum)
    kg = (gk * jnp.exp(-bcum)).astype(BF16)
    lane = lax.broadcasted_iota(jnp.int32, (1, LANES), 1)
    causal = (lax.broadcasted_iota(jnp.int32, (c_sz, c_sz), 0)
              >= lax.broadcasted_iota(jnp.int32, (c_sz, c_sz), 1))
    rows = lambda c: slice(c * c_sz, (c + 1) * c_sz)
    klanes = lambda h: slice((h // 2) * LANES, (h // 2 + 1) * LANES)
    vlanes = lambda h: slice(h * GLA_DV, (h + 1) * GLA_DV)
    kd, dec = [], []
    for c in chunks:
        bl = bcum[(c + 1) * c_sz - 1:(c + 1) * c_sz]
        kd.append((gk[rows(c)] * jnp.exp(bl - bcum[rows(c)])).astype(BF16))
        dec.append(jnp.exp(bl))
    qg_h, att, upd = {}, {}, {}
    pair = 2
    for c in chunks:
        for h0 in range(0, GLA_HEADS, pair):
            for h in range(h0, h0 + pair):
                own = (lane // GLA_DK) == (h % pair)
                qg_h[c, h] = jnp.where(own, qg[rows(c), klanes(h)], 0.0).astype(BF16)
            both = lax.dot_general(jnp.concatenate([qg_h[c, h0 + e] for e in range(pair)], axis=0),
                                   kg[rows(c), klanes(h0)], NT_DIMS, preferred_element_type=F32)
            for e in range(pair):
                att[c, h0 + e] = both[e * c_sz:(e + 1) * c_sz]
    for c in chunks:
        for h0 in range(0, GLA_HEADS, pair):
            both = lax.dot_general(gv_ref[rows(c), h0 * GLA_DV:(h0 + pair) * GLA_DV], kd[c][:, klanes(h0)], TN_DIMS,
                                   preferred_element_type=F32)
            for e in range(pair):
                upd[c, h0 + e] = both[e * GLA_DV:(e + 1) * GLA_DV]
    st_before = {}
    for h in heads:
        st = st_ref[h]
        for c in chunks:
            st_before[c, h] = st.astype(BF16)
            st = st * dec[c][:, klanes(h)] + upd[c, h]
        st_ref[h] = st
    gn = gn_ref[...]
    for c in chunks:
        for h in heads:
            a_c = jnp.where(causal, att[c, h], 0.0).astype(BF16)
            o = (jnp.dot(a_c, gv_ref[rows(c), vlanes(h)], preferred_element_type=F32)
                 + lax.dot_general(qg_h[c, h], st_before[c, h], NT_DIMS, preferred_element_type=F32))
            y = o * lax.rsqrt(jnp.mean(o * o, axis=-1, keepdims=True) + NORM_EPS) * gn
            gr = gr_ref[rows(c), vlanes(h)].astype(F32)
            o_ref[rows(c), vlanes(h)] = (y * (gr * _sigmoid(gr))).astype(BF16)


def _gla_call(p, wa, ba, gn, B, S, tc):
    nt = S // tc
    row = lambda w: pl.BlockSpec((tc, w), lambda b, t: (b * nt + t, 0))
    full = lambda a: pl.BlockSpec(a.shape, lambda b, t: (0,) * a.ndim)
    kwidth = GLA_HEADS * GLA_DK
    vwidth = GLA_HEADS * GLA_DV
    return pl.pallas_call(
        functools.partial(_gla_kernel, tc=tc),
        grid=(B, nt),
        in_specs=[row(kwidth), row(kwidth), row(vwidth), row(LANES), row(vwidth), full(wa), full(ba), full(gn)],
        out_specs=row(vwidth),
        out_shape=jax.ShapeDtypeStruct((B * S, vwidth), BF16),
        scratch_shapes=[pltpu.VMEM((GLA_HEADS, GLA_DV, LANES), F32)],
        compiler_params=pltpu.CompilerParams(dimension_semantics=("parallel", "arbitrary"),
                                             vmem_limit_bytes=VMEM_LIMIT),
        name="gla",
    )(p["gq"], p["gk"], p["gv"], p["ga"], p["gr"], wa, ba, gn)


def _out_kernel(on_ref, og_ref, x_ref, gpre_ref, wz_ref, wn_ref, wg_ref, wo_ref, gp_ref, o_ref):
    d = x_ref.shape[1]
    nz_w = _MSRC["nz"][1]
    blocks = [slice(r0, r0 + TILE) for r0 in range(0, x_ref.shape[0], TILE)]
    ups = []
    for rs in blocks:
        x = x_ref[rs, :]
        h = ((x * lax.rsqrt(jnp.mean(x * x, axis=-1, keepdims=True) + NORM_EPS)) * gpre_ref[...]).astype(BF16)
        nz = jnp.dot(h, wz_ref[:, :nz_w], preferred_element_type=F32)
        mg = [jnp.dot(h, wz_ref[:, nz_w + c * d:nz_w + (c + 1) * d], preferred_element_type=F32) for c in range(2)]
        gated = on_ref[rs, :].astype(F32) * (nz * _sigmoid(nz))
        ups.append((jnp.dot(gated.astype(BF16), wn_ref[...], preferred_element_type=F32),
                    jnp.dot(og_ref[rs, :], wg_ref[...], preferred_element_type=F32), mg))
    for rs, (a, b, mg) in zip(blocks, ups):
        y = _sigmoid(mg[0]) * a + _sigmoid(mg[1]) * b
        out = jnp.dot(y.astype(BF16), wo_ref[...], preferred_element_type=F32)
        r = out * lax.rsqrt(jnp.mean(out * out, axis=-1, keepdims=True) + NORM_EPS)
        o_ref[rs, :] = x_ref[rs, :] + r * gp_ref[...]


def _out_call(o_nsa, o_gla, xf, g_pre, wz, wn, wg, wo, gp, tm):
    n, d = xf.shape
    row = lambda w: pl.BlockSpec((tm, w), lambda i: (i, 0))
    full = lambda a: pl.BlockSpec(a.shape, lambda i: (0,) * a.ndim)
    return pl.pallas_call(
        _out_kernel,
        grid=(n // tm,),
        in_specs=[row(o_nsa.shape[1]), row(o_gla.shape[1]), row(d),
                  full(g_pre), full(wz), full(wn), full(wg), full(wo), full(gp)],
        out_specs=row(d),
        out_shape=jax.ShapeDtypeStruct((n, d), F32),
        compiler_params=pltpu.CompilerParams(dimension_semantics=("parallel",), vmem_limit_bytes=VMEM_LIMIT),
        name="merge_out",
    )(o_nsa, o_gla, xf, g_pre, wz, wn, wg, wo, gp)


def _position_tables(S):
    inv_freq = ROPE_THETA ** (-jnp.arange(0, HEAD_DIM, 2, dtype=F32) / HEAD_DIM)
    step = SLC_BLOCK
    a_hi = (step * jnp.arange(S // step, dtype=F32))[:, None] * inv_freq[None, :]
    a_lo = jnp.arange(step, dtype=F32)[:, None] * inv_freq[None, :]
    c_hi, s_hi, c_lo, s_lo = jnp.cos(a_hi)[:, None, :], jnp.sin(a_hi)[:, None, :], jnp.cos(a_lo)[None], jnp.sin(a_lo)[None]
    cos = (c_hi * c_lo - s_hi * s_lo).reshape(S, HEAD_DIM // 2)
    sin = (s_hi * c_lo + c_hi * s_lo).reshape(S, HEAD_DIM // 2)
    cosf = jnp.concatenate([cos] * 4, axis=1)
    sinf = jnp.concatenate([-sin, sin, -sin, sin], axis=1)
    blk = np.arange(S) // SLC_BLOCK
    noh = np.zeros((S, LANES), np.float32)
    noh[np.arange(S), HEAD_DIM + blk] = -MASK_BIG
    nb, nc = S // SLC_BLOCK, S // CMP_STRIDE
    cs = CMP_STRIDE * np.arange(nc)
    bs = SLC_BLOCK * np.arange(nb)
    ov = (cs[None, :] < bs[:, None] + SLC_BLOCK) & (cs[None, :] + CMP_BLOCK > bs[:, None])
    ov[:, (S - CMP_BLOCK) // CMP_STRIDE + 1:] = False
    return cosf, sinf, jnp.asarray(noh), jnp.asarray(ov.astype(np.float32)).astype(BF16)


def _compress_weights(pos_k, w1_k, w2_k, pos_v, w1_v, w2_v):
    assert NSA_GROUPS == 2

    def block_diag2(w):
        z = jnp.zeros_like(w)
        return jnp.concatenate([jnp.concatenate([w, z], axis=-1), jnp.concatenate([z, w], axis=-1)], axis=-2)

    pad_k = jnp.concatenate([w2_k, jnp.zeros((CMP_HIDDEN, LANES - HEAD_DIM), w2_k.dtype)], axis=1)
    return (jnp.concatenate([pos_k, pos_k], axis=1), jnp.concatenate([pos_v, pos_v], axis=1),
            block_diag2(w1_k).astype(BF16), block_diag2(w1_v).astype(BF16),
            block_diag2(pad_k).astype(BF16),
            block_diag2(w2_v).astype(BF16))


def _layer(xf, tabs, B, S, g_pre, w_in, cmp_pos_k, cmp_w1_k, cmp_w2_k, cmp_pos_v, cmp_w1_v, cmp_w2_v,
           gla_w_a, gla_b_a, gla_g_norm, w_up_nsa, w_up_gla, w_out, g_post):
    cosf, sinf, noh, ovt = tabs
    d = xf.shape[1]
    p = _proj_call(xf, g_pre.reshape(1, d), _pad_proj_weight(w_in), cosf, sinf, noh, S, tm=ROW_BLOCK)

    cmp_w = _compress_weights(cmp_pos_k, cmp_w1_k, cmp_w2_k, cmp_pos_v, cmp_w1_v, cmp_w2_v)
    kc, vct = _cmp_call(p["kcr"], p["vcr"], cmp_w, B, S)
    o_nsa = _nsa_call(p, kc, vct, ovt, B, S, qt=TILE)

    wa = jnp.zeros((LANES, gla_w_a.shape[1]), F32).at[:GLA_RANK].set(gla_w_a)
    o_gla = _gla_call(p, wa, gla_b_a.reshape(1, -1), gla_g_norm.reshape(1, GLA_DV), B, S, tc=ROW_BLOCK)

    return _out_call(o_nsa, o_gla, xf, g_pre.reshape(1, d), _merge_gate_weight(w_in), w_up_nsa.astype(BF16),
                     w_up_gla.astype(BF16), w_out.astype(BF16), g_post.reshape(1, d), tm=ROW_BLOCK)


def kernel(x, g_pre, w_in, cmp_pos_k, cmp_w1_k, cmp_w2_k, cmp_pos_v, cmp_w1_v, cmp_w2_v, gla_w_a, gla_b_a,
           gla_g_norm, w_up_nsa, w_up_gla, w_out, g_post):
    B, S, d = x.shape
    assert d == D_MODEL and S % 2048 == 0 and S // SLC_BLOCK <= LANES - HEAD_DIM
    tabs = _position_tables(S)
    xf = x.reshape(B * S, d)
    for l in range(g_pre.shape[0]):
        xf = _layer(xf, tabs, B, S, g_pre[l], w_in[l], cmp_pos_k[l], cmp_w1_k[l], cmp_w2_k[l],
                    cmp_pos_v[l], cmp_w1_v[l], cmp_w2_v[l], gla_w_a[l], gla_b_a[l], gla_g_norm[l],
                    w_up_nsa[l], w_up_gla[l], w_out[l], g_post[l])
    return xf.reshape(B, S, d)
```

```python
import functools

import numpy as np
import jax
import jax.numpy as jnp
from jax import lax
from jax.experimental import pallas as pl
from jax.experimental.pallas import tpu as pltpu

F32 = jnp.float32
BF16 = jnp.bfloat16

D_MODEL = 1024
NSA_HEADS = 8
NSA_GROUPS = 2
HPG = NSA_HEADS // NSA_GROUPS
HEAD_DIM = 64
CMP_BLOCK = 32
CMP_STRIDE = 16
CMP_HIDDEN = 128
SLC_BLOCK = 64
SLC_TOPK = 16
WINDOW = 512
GLA_HEADS = 4
GLA_DK = 64
GLA_DV = 128
GLA_RANK = 16
GLA_TAU = 16.0
GLA_CHUNK = 64
ROPE_THETA = 10000.0
NORM_EPS = 1e-6
QK_SCALE = HEAD_DIM ** -0.5
Q_PRESCALE = QK_SCALE * 1.4426950408889634
GLA_SCALE = GLA_DK ** -0.5

LANES = 128
MXU_COLS = 256
BF16_SUBLANES = 16
TILE = 256
ROW_BLOCK = 4 * TILE
GATE_ROWS = 32
V_ROWS = HEAD_DIM + BF16_SUBLANES
SLOT_SKEW_ROWS = 8
MASK_BIG = 1e30
VMEM_LIMIT = 56 * 1024 * 1024

NT_DIMS = (((1,), (1,)), ((), ()))
TN_DIMS = (((0,), (0,)), ((), ()))

_SPLITS = (("q", 512), ("kcvc", 256), ("ks", 128), ("vs", 128), ("kw", 128), ("vw", 128),
           ("ng", 24), ("nz", 512), ("gq", 256), ("gk", 256), ("gv", 512), ("ga", 16), ("gr", 512),
           ("mg", 2048))


_MERGE_SEGS = ("nz", "mg")


def _proj_layout():
    src, pofs, msrc, o_src, o_dst = [], {}, {}, 0, 0
    for name, width in _SPLITS:
        if name in _MERGE_SEGS:
            msrc[name] = (o_src, width)
        else:
            padded = -(-width // LANES) * LANES
            src.append((o_src, width, padded - width))
            pofs[name] = (o_dst, padded)
            o_dst += padded
        o_src += width
    return src, pofs, o_dst, msrc


_PSRC, _POFS, NP_COLS, _MSRC = _proj_layout()


def _pad_proj_weight(w):
    parts = []
    for start, width, pad in _PSRC:
        parts.append(w[:, start:start + width])
        if pad:
            parts.append(jnp.zeros((w.shape[0], pad), w.dtype))
    return jnp.concatenate(parts, axis=1).astype(BF16)


def _merge_gate_weight(w):
    return jnp.concatenate([w[:, s:s + n] for s, n in (_MSRC[name] for name in _MERGE_SEGS)], axis=1).astype(BF16)


def _sigmoid(v):
    return 1.0 / (1.0 + jnp.exp(-v))


def _proj_kernel(x_ref, g_ref, w_ref, cos_ref, sin_ref, noh_ref,
                 qn_ref, qr_ref, kcr_ref, vcr_ref, ksa_ref, kwr_ref, vst_ref, vwt_ref, gate_ref,
                 gq_ref, gk_ref, gv_ref, ga_ref, gr_ref):
    for blk in range(x_ref.shape[0] // TILE):
        _proj_block(blk, slice(blk * TILE, (blk + 1) * TILE), x_ref, g_ref, w_ref, cos_ref, sin_ref, noh_ref,
                    qn_ref, qr_ref, kcr_ref, vcr_ref, ksa_ref, kwr_ref, vst_ref, vwt_ref, gate_ref,
                    gq_ref, gk_ref, gv_ref, ga_ref, gr_ref)


def _proj_block(blk, rs, x_ref, g_ref, w_ref, cos_ref, sin_ref, noh_ref,
                qn_ref, qr_ref, kcr_ref, vcr_ref, ksa_ref, kwr_ref, vst_ref, vwt_ref, gate_ref,
                gq_ref, gk_ref, gv_ref, ga_ref, gr_ref):
    x = x_ref[rs, :]
    y = x * lax.rsqrt(jnp.mean(x * x, axis=-1, keepdims=True) + NORM_EPS)
    h = (y * g_ref[...]).astype(BF16)
    cos = cos_ref[rs, :]
    sin = sin_ref[rs, :]
    tm = TILE
    lane = lax.broadcasted_iota(jnp.int32, (tm, LANES), 1)
    lo_half = (lane & (HEAD_DIM - 1)) < (HEAD_DIM // 2)
    first_head = lane < HEAD_DIM

    def rope(v):
        rot = jnp.where(lo_half, pltpu.roll(v, LANES - HEAD_DIM // 2, 1), pltpu.roll(v, HEAD_DIM // 2, 1))
        return v * cos + rot * sin

    windows = {}

    def chunks(name):
        start, width = _POFS[name]
        for c in range(width // LANES):
            col = start + c * LANES
            w = col // MXU_COLS
            if w not in windows:
                windows[w] = jnp.dot(h, w_ref[:, w * MXU_COLS:(w + 1) * MXU_COLS], preferred_element_type=F32)
            off = col % MXU_COLS
            yield c, windows[w][:, off:off + LANES]

    for pr, v in chunks("q"):
        n_t = (v * Q_PRESCALE).T.astype(BF16)
        r_t = (rope(v) * Q_PRESCALE).T.astype(BF16)
        for e in range(2):
            qn_ref[blk, 2 * pr + e] = n_t[e * HEAD_DIM:(e + 1) * HEAD_DIM]
            qr_ref[blk, 2 * pr + e] = r_t[e * HEAD_DIM:(e + 1) * HEAD_DIM]
    for (_, v), ref in zip(chunks("kcvc"), (kcr_ref, vcr_ref)):
        ref[rs, :] = v
    noh = noh_ref[rs, :]
    for name, ref, upper in (("ks", ksa_ref, noh), ("kw", kwr_ref, jnp.zeros_like(noh))):
        for _, v in chunks(name):
            r = rope(v)
            ref[rs, :LANES] = jnp.where(first_head, r, upper).astype(BF16)
            ref[rs, LANES:] = jnp.where(first_head, pltpu.roll(r, HEAD_DIM, 1), upper).astype(BF16)
    row16 = lax.broadcasted_iota(jnp.int32, (BF16_SUBLANES, tm), 0)
    ones_rows = jnp.where(row16 == 0, 1.0, 0.0)
    for name, ref in (("vs", vst_ref), ("vw", vwt_ref)):
        for _, v in chunks(name):
            v_t = v.T
            for g in range(NSA_GROUPS):
                ref[g, blk] = jnp.concatenate([v_t[g * HEAD_DIM:(g + 1) * HEAD_DIM], ones_rows], axis=0).astype(BF16)
    for _, v in chunks("ng"):
        gate_ref[blk] = _sigmoid(v).T[:GATE_ROWS]
    for name, ref in (("gq", gq_ref), ("gk", gk_ref), ("gv", gv_ref), ("gr", gr_ref)):
        for c, v in chunks(name):
            ref[rs, c * LANES:(c + 1) * LANES] = v.astype(BF16)
    for _, v in chunks("ga"):
        ga_ref[rs, :] = v


def _proj_call(xf, g_pre, w_p, cosf, sinf, noh, S, tm):
    n, d = xf.shape
    nt = n // tm
    spt = S // tm
    row = lambda width: pl.BlockSpec((tm, width), lambda i: (i, 0))
    pos = pl.BlockSpec((tm, LANES), lambda i: (i % spt, 0))
    nb_ = tm // TILE
    ntile = n // TILE
    qt_shape = ((ntile, NSA_HEADS, HEAD_DIM, TILE), (nb_, NSA_HEADS, HEAD_DIM, TILE), lambda i: (i, 0, 0, 0))
    vt_shape = ((NSA_GROUPS, ntile, V_ROWS, TILE), (NSA_GROUPS, nb_, V_ROWS, TILE), lambda i: (0, i, 0, 0))
    gt_shape = ((ntile, GATE_ROWS, TILE), (nb_, GATE_ROWS, TILE), lambda i: (i, 0, 0))
    outs = [("qn", qt_shape, BF16), ("qr", qt_shape, BF16), ("kcr", 128, F32), ("vcr", 128, F32),
            ("ksa", 256, BF16),
            ("kwr", 256, BF16), ("vst", vt_shape, BF16), ("vwt", vt_shape, BF16), ("gate", gt_shape, F32),
            ("gq", 256, BF16), ("gk", 256, BF16), ("gv", 512, BF16), ("ga", 128, F32), ("gr", 512, BF16)]
    out_specs, out_shape = [], []
    for _, sh, dt in outs:
        if isinstance(sh, int):
            out_specs.append(row(sh))
            out_shape.append(jax.ShapeDtypeStruct((n, sh), dt))
        else:
            out_specs.append(pl.BlockSpec(sh[1], sh[2]))
            out_shape.append(jax.ShapeDtypeStruct(sh[0], dt))
    res = pl.pallas_call(
        _proj_kernel,
        grid=(nt,),
        in_specs=[row(d),
                  pl.BlockSpec((1, d), lambda i: (0, 0)),
                  pl.BlockSpec((d, NP_COLS), lambda i: (0, 0), pipeline_mode=pl.Buffered(1)),
                  pos, pos, pos],
        out_specs=out_specs,
        out_shape=out_shape,
        compiler_params=pltpu.CompilerParams(dimension_semantics=("parallel",),
                                             vmem_limit_bytes=VMEM_LIMIT),
        name="proj",
    )(xf, g_pre, w_p, cosf, sinf, noh)
    return {nm: r for (nm, _, _), r in zip(outs, res)}


def _cmp_kernel(xk_ref, xv_ref, pk_ref, pv_ref, w1k_ref, w1v_ref, w2k_ref, w2v_ref, kc_ref, vct_ref):
    nc = xk_ref.shape[0] // CMP_STRIDE

    def hidden(x_ref, p_ref, w1_ref):
        a = None
        b = None
        for l in range(CMP_STRIDE):
            x = x_ref[pl.ds(l, nc, stride=CMP_STRIDE), :]
            l2 = CMP_STRIDE + l
            da = jnp.dot((x + p_ref[l:l + 1, :]).astype(BF16), w1_ref[l], preferred_element_type=F32)
            db = jnp.dot((x + p_ref[l2:l2 + 1, :]).astype(BF16), w1_ref[l2], preferred_element_type=F32)
            a = da if a is None else a + da
            b = db if b is None else b + db
        hid = a + pltpu.roll(b, nc - 1, 0)
        return (hid * _sigmoid(hid)).astype(BF16)

    kc_ref[0] = jnp.dot(hidden(xk_ref, pk_ref, w1k_ref), w2k_ref[...], preferred_element_type=F32).astype(BF16)
    v_t = jnp.dot(hidden(xv_ref, pv_ref, w1v_ref), w2v_ref[...], preferred_element_type=F32).T
    for g in range(NSA_GROUPS):
        vct_ref[0, g] = v_t[g * HEAD_DIM:(g + 1) * HEAD_DIM].astype(BF16)


def _cmp_call(kcr, vcr, weights, B, S):
    nc = S // CMP_STRIDE
    full = lambda a: pl.BlockSpec(a.shape, lambda b: (0,) * a.ndim)
    kw = NSA_GROUPS * LANES
    return pl.pallas_call(
        _cmp_kernel,
        grid=(B,),
        in_specs=[pl.BlockSpec((S, LANES), lambda b: (b, 0)), pl.BlockSpec((S, LANES), lambda b: (b, 0))]
        + [full(w) for w in weights],
        out_specs=[pl.BlockSpec((1, nc, kw), lambda b: (b, 0, 0)),
                   pl.BlockSpec((1, NSA_GROUPS, HEAD_DIM, nc), lambda b: (b, 0, 0, 0))],
        out_shape=[jax.ShapeDtypeStruct((B, nc, kw), BF16),
                   jax.ShapeDtypeStruct((B, NSA_GROUPS, HEAD_DIM, nc), BF16)],
        compiler_params=pltpu.CompilerParams(dimension_semantics=("parallel",), vmem_limit_bytes=VMEM_LIMIT),
        name="compress",
    )(kcr, vcr, *weights)


def _nsa_kernel(qn_ref, qr_ref, gate_ref, kc_ref, vct_ref, ksa_ref, vst_ref, kw_ref, vwt_ref, ovt_ref, o_ref,
                qa_sc, acc_sc, m_sc, out_sc, s_sc, smax_sc, *, qt, nb, nc):
    i = pl.program_id(1)
    s0 = i * qt
    hd = HEAD_DIM
    heads = range(NSA_HEADS)
    grp_of = lambda h: h // HPG

    def gate_row(branch, h):
        return gate_ref[0, h * 3 + branch:h * 3 + branch + 1, :]

    def lanes_of(g):
        return slice(g * LANES, (g + 1) * LANES)

    r_i = lax.broadcasted_iota(jnp.int32, (qt, qt), 0)
    c_i = lax.broadcasted_iota(jnp.int32, (qt, qt), 1)

    def reset():
        acc_sc[...] = jnp.zeros(acc_sc.shape, F32)
        m_sc[...] = jnp.full(m_sc.shape, -jnp.inf, F32)

    def scores(k, q_of, h):
        s = jnp.dot(k, q_of(h), preferred_element_type=F32)
        return s, jnp.max(s, axis=0, keepdims=True)

    def fold(h, s_and_max, vt, mask, bias=None):
        s, s_max = s_and_max
        if mask is not None:
            s = jnp.where(mask, s, -MASK_BIG)
            s_max = jnp.max(s, axis=0, keepdims=True)
        if bias is not None:
            s_max = s_max + bias
        m_old = m_sc[h]
        m_new = jnp.maximum(m_old, s_max)
        alpha = jnp.exp2(m_old - m_new)
        pe = jnp.exp2(s - (m_new if bias is None else m_new - bias)).astype(BF16)
        acc_sc[h] = alpha * acc_sc[h] + jnp.dot(vt, pe, preferred_element_type=F32)
        m_sc[h] = m_new

    def tile(pending, vt, q_of, mask, k_next, bias=None):
        nxt = []
        for h in heads:
            if k_next is not None:
                nxt.append(scores(k_next[grp_of(h)], q_of, h))
            fold(h, pending[h], vt[grp_of(h)], mask, bias)
        return tuple(nxt)

    def first_scores(k, q_of):
        return tuple(scores(k[grp_of(h)], q_of, h) for h in heads)

    def finish(branch):
        for h in heads:
            scale = gate_row(branch, h) * (1.0 / acc_sc[h, hd:hd + 1, :])
            out_sc[h * hd:(h + 1) * hd, :] += acc_sc[h, :hd, :] * scale

    def rows(j):
        return pl.ds(pl.multiple_of(j * qt, qt), qt)

    for h in heads:
        qa_sc[h, :hd, :] = qr_ref[0, h]
    topk = min(SLC_TOPK, nb)
    quarters = 4
    assert nc % (quarters * BF16_SUBLANES) == 0 and nb % (quarters * 8) == 0

    def select_blocks(nc_e, nb_e):
        jj = lax.broadcasted_iota(jnp.int32, (nc_e, qt), 0)
        tt = s0 + lax.broadcasted_iota(jnp.int32, (nc_e, qt), 1)
        cmask = (CMP_STRIDE * jj + (CMP_BLOCK - 1)) <= tt
        kc = [kc_ref[0, :nc_e, g * LANES:g * LANES + hd] for g in range(NSA_GROUPS)]
        has_key = (s0 + lax.broadcasted_iota(jnp.int32, (1, qt), 1)) >= (CMP_BLOCK - 1)
        psum = [None] * NSA_GROUPS
        s_all = [jnp.dot(kc[grp_of(h)], qn_ref[0, h], preferred_element_type=F32) for h in heads]
        for h in heads:
            g = grp_of(h)
            s = jnp.where(cmask, s_all[h], -MASK_BIG)
            e = jnp.exp2(s - jnp.max(s, axis=0, keepdims=True))
            inv = jnp.where(has_key, 1.0 / jnp.sum(e, axis=0, keepdims=True), 0.0)
            p = e * inv
            o = jnp.dot(vct_ref[0, g, :, :nc_e], p.astype(BF16), preferred_element_type=F32)
            out_sc[h * hd:(h + 1) * hd, :] = gate_row(0, h) * o
            psum[g] = p if psum[g] is None else psum[g] + p

        n_idx = lax.broadcasted_iota(jnp.int32, (nb_e, qt), 0)
        n_f = n_idx.astype(F32)
        cur = (s0 + lax.broadcasted_iota(jnp.int32, (nb_e, qt), 1)) // SLC_BLOCK
        forced = (n_idx == 0) | (n_idx == cur) | (n_idx == cur - 1)
        valid = n_idx <= cur
        free = valid & jnp.logical_not(forced)
        ovt = ovt_ref[:nb_e, :nc_e]
        for g in range(NSA_GROUPS):
            if nb_e <= topk:
                sel = valid
            else:
                hi = psum[g].astype(BF16)
                r1 = psum[g] - hi.astype(F32)
                mid = r1.astype(BF16)
                lo = (r1 - mid.astype(F32)).astype(BF16)
                imp_t = (jnp.dot(ovt, hi, preferred_element_type=F32) + jnp.dot(ovt, mid, preferred_element_type=F32)
                         + jnp.dot(ovt, lo, preferred_element_type=F32))
                val = jnp.where(free, imp_t, -jnp.inf)
                picked = jnp.zeros((nb_e, qt), F32)
                for _ in range(max(topk - 3, 0)):
                    top = jnp.max(val, axis=0, keepdims=True)
                    first = jnp.min(jnp.where(val == top, n_f, float(nb_e)), axis=0, keepdims=True)
                    pick = n_f == first
                    picked = jnp.where(pick, 1.0, picked)
                    val = jnp.where(pick, -jnp.inf, val)
                sel = (forced | (picked > 0.0)) & valid
            notsel = jnp.where(sel, 0.0, 1.0).astype(BF16)
            if nb_e < LANES - hd:
                notsel = jnp.concatenate([notsel, jnp.zeros((LANES - hd - nb_e, qt), BF16)], axis=0)
            for h in range(g * HPG, (g + 1) * HPG):
                qa_sc[h, hd:, :] = notsel

    quarter = (i * quarters) // (nc * CMP_STRIDE // qt)
    for qq in range(quarters):
        pl.when(quarter == qq)(functools.partial(select_blocks, nc * (qq + 1) // quarters, nb * (qq + 1) // quarters))


    reset()
    qr_of = lambda h: qr_ref[0, h]
    mask_a = (r_i - c_i) > jnp.where(i >= 2, 0, qt)
    bias_b = jnp.where(i >= 1, 0.0, -MASK_BIG)
    ja = jnp.maximum(i - 2, 0)
    jb = jnp.maximum(i - 1, 0)
    k_win = lambda j: [kw_ref[rows(j), g * LANES:g * LANES + hd] for g in range(NSA_GROUPS)]
    v_win = lambda j: [vwt_ref[g, j] for g in range(NSA_GROUPS)]
    kb, ka = k_win(jb), k_win(ja)
    carry = first_scores(k_win(i), qr_of)
    carry = tile(carry, v_win(i), qr_of, r_i <= c_i, kb)
    carry = tile(carry, v_win(jb), qr_of, None, ka, bias=bias_b)
    tile(carry, v_win(ja), qr_of, mask_a, None)
    finish(2)

    reset()
    qa_of = lambda h: qa_sc[h]
    k_slc = lambda j: [ksa_ref[rows(j), lanes_of(g)] for g in range(NSA_GROUPS)]
    v_slc = lambda j: [vst_ref[g, j] for g in range(NSA_GROUPS)]

    def slc_scores(j, dst):
        ks = k_slc(j)
        for h in heads:
            s, s_max = scores(ks[grp_of(h)], qa_of, h)
            s_sc[dst, h * qt:(h + 1) * qt, :] = s
            smax_sc[dst, h] = s_max

    def slc_tile(j, src, dst, mask=None):
        ks = k_slc(j + 1) if dst is not None else None
        vt = v_slc(j)
        for h in heads:
            if dst is not None:
                s, s_max = scores(ks[grp_of(h)], qa_of, h)
                s_sc[dst, h * qt:(h + 1) * qt, :] = s
                smax_sc[dst, h] = s_max
            fold(h, (s_sc[src, h * qt:(h + 1) * qt, :], smax_sc[src, h]), vt[grp_of(h)], mask)

    slc_scores(0, 0)
    n_pairs = i // 2

    def slc_path(first_slot, peel):
        a, b = first_slot, 1 - first_slot

        def pair(p, carry):
            j = 2 * p + peel
            slc_tile(j, a, b)
            slc_tile(j + 1, b, a)
            return carry

        lax.fori_loop(0, n_pairs, pair, 0)
        slc_tile(i, a, None, r_i <= c_i)

    @pl.when(i % 2 == 1)
    def _():
        slc_tile(0, 0, 1)
        slc_path(1, 1)

    @pl.when(i % 2 == 0)
    def _():
        slc_path(0, 0)

    finish(1)

    o_ref[...] = out_sc[...].T.astype(o_ref.dtype)


def _nsa_call(p, kc, vct, ovt, B, S, qt):
    assert WINDOW == 2 * qt
    nq = S // qt
    nb = S // SLC_BLOCK
    nc = S // CMP_STRIDE
    qspec = pl.BlockSpec((1, NSA_HEADS, HEAD_DIM, qt), lambda b, i: (b * nq + i, 0, 0, 0))
    kspec = pl.BlockSpec((S, NSA_GROUPS * LANES), lambda b, i: (b, 0))
    vspec = pl.BlockSpec((NSA_GROUPS, nq, V_ROWS, qt), lambda b, i: (0, b, 0, 0))
    kernel = functools.partial(_nsa_kernel, qt=qt, nb=nb, nc=nc)
    return pl.pallas_call(
        kernel,
        grid=(B, nq),
        in_specs=[qspec, qspec,
                  pl.BlockSpec((1, GATE_ROWS, qt), lambda b, i: (b * nq + i, 0, 0)),
                  pl.BlockSpec((1, nc, NSA_GROUPS * LANES), lambda b, i: (b, 0, 0)),
                  pl.BlockSpec((1, NSA_GROUPS, HEAD_DIM, nc), lambda b, i: (b, 0, 0, 0)),
                  kspec, vspec, kspec, vspec,
                  pl.BlockSpec(ovt.shape, lambda b, i: (0, 0))],
        out_specs=pl.BlockSpec((qt, NSA_HEADS * HEAD_DIM), lambda b, i: (b * nq + i, 0)),
        out_shape=jax.ShapeDtypeStruct((B * S, NSA_HEADS * HEAD_DIM), BF16),
        scratch_shapes=[pltpu.VMEM((NSA_HEADS, LANES, qt), BF16),
                        pltpu.VMEM((NSA_HEADS, V_ROWS, qt), F32),
                        pltpu.VMEM((NSA_HEADS, 1, qt), F32),
                        pltpu.VMEM((NSA_HEADS * HEAD_DIM, qt), F32),
                        pltpu.VMEM((2, NSA_HEADS * qt + SLOT_SKEW_ROWS, qt), F32),
                        pltpu.VMEM((2, NSA_HEADS, 1, qt), F32)],
        compiler_params=pltpu.CompilerParams(dimension_semantics=("parallel", "arbitrary"),
                                             vmem_limit_bytes=VMEM_LIMIT),
        name="nsa",
    )(p["qn"], p["qr"], p["gate"], kc, vct, p["ksa"], p["vst"], p["kwr"], p["vwt"], ovt)


def _gla_kernel(gq_ref, gk_ref, gv_ref, ga_ref, gr_ref, wa_ref, ba_ref, gn_ref, o_ref, st_ref, *, tc):
    @pl.when(pl.program_id(1) == 0)
    def _():
        st_ref[...] = jnp.zeros(st_ref.shape, F32)

    def split2(v):
        hi = v.astype(BF16)
        return hi, (v - hi.astype(F32)).astype(BF16)

    g_hi, g_lo = split2(ga_ref[...])
    w_hi, w_lo = split2(wa_ref[...])
    a = (jnp.dot(g_hi, w_hi, preferred_element_type=F32) + jnp.dot(g_hi, w_lo, preferred_element_type=F32)
         + jnp.dot(g_lo, w_hi, preferred_element_type=F32)) + ba_ref[...]
    log_a = (jnp.minimum(a, 0.0) - jnp.log1p(jnp.exp(-jnp.abs(a)))) * (1.0 / GLA_TAU)
    c_sz = GLA_CHUNK
    n_chunk = tc // c_sz
    chunks = range(n_chunk)
    heads = range(GLA_HEADS)
    r_t = lax.broadcasted_iota(jnp.int32, (MXU_COLS, MXU_COLS), 0)
    c_t = lax.broadcasted_iota(jnp.int32, (MXU_COLS, MXU_COLS), 1)
    tril = jnp.where((r_t >= c_t) & ((r_t // c_sz) == (c_t // c_sz)), 1.0, 0.0).astype(BF16)
    hi = log_a.astype(BF16)
    r1 = log_a - hi.astype(F32)
    mid = r1.astype(BF16)
    lo = (r1 - mid.astype(F32)).astype(BF16)
    bcum = jnp.concatenate(
        [jnp.dot(tril, hi[r0:r0 + MXU_COLS], preferred_element_type=F32)
         + jnp.dot(tril, mid[r0:r0 + MXU_COLS], preferred_element_type=F32)
         + jnp.dot(tril, lo[r0:r0 + MXU_COLS], preferred_element_type=F32)
         for r0 in range(0, tc, MXU_COLS)], axis=0)
    gk = gk_ref[...].astype(F32)
    qg = (gq_ref[...].astype(F32) * GLA_SCALE) * jnp.exp(bcum)
    kg = (gk * jnp.exp(-bcum)).astype(BF16)
    lane = lax.broadcasted_iota(jnp.int32, (1, LANES), 1)
    causal = (lax.broadcasted_iota(jnp.int32, (c_sz, c_sz), 0)
              >= lax.broadcasted_iota(jnp.int32, (c_sz, c_sz), 1))
    rows = lambda c: slice(c * c_sz, (c + 1) * c_sz)
    klanes = lambda h: slice((h // 2) * LANES, (h // 2 + 1) * LANES)
    vlanes = lambda h: slice(h * GLA_DV, (h + 1) * GLA_DV)
    kd, dec = [], []
    for c in chunks:
        bl = bcum[(c + 1) * c_sz - 1:(c + 1) * c_sz]
        kd.append((gk[rows(c)] * jnp.exp(bl - bcum[rows(c)])).astype(BF16))
        dec.append(jnp.exp(bl))
    qg_h, att, upd = {}, {}, {}
    pair = 2
    for c in chunks:
        for h0 in range(0, GLA_HEADS, pair):
            for h in range(h0, h0 + pair):
                own = (lane // GLA_DK) == (h % pair)
                qg_h[c, h] = jnp.where(own, qg[rows(c), klanes(h)], 0.0).astype(BF16)
            both = lax.dot_general(jnp.concatenate([qg_h[c, h0 + e] for e in range(pair)], axis=0),
                                   kg[rows(c), klanes(h0)], NT_DIMS, preferred_element_type=F32)
            for e in range(pair):
                att[c, h0 + e] = both[e * c_sz:(e + 1) * c_sz]
    for c in chunks:
        for h0 in range(0, GLA_HEADS, pair):
            both = lax.dot_general(gv_ref[rows(c), h0 * GLA_DV:(h0 + pair) * GLA_DV], kd[c][:, klanes(h0)], TN_DIMS,
                                   preferred_element_type=F32)
            for e in range(pair):
                upd[c, h0 + e] = both[e * GLA_DV:(e + 1) * GLA_DV]
    st_before = {}
    for h in heads:
        st = st_ref[h]
        for c in chunks:
            st_before[c, h] = st.astype(BF16)
            st = st * dec[c][:, klanes(h)] + upd[c, h]
        st_ref[h] = st
    gn = gn_ref[...]
    for c in chunks:
        for h in heads:
            a_c = jnp.where(causal, att[c, h], 0.0).astype(BF16)
            o = (jnp.dot(a_c, gv_ref[rows(c), vlanes(h)], preferred_element_type=F32)
                 + lax.dot_general(qg_h[c, h], st_before[c, h], NT_DIMS, preferred_element_type=F32))
            y = o * lax.rsqrt(jnp.mean(o * o, axis=-1, keepdims=True) + NORM_EPS) * gn
            gr = gr_ref[rows(c), vlanes(h)].astype(F32)
            o_ref[rows(c), vlanes(h)] = (y * (gr * _sigmoid(gr))).astype(BF16)


def _gla_call(p, wa, ba, gn, B, S, tc):
    nt = S // tc
    row = lambda w: pl.BlockSpec((tc, w), lambda b, t: (b * nt + t, 0))
    full = lambda a: pl.BlockSpec(a.shape, lambda b, t: (0,) * a.ndim)
    kwidth = GLA_HEADS * GLA_DK
    vwidth = GLA_HEADS * GLA_DV
    return pl.pallas_call(
        functools.partial(_gla_kernel, tc=tc),
        grid=(B, nt),
        in_specs=[row(kwidth), row(kwidth), row(vwidth), row(LANES), row(vwidth), full(wa), full(ba), full(gn)],
        out_specs=row(vwidth),
        out_shape=jax.ShapeDtypeStruct((B * S, vwidth), BF16),
        scratch_shapes=[pltpu.VMEM((GLA_HEADS, GLA_DV, LANES), F32)],
        compiler_params=pltpu.CompilerParams(dimension_semantics=("parallel", "arbitrary"),
                                             vmem_limit_bytes=VMEM_LIMIT),
        name="gla",
    )(p["gq"], p["gk"], p["gv"], p["ga"], p["gr"], wa, ba, gn)


def _out_kernel(on_ref, og_ref, x_ref, gpre_ref, wz_ref, wn_ref, wg_ref, wo_ref, gp_ref, o_ref):
    d = x_ref.shape[1]
    nz_w = _MSRC["nz"][1]
    blocks = [slice(r0, r0 + TILE) for r0 in range(0, x_ref.shape[0], TILE)]
    ups = []
    for rs in blocks:
        x = x_ref[rs, :]
        h = ((x * lax.rsqrt(jnp.mean(x * x, axis=-1, keepdims=True) + NORM_EPS)) * gpre_ref[...]).astype(BF16)
        nz = jnp.dot(h, wz_ref[:, :nz_w], preferred_element_type=F32)
        mg = [jnp.dot(h, wz_ref[:, nz_w + c * d:nz_w + (c + 1) * d], preferred_element_type=F32) for c in range(2)]
        gated = on_ref[rs, :].astype(F32) * (nz * _sigmoid(nz))
        ups.append((jnp.dot(gated.astype(BF16), wn_ref[...], preferred_element_type=F32),
                    jnp.dot(og_ref[rs, :], wg_ref[...], preferred_element_type=F32), mg))
    for rs, (a, b, mg) in zip(blocks, ups):
        y = _sigmoid(mg[0]) * a + _sigmoid(mg[1]) * b
        out = jnp.dot(y.astype(BF16), wo_ref[...], preferred_element_type=F32)
        r = out * lax.rsqrt(jnp.mean(out * out, axis=-1, keepdims=True) + NORM_EPS)
        o_ref[rs, :] = x_ref[rs, :] + r * gp_ref[...]


def _out_call(o_nsa, o_gla, xf, g_pre, wz, wn, wg, wo, gp, tm):
    n, d = xf.shape
    row = lambda w: pl.BlockSpec((tm, w), lambda i: (i, 0))
    full = lambda a: pl.BlockSpec(a.shape, lambda i: (0,) * a.ndim)
    return pl.pallas_call(
        _out_kernel,
        grid=(n // tm,),
        in_specs=[row(o_nsa.shape[1]), row(o_gla.shape[1]), row(d),
                  full(g_pre), full(wz), full(wn), full(wg), full(wo), full(gp)],
        out_specs=row(d),
        out_shape=jax.ShapeDtypeStruct((n, d), F32),
        compiler_params=pltpu.CompilerParams(dimension_semantics=("parallel",), vmem_limit_bytes=VMEM_LIMIT),
        name="merge_out",
    )(o_nsa, o_gla, xf, g_pre, wz, wn, wg, wo, gp)


def _position_tables(S):
    inv_freq = ROPE_THETA ** (-jnp.arange(0, HEAD_DIM, 2, dtype=F32) / HEAD_DIM)
    step = SLC_BLOCK
    a_hi = (step * jnp.arange(S // step, dtype=F32))[:, None] * inv_freq[None, :]
    a_lo = jnp.arange(step, dtype=F32)[:, None] * inv_freq[None, :]
    c_hi, s_hi, c_lo, s_lo = jnp.cos(a_hi)[:, None, :], jnp.sin(a_hi)[:, None, :], jnp.cos(a_lo)[None], jnp.sin(a_lo)[None]
    cos = (c_hi * c_lo - s_hi * s_lo).reshape(S, HEAD_DIM // 2)
    sin = (s_hi * c_lo + c_hi * s_lo).reshape(S, HEAD_DIM // 2)
    cosf = jnp.concatenate([cos] * 4, axis=1)
    sinf = jnp.concatenate([-sin, sin, -sin, sin], axis=1)
    blk = np.arange(S) // SLC_BLOCK
    noh = np.zeros((S, LANES), np.float32)
    noh[np.arange(S), HEAD_DIM + blk] = -MASK_BIG
    nb, nc = S // SLC_BLOCK, S // CMP_STRIDE
    cs = CMP_STRIDE * np.arange(nc)
    bs = SLC_BLOCK * np.arange(nb)
    ov = (cs[None, :] < bs[:, None] + SLC_BLOCK) & (cs[None, :] + CMP_BLOCK > bs[:, None])
    ov[:, (S - CMP_BLOCK) // CMP_STRIDE + 1:] = False
    return cosf, sinf, jnp.asarray(noh), jnp.asarray(ov.astype(np.float32)).astype(BF16)


def _compress_weights(pos_k, w1_k, w2_k, pos_v, w1_v, w2_v):
    assert NSA_GROUPS == 2

    def block_diag2(w):
        z = jnp.zeros_like(w)
        return jnp.concatenate([jnp.concatenate([w, z], axis=-1), jnp.concatenate([z, w], axis=-1)], axis=-2)

    pad_k = jnp.concatenate([w2_k, jnp.zeros((CMP_HIDDEN, LANES - HEAD_DIM), w2_k.dtype)], axis=1)
    return (jnp.concatenate([pos_k, pos_k], axis=1), jnp.concatenate([pos_v, pos_v], axis=1),
            block_diag2(w1_k).astype(BF16), block_diag2(w1_v).astype(BF16),
            block_diag2(pad_k).astype(BF16),
            block_diag2(w2_v).astype(BF16))


def _layer(xf, tabs, B, S, g_pre, w_in, cmp_pos_k, cmp_w1_k, cmp_w2_k, cmp_pos_v, cmp_w1_v, cmp_w2_v,
           gla_w_a, gla_b_a, gla_g_norm, w_up_nsa, w_up_gla, w_out, g_post):
    cosf, sinf, noh, ovt = tabs
    d = xf.shape[1]
    p = _proj_call(xf, g_pre.reshape(1, d), _pad_proj_weight(w_in), cosf, sinf, noh, S, tm=ROW_BLOCK)

    cmp_w = _compress_weights(cmp_pos_k, cmp_w1_k, cmp_w2_k, cmp_pos_v, cmp_w1_v, cmp_w2_v)
    kc, vct = _cmp_call(p["kcr"], p["vcr"], cmp_w, B, S)
    o_nsa = _nsa_call(p, kc, vct, ovt, B, S, qt=TILE)

    wa = jnp.zeros((LANES, gla_w_a.shape[1]), F32).at[:GLA_RANK].set(gla_w_a)
    o_gla = _gla_call(p, wa, gla_b_a.reshape(1, -1), gla_g_norm.reshape(1, GLA_DV), B, S, tc=ROW_BLOCK)

    return _out_call(o_nsa, o_gla, xf, g_pre.reshape(1, d), _merge_gate_weight(w_in), w_up_nsa.astype(BF16),
                     w_up_gla.astype(BF16), w_out.astype(BF16), g_post.reshape(1, d), tm=ROW_BLOCK)


def kernel(x, g_pre, w_in, cmp_pos_k, cmp_w1_k, cmp_w2_k, cmp_pos_v, cmp_w1_v, cmp_w2_v, gla_w_a, gla_b_a,
           gla_g_norm, w_up_nsa, w_up_gla, w_out, g_post):
    B, S, d = x.shape
    assert d == D_MODEL and S % 2048 == 0 and S // SLC_BLOCK <= LANES - HEAD_DIM
    tabs = _position_tables(S)
    xf = x.reshape(B * S, d)
    for l in range(g_pre.shape[0]):
        xf = _layer(xf, tabs, B, S, g_pre[l], w_in[l], cmp_pos_k[l], cmp_w1_k[l], cmp_w2_k[l],
                    cmp_pos_v[l], cmp_w1_v[l], cmp_w2_v[l], gla_w_a[l], gla_b_a[l], gla_g_norm[l],
                    w_up_nsa[l], w_up_gla[l], w_out[l], g_post[l])
    return xf.reshape(B, S, d)
```

```python
import functools

import numpy as np
import jax
import jax.numpy as jnp
from jax import lax
from jax.experimental import pallas as pl
from jax.experimental.pallas import tpu as pltpu

F32 = jnp.float32
BF16 = jnp.bfloat16

D_MODEL = 1024
NSA_HEADS = 8
NSA_GROUPS = 2
HPG = NSA_HEADS // NSA_GROUPS
HEAD_DIM = 64
CMP_BLOCK = 32
CMP_STRIDE = 16
CMP_HIDDEN = 128
SLC_BLOCK = 64
SLC_TOPK = 16
WINDOW = 512
GLA_HEADS = 4
GLA_DK = 64
GLA_DV = 128
GLA_RANK = 16
GLA_TAU = 16.0
GLA_CHUNK = 64
ROPE_THETA = 10000.0
NORM_EPS = 1e-6
QK_SCALE = HEAD_DIM ** -0.5
Q_PRESCALE = QK_SCALE * 1.4426950408889634
GLA_SCALE = GLA_DK ** -0.5

LANES = 128
MXU_COLS = 256
BF16_SUBLANES = 16
TILE = 256
ROW_BLOCK = 4 * TILE
GATE_ROWS = 32
V_ROWS = HEAD_DIM + BF16_SUBLANES
MASK_BIG = 1e30
VMEM_LIMIT = 56 * 1024 * 1024

NT_DIMS = (((1,), (1,)), ((), ()))
TN_DIMS = (((0,), (0,)), ((), ()))

_SPLITS = (("q", 512), ("kcvc", 256), ("ks", 128), ("vs", 128), ("kw", 128), ("vw", 128),
           ("ng", 24), ("nz", 512), ("gq", 256), ("gk", 256), ("gv", 512), ("ga", 16), ("gr", 512),
           ("mg", 2048))


_MERGE_SEGS = ("nz", "mg")


def _proj_layout():
    src, pofs, msrc, o_src, o_dst = [], {}, {}, 0, 0
    for name, width in _SPLITS:
        if name in _MERGE_SEGS:
            msrc[name] = (o_src, width)
        else:
            padded = -(-width // LANES) * LANES
            src.append((o_src, width, padded - width))
            pofs[name] = (o_dst, padded)
            o_dst += padded
        o_src += width
    return src, pofs, o_dst, msrc


_PSRC, _POFS, NP_COLS, _MSRC = _proj_layout()


def _pad_proj_weight(w):
    parts = []
    for start, width, pad in _PSRC:
        parts.append(w[:, start:start + width])
        if pad:
            parts.append(jnp.zeros((w.shape[0], pad), w.dtype))
    return jnp.concatenate(parts, axis=1).astype(BF16)


def _merge_gate_weight(w):
    return jnp.concatenate([w[:, s:s + n] for s, n in (_MSRC[name] for name in _MERGE_SEGS)], axis=1).astype(BF16)


def _sigmoid(v):
    return 1.0 / (1.0 + jnp.exp(-v))


def _proj_kernel(x_ref, g_ref, w_ref, cos_ref, sin_ref, noh_ref,
                 qn_ref, qr_ref, kcr_ref, vcr_ref, ksa_ref, kwr_ref, vst_ref, vwt_ref, gate_ref,
                 gq_ref, gk_ref, gv_ref, ga_ref, gr_ref):
    for blk in range(x_ref.shape[0] // TILE):
        _proj_block(blk, slice(blk * TILE, (blk + 1) * TILE), x_ref, g_ref, w_ref, cos_ref, sin_ref, noh_ref,
                    qn_ref, qr_ref, kcr_ref, vcr_ref, ksa_ref, kwr_ref, vst_ref, vwt_ref, gate_ref,
                    gq_ref, gk_ref, gv_ref, ga_ref, gr_ref)


def _proj_block(blk, rs, x_ref, g_ref, w_ref, cos_ref, sin_ref, noh_ref,
                qn_ref, qr_ref, kcr_ref, vcr_ref, ksa_ref, kwr_ref, vst_ref, vwt_ref, gate_ref,
                gq_ref, gk_ref, gv_ref, ga_ref, gr_ref):
    x = x_ref[rs, :]
    y = x * lax.rsqrt(jnp.mean(x * x, axis=-1, keepdims=True) + NORM_EPS)
    h = (y * g_ref[...]).astype(BF16)
    cos = cos_ref[rs, :]
    sin = sin_ref[rs, :]
    tm = TILE
    lane = lax.broadcasted_iota(jnp.int32, (tm, LANES), 1)
    lo_half = (lane & (HEAD_DIM - 1)) < (HEAD_DIM // 2)
    first_head = lane < HEAD_DIM

    def rope(v):
        rot = jnp.where(lo_half, pltpu.roll(v, LANES - HEAD_DIM // 2, 1), pltpu.roll(v, HEAD_DIM // 2, 1))
        return v * cos + rot * sin

    windows = {}

    def chunks(name):
        start, width = _POFS[name]
        for c in range(width // LANES):
            col = start + c * LANES
            w = col // MXU_COLS
            if w not in windows:
                windows[w] = jnp.dot(h, w_ref[:, w * MXU_COLS:(w + 1) * MXU_COLS], preferred_element_type=F32)
            off = col % MXU_COLS
            yield c, windows[w][:, off:off + LANES]

    for pr, v in chunks("q"):
        n_t = (v * Q_PRESCALE).T.astype(BF16)
        r_t = (rope(v) * Q_PRESCALE).T.astype(BF16)
        for e in range(2):
            qn_ref[blk, 2 * pr + e] = n_t[e * HEAD_DIM:(e + 1) * HEAD_DIM]
            qr_ref[blk, 2 * pr + e] = r_t[e * HEAD_DIM:(e + 1) * HEAD_DIM]
    for (_, v), ref in zip(chunks("kcvc"), (kcr_ref, vcr_ref)):
        ref[rs, :] = v
    noh = noh_ref[rs, :]
    for name, ref, upper in (("ks", ksa_ref, noh), ("kw", kwr_ref, jnp.zeros_like(noh))):
        for _, v in chunks(name):
            r = rope(v)
            ref[rs, :LANES] = jnp.where(first_head, r, upper).astype(BF16)
            ref[rs, LANES:] = jnp.where(first_head, pltpu.roll(r, HEAD_DIM, 1), upper).astype(BF16)
    row16 = lax.broadcasted_iota(jnp.int32, (BF16_SUBLANES, tm), 0)
    ones_rows = jnp.where(row16 == 0, 1.0, 0.0)
    for name, ref in (("vs", vst_ref), ("vw", vwt_ref)):
        for _, v in chunks(name):
            v_t = v.T
            for g in range(NSA_GROUPS):
                ref[g, blk] = jnp.concatenate([v_t[g * HEAD_DIM:(g + 1) * HEAD_DIM], ones_rows], axis=0).astype(BF16)
    for _, v in chunks("ng"):
        gate_ref[blk] = _sigmoid(v).T[:GATE_ROWS]
    for name, ref in (("gq", gq_ref), ("gk", gk_ref), ("gv", gv_ref), ("gr", gr_ref)):
        for c, v in chunks(name):
            ref[rs, c * LANES:(c + 1) * LANES] = v.astype(BF16)
    for _, v in chunks("ga"):
        ga_ref[rs, :] = v


def _proj_call(xf, g_pre, w_p, cosf, sinf, noh, S, tm):
    n, d = xf.shape
    nt = n // tm
    spt = S // tm
    row = lambda width: pl.BlockSpec((tm, width), lambda i: (i, 0))
    pos = pl.BlockSpec((tm, LANES), lambda i: (i % spt, 0))
    nb_ = tm // TILE
    ntile = n // TILE
    qt_shape = ((ntile, NSA_HEADS, HEAD_DIM, TILE), (nb_, NSA_HEADS, HEAD_DIM, TILE), lambda i: (i, 0, 0, 0))
    vt_shape = ((NSA_GROUPS, ntile, V_ROWS, TILE), (NSA_GROUPS, nb_, V_ROWS, TILE), lambda i: (0, i, 0, 0))
    gt_shape = ((ntile, GATE_ROWS, TILE), (nb_, GATE_ROWS, TILE), lambda i: (i, 0, 0))
    outs = [("qn", qt_shape, BF16), ("qr", qt_shape, BF16), ("kcr", 128, F32), ("vcr", 128, F32),
            ("ksa", 256, BF16),
            ("kwr", 256, BF16), ("vst", vt_shape, BF16), ("vwt", vt_shape, BF16), ("gate", gt_shape, F32),
            ("gq", 256, BF16), ("gk", 256, BF16), ("gv", 512, BF16), ("ga", 128, F32), ("gr", 512, BF16)]
    out_specs, out_shape = [], []
    for _, sh, dt in outs:
        if isinstance(sh, int):
            out_specs.append(row(sh))
            out_shape.append(jax.ShapeDtypeStruct((n, sh), dt))
        else:
            out_specs.append(pl.BlockSpec(sh[1], sh[2]))
            out_shape.append(jax.ShapeDtypeStruct(sh[0], dt))
    res = pl.pallas_call(
        _proj_kernel,
        grid=(nt,),
        in_specs=[row(d),
                  pl.BlockSpec((1, d), lambda i: (0, 0)),
                  pl.BlockSpec((d, NP_COLS), lambda i: (0, 0), pipeline_mode=pl.Buffered(1)),
                  pos, pos, pos],
        out_specs=out_specs,
        out_shape=out_shape,
        compiler_params=pltpu.CompilerParams(dimension_semantics=("parallel",),
                                             vmem_limit_bytes=VMEM_LIMIT),
        name="proj",
    )(xf, g_pre, w_p, cosf, sinf, noh)
    return {nm: r for (nm, _, _), r in zip(outs, res)}


def _cmp_kernel(xk_ref, xv_ref, pk_ref, pv_ref, w1k_ref, w1v_ref, w2k_ref, w2v_ref, kc_ref, vct_ref):
    nc = xk_ref.shape[0] // CMP_STRIDE

    def hidden(x_ref, p_ref, w1_ref):
        a = None
        b = None
        for l in range(CMP_STRIDE):
            x = x_ref[pl.ds(l, nc, stride=CMP_STRIDE), :]
            l2 = CMP_STRIDE + l
            da = jnp.dot((x + p_ref[l:l + 1, :]).astype(BF16), w1_ref[l], preferred_element_type=F32)
            db = jnp.dot((x + p_ref[l2:l2 + 1, :]).astype(BF16), w1_ref[l2], preferred_element_type=F32)
            a = da if a is None else a + da
            b = db if b is None else b + db
        hid = a + pltpu.roll(b, nc - 1, 0)
        return (hid * _sigmoid(hid)).astype(BF16)

    kc_ref[0] = jnp.dot(hidden(xk_ref, pk_ref, w1k_ref), w2k_ref[...], preferred_element_type=F32).astype(BF16)
    v_t = jnp.dot(hidden(xv_ref, pv_ref, w1v_ref), w2v_ref[...], preferred_element_type=F32).T
    for g in range(NSA_GROUPS):
        vct_ref[0, g] = v_t[g * HEAD_DIM:(g + 1) * HEAD_DIM].astype(BF16)


def _cmp_call(kcr, vcr, weights, B, S):
    nc = S // CMP_STRIDE
    full = lambda a: pl.BlockSpec(a.shape, lambda b: (0,) * a.ndim)
    kw = NSA_GROUPS * LANES
    return pl.pallas_call(
        _cmp_kernel,
        grid=(B,),
        in_specs=[pl.BlockSpec((S, LANES), lambda b: (b, 0)), pl.BlockSpec((S, LANES), lambda b: (b, 0))]
        + [full(w) for w in weights],
        out_specs=[pl.BlockSpec((1, nc, kw), lambda b: (b, 0, 0)),
                   pl.BlockSpec((1, NSA_GROUPS, HEAD_DIM, nc), lambda b: (b, 0, 0, 0))],
        out_shape=[jax.ShapeDtypeStruct((B, nc, kw), BF16),
                   jax.ShapeDtypeStruct((B, NSA_GROUPS, HEAD_DIM, nc), BF16)],
        compiler_params=pltpu.CompilerParams(dimension_semantics=("parallel",), vmem_limit_bytes=VMEM_LIMIT),
        name="compress",
    )(kcr, vcr, *weights)


def _nsa_kernel(qn_ref, qr_ref, gate_ref, kc_ref, vct_ref, ksa_ref, vst_ref, kw_ref, vwt_ref, ovt_ref, o_ref,
                qa_sc, acc_sc, m_sc, out_sc, s_sc, smax_sc, *, qt, nb, nc):
    i = pl.program_id(1)
    s0 = i * qt
    hd = HEAD_DIM
    heads = range(NSA_HEADS)
    grp_of = lambda h: h // HPG

    def gate_row(branch, h):
        return gate_ref[0, h * 3 + branch:h * 3 + branch + 1, :]

    def lanes_of(g):
        return slice(g * LANES, (g + 1) * LANES)

    r_i = lax.broadcasted_iota(jnp.int32, (qt, qt), 0)
    c_i = lax.broadcasted_iota(jnp.int32, (qt, qt), 1)

    def reset():
        acc_sc[...] = jnp.zeros(acc_sc.shape, F32)
        m_sc[...] = jnp.full(m_sc.shape, -jnp.inf, F32)

    def scores(k, q_of, h):
        s = jnp.dot(k, q_of(h), preferred_element_type=F32)
        return s, jnp.max(s, axis=0, keepdims=True)

    def fold(h, s_and_max, vt, mask, bias=None):
        s, s_max = s_and_max
        if mask is not None:
            s = jnp.where(mask, s, -MASK_BIG)
            s_max = jnp.max(s, axis=0, keepdims=True)
        if bias is not None:
            s_max = s_max + bias
        m_old = m_sc[h]
        m_new = jnp.maximum(m_old, s_max)
        alpha = jnp.exp2(m_old - m_new)
        pe = jnp.exp2(s - (m_new if bias is None else m_new - bias)).astype(BF16)
        acc_sc[h] = alpha * acc_sc[h] + jnp.dot(vt, pe, preferred_element_type=F32)
        m_sc[h] = m_new

    def tile(pending, vt, q_of, mask, k_next, bias=None):
        nxt = []
        for h in heads:
            if k_next is not None:
                nxt.append(scores(k_next[grp_of(h)], q_of, h))
            fold(h, pending[h], vt[grp_of(h)], mask, bias)
        return tuple(nxt)

    def first_scores(k, q_of):
        return tuple(scores(k[grp_of(h)], q_of, h) for h in heads)

    def finish(branch):
        for h in heads:
            scale = gate_row(branch, h) * (1.0 / acc_sc[h, hd:hd + 1, :])
            out_sc[h * hd:(h + 1) * hd, :] += acc_sc[h, :hd, :] * scale

    def rows(j):
        return pl.ds(pl.multiple_of(j * qt, qt), qt)

    for h in heads:
        qa_sc[h, :hd, :] = qr_ref[0, h]
    topk = min(SLC_TOPK, nb)
    quarters = 4
    assert nc % (quarters * BF16_SUBLANES) == 0 and nb % (quarters * 8) == 0

    def select_blocks(nc_e, nb_e):
        jj = lax.broadcasted_iota(jnp.int32, (nc_e, qt), 0)
        tt = s0 + lax.broadcasted_iota(jnp.int32, (nc_e, qt), 1)
        cmask = (CMP_STRIDE * jj + (CMP_BLOCK - 1)) <= tt
        kc = [kc_ref[0, :nc_e, g * LANES:g * LANES + hd] for g in range(NSA_GROUPS)]
        has_key = (s0 + lax.broadcasted_iota(jnp.int32, (1, qt), 1)) >= (CMP_BLOCK - 1)
        psum = [None] * NSA_GROUPS
        s_all = [jnp.dot(kc[grp_of(h)], qn_ref[0, h], preferred_element_type=F32) for h in heads]
        for h in heads:
            g = grp_of(h)
            s = jnp.where(cmask, s_all[h], -MASK_BIG)
            e = jnp.exp2(s - jnp.max(s, axis=0, keepdims=True))
            inv = jnp.where(has_key, 1.0 / jnp.sum(e, axis=0, keepdims=True), 0.0)
            p = e * inv
            o = jnp.dot(vct_ref[0, g, :, :nc_e], p.astype(BF16), preferred_element_type=F32)
            out_sc[h * hd:(h + 1) * hd, :] = gate_row(0, h) * o
            psum[g] = p if psum[g] is None else psum[g] + p

        n_idx = lax.broadcasted_iota(jnp.int32, (nb_e, qt), 0)
        n_f = n_idx.astype(F32)
        cur = (s0 + lax.broadcasted_iota(jnp.int32, (nb_e, qt), 1)) // SLC_BLOCK
        forced = (n_idx == 0) | (n_idx == cur) | (n_idx == cur - 1)
        valid = n_idx <= cur
        free = valid & jnp.logical_not(forced)
        ovt = ovt_ref[:nb_e, :nc_e]
        for g in range(NSA_GROUPS):
            if nb_e <= topk:
                sel = valid
            else:
                hi = psum[g].astype(BF16)
                r1 = psum[g] - hi.astype(F32)
                mid = r1.astype(BF16)
                lo = (r1 - mid.astype(F32)).astype(BF16)
                imp_t = (jnp.dot(ovt, hi, preferred_element_type=F32) + jnp.dot(ovt, mid, preferred_element_type=F32)
                         + jnp.dot(ovt, lo, preferred_element_type=F32))
                val = jnp.where(free, imp_t, -jnp.inf)
                picked = jnp.zeros((nb_e, qt), F32)
                for _ in range(max(topk - 3, 0)):
                    top = jnp.max(val, axis=0, keepdims=True)
                    first = jnp.min(jnp.where(val == top, n_f, float(nb_e)), axis=0, keepdims=True)
                    pick = n_f == first
                    picked = jnp.where(pick, 1.0, picked)
                    val = jnp.where(pick, -jnp.inf, val)
                sel = (forced | (picked > 0.0)) & valid
            notsel = jnp.where(sel, 0.0, 1.0).astype(BF16)
            if nb_e < LANES - hd:
                notsel = jnp.concatenate([notsel, jnp.zeros((LANES - hd - nb_e, qt), BF16)], axis=0)
            for h in range(g * HPG, (g + 1) * HPG):
                qa_sc[h, hd:, :] = notsel

    quarter = (i * quarters) // (nc * CMP_STRIDE // qt)
    for qq in range(quarters):
        pl.when(quarter == qq)(functools.partial(select_blocks, nc * (qq + 1) // quarters, nb * (qq + 1) // quarters))


    reset()
    qr_of = lambda h: qr_ref[0, h]
    mask_a = (r_i - c_i) > jnp.where(i >= 2, 0, qt)
    bias_b = jnp.where(i >= 1, 0.0, -MASK_BIG)
    ja = jnp.maximum(i - 2, 0)
    jb = jnp.maximum(i - 1, 0)
    k_win = lambda j: [kw_ref[rows(j), g * LANES:g * LANES + hd] for g in range(NSA_GROUPS)]
    v_win = lambda j: [vwt_ref[g, j] for g in range(NSA_GROUPS)]
    kb, ka = k_win(jb), k_win(ja)
    carry = first_scores(k_win(i), qr_of)
    carry = tile(carry, v_win(i), qr_of, r_i <= c_i, kb)
    carry = tile(carry, v_win(jb), qr_of, None, ka, bias=bias_b)
    tile(carry, v_win(ja), qr_of, mask_a, None)
    finish(2)

    reset()
    qa_of = lambda h: qa_sc[h]
    k_slc = lambda j: [ksa_ref[rows(j), lanes_of(g)] for g in range(NSA_GROUPS)]
    v_slc = lambda j: [vst_ref[g, j] for g in range(NSA_GROUPS)]

    def slc_scores(j, dst):
        ks = k_slc(j)
        for h in heads:
            s, s_max = scores(ks[grp_of(h)], qa_of, h)
            s_sc[dst, h] = s
            smax_sc[dst, h] = s_max

    def slc_tile(j, src, dst, mask=None):
        ks = k_slc(j + 1) if dst is not None else None
        vt = v_slc(j)
        for h in heads:
            if dst is not None:
                s, s_max = scores(ks[grp_of(h)], qa_of, h)
                s_sc[dst, h] = s
                smax_sc[dst, h] = s_max
            fold(h, (s_sc[src, h], smax_sc[src, h]), vt[grp_of(h)], mask)

    slc_scores(0, 0)
    n_pairs = i // 2

    def slc_path(first_slot, peel):
        a, b = first_slot, 1 - first_slot

        def pair(p, carry):
            j = 2 * p + peel
            slc_tile(j, a, b)
            slc_tile(j + 1, b, a)
            return carry

        lax.fori_loop(0, n_pairs, pair, 0)
        slc_tile(i, a, None, r_i <= c_i)

    @pl.when(i % 2 == 1)
    def _():
        slc_tile(0, 0, 1)
        slc_path(1, 1)

    @pl.when(i % 2 == 0)
    def _():
        slc_path(0, 0)

    finish(1)

    o_ref[...] = out_sc[...].T.astype(o_ref.dtype)


def _nsa_call(p, kc, vct, ovt, B, S, qt):
    assert WINDOW == 2 * qt
    nq = S // qt
    nb = S // SLC_BLOCK
    nc = S // CMP_STRIDE
    qspec = pl.BlockSpec((1, NSA_HEADS, HEAD_DIM, qt), lambda b, i: (b * nq + i, 0, 0, 0))
    kspec = pl.BlockSpec((S, NSA_GROUPS * LANES), lambda b, i: (b, 0))
    vspec = pl.BlockSpec((NSA_GROUPS, nq, V_ROWS, qt), lambda b, i: (0, b, 0, 0))
    kernel = functools.partial(_nsa_kernel, qt=qt, nb=nb, nc=nc)
    return pl.pallas_call(
        kernel,
        grid=(B, nq),
        in_specs=[qspec, qspec,
                  pl.BlockSpec((1, GATE_ROWS, qt), lambda b, i: (b * nq + i, 0, 0)),
                  pl.BlockSpec((1, nc, NSA_GROUPS * LANES), lambda b, i: (b, 0, 0)),
                  pl.BlockSpec((1, NSA_GROUPS, HEAD_DIM, nc), lambda b, i: (b, 0, 0, 0)),
                  kspec, vspec, kspec, vspec,
                  pl.BlockSpec(ovt.shape, lambda b, i: (0, 0))],
        out_specs=pl.BlockSpec((qt, NSA_HEADS * HEAD_DIM), lambda b, i: (b * nq + i, 0)),
        out_shape=jax.ShapeDtypeStruct((B * S, NSA_HEADS * HEAD_DIM), BF16),
        scratch_shapes=[pltpu.VMEM((NSA_HEADS, LANES, qt), BF16),
                        pltpu.VMEM((NSA_HEADS, V_ROWS, qt), F32),
                        pltpu.VMEM((NSA_HEADS, 1, qt), F32),
                        pltpu.VMEM((NSA_HEADS * HEAD_DIM, qt), F32),
                        pltpu.VMEM((2, NSA_HEADS, qt, qt), F32),
                        pltpu.VMEM((2, NSA_HEADS, 1, qt), F32)],
        compiler_params=pltpu.CompilerParams(dimension_semantics=("parallel", "arbitrary"),
                                             vmem_limit_bytes=VMEM_LIMIT),
        name="nsa",
    )(p["qn"], p["qr"], p["gate"], kc, vct, p["ksa"], p["vst"], p["kwr"], p["vwt"], ovt)


def _gla_kernel(gq_ref, gk_ref, gv_ref, ga_ref, gr_ref, wa_ref, ba_ref, gn_ref, o_ref, st_ref, *, tc):
    @pl.when(pl.program_id(1) == 0)
    def _():
        st_ref[...] = jnp.zeros(st_ref.shape, F32)

    def split2(v):
        hi = v.astype(BF16)
        return hi, (v - hi.astype(F32)).astype(BF16)

    g_hi, g_lo = split2(ga_ref[...])
    w_hi, w_lo = split2(wa_ref[...])
    a = (jnp.dot(g_hi, w_hi, preferred_element_type=F32) + jnp.dot(g_hi, w_lo, preferred_element_type=F32)
         + jnp.dot(g_lo, w_hi, preferred_element_type=F32)) + ba_ref[...]
    log_a = (jnp.minimum(a, 0.0) - jnp.log1p(jnp.exp(-jnp.abs(a)))) * (1.0 / GLA_TAU)
    c_sz = GLA_CHUNK
    n_chunk = tc // c_sz
    chunks = range(n_chunk)
    heads = range(GLA_HEADS)
    r_t = lax.broadcasted_iota(jnp.int32, (MXU_COLS, MXU_COLS), 0)
    c_t = lax.broadcasted_iota(jnp.int32, (MXU_COLS, MXU_COLS), 1)
    tril = jnp.where((r_t >= c_t) & ((r_t // c_sz) == (c_t // c_sz)), 1.0, 0.0).astype(BF16)
    hi = log_a.astype(BF16)
    r1 = log_a - hi.astype(F32)
    mid = r1.astype(BF16)
    lo = (r1 - mid.astype(F32)).astype(BF16)
    bcum = jnp.concatenate(
        [jnp.dot(tril, hi[r0:r0 + MXU_COLS], preferred_element_type=F32)
         + jnp.dot(tril, mid[r0:r0 + MXU_COLS], preferred_element_type=F32)
         + jnp.dot(tril, lo[r0:r0 + MXU_COLS], preferred_element_type=F32)
         for r0 in range(0, tc, MXU_COLS)], axis=0)
    gk = gk_ref[...].astype(F32)
    qg = (gq_ref[...].astype(F32) * GLA_SCALE) * jnp.exp(bcum)
    kg = (gk * jnp.exp(-bcum)).astype(BF16)
    lane = lax.broadcasted_iota(jnp.int32, (1, LANES), 1)
    causal = (lax.broadcasted_iota(jnp.int32, (c_sz, c_sz), 0)
              >= lax.broadcasted_iota(jnp.int32, (c_sz, c_sz), 1))
    rows = lambda c: slice(c * c_sz, (c + 1) * c_sz)
    klanes = lambda h: slice((h // 2) * LANES, (h // 2 + 1) * LANES)
    vlanes = lambda h: slice(h * GLA_DV, (h + 1) * GLA_DV)
    kd, dec = [], []
    for c in chunks:
        bl = bcum[(c + 1) * c_sz - 1:(c + 1) * c_sz]
        kd.append((gk[rows(c)] * jnp.exp(bl - bcum[rows(c)])).astype(BF16))
        dec.append(jnp.exp(bl))
    qg_h, att, upd = {}, {}, {}
    pair = 2
    for c in chunks:
        for h0 in range(0, GLA_HEADS, pair):
            for h in range(h0, h0 + pair):
                own = (lane // GLA_DK) == (h % pair)
                qg_h[c, h] = jnp.where(own, qg[rows(c), klanes(h)], 0.0).astype(BF16)
            both = lax.dot_general(jnp.concatenate([qg_h[c, h0 + e] for e in range(pair)], axis=0),
                                   kg[rows(c), klanes(h0)], NT_DIMS, preferred_element_type=F32)
            for e in range(pair):
                att[c, h0 + e] = both[e * c_sz:(e + 1) * c_sz]
    for c in chunks:
        for h0 in range(0, GLA_HEADS, pair):
            both = lax.dot_general(gv_ref[rows(c), h0 * GLA_DV:(h0 + pair) * GLA_DV], kd[c][:, klanes(h0)], TN_DIMS,
                                   preferred_element_type=F32)
            for e in range(pair):
                upd[c, h0 + e] = both[e * GLA_DV:(e + 1) * GLA_DV]
    st_before = {}
    for h in heads:
        st = st_ref[h]
        for c in chunks:
            st_before[c, h] = st.astype(BF16)
            st = st * dec[c][:, klanes(h)] + upd[c, h]
        st_ref[h] = st
    gn = gn_ref[...]
    for c in chunks:
        for h in heads:
            a_c = jnp.where(causal, att[c, h], 0.0).astype(BF16)
            o = (jnp.dot(a_c, gv_ref[rows(c), vlanes(h)], preferred_element_type=F32)
                 + lax.dot_general(qg_h[c, h], st_before[c, h], NT_DIMS, preferred_element_type=F32))
            y = o * lax.rsqrt(jnp.mean(o * o, axis=-1, keepdims=True) + NORM_EPS) * gn
            gr = gr_ref[rows(c), vlanes(h)].astype(F32)
            o_ref[rows(c), vlanes(h)] = (y * (gr * _sigmoid(gr))).astype(BF16)


def _gla_call(p, wa, ba, gn, B, S, tc):
    nt = S // tc
    row = lambda w: pl.BlockSpec((tc, w), lambda b, t: (b * nt + t, 0))
    full = lambda a: pl.BlockSpec(a.shape, lambda b, t: (0,) * a.ndim)
    kwidth = GLA_HEADS * GLA_DK
    vwidth = GLA_HEADS * GLA_DV
    return pl.pallas_call(
        functools.partial(_gla_kernel, tc=tc),
        grid=(B, nt),
        in_specs=[row(kwidth), row(kwidth), row(vwidth), row(LANES), row(vwidth), full(wa), full(ba), full(gn)],
        out_specs=row(vwidth),
        out_shape=jax.ShapeDtypeStruct((B * S, vwidth), BF16),
        scratch_shapes=[pltpu.VMEM((GLA_HEADS, GLA_DV, LANES), F32)],
        compiler_params=pltpu.CompilerParams(dimension_semantics=("parallel", "arbitrary"),
                                             vmem_limit_bytes=VMEM_LIMIT),
        name="gla",
    )(p["gq"], p["gk"], p["gv"], p["ga"], p["gr"], wa, ba, gn)


def _out_kernel(on_ref, og_ref, x_ref, gpre_ref, wz_ref, wn_ref, wg_ref, wo_ref, gp_ref, o_ref):
    d = x_ref.shape[1]
    nz_w = _MSRC["nz"][1]
    blocks = [slice(r0, r0 + TILE) for r0 in range(0, x_ref.shape[0], TILE)]

    def up(rs):
        x = x_ref[rs, :]
        h = ((x * lax.rsqrt(jnp.mean(x * x, axis=-1, keepdims=True) + NORM_EPS)) * gpre_ref[...]).astype(BF16)
        nz = jnp.dot(h, wz_ref[:, :nz_w], preferred_element_type=F32)
        mg = [jnp.dot(h, wz_ref[:, nz_w + c * d:nz_w + (c + 1) * d], preferred_element_type=F32) for c in range(2)]
        gated = on_ref[rs, :].astype(F32) * (nz * _sigmoid(nz))
        return (jnp.dot(gated.astype(BF16), wn_ref[...], preferred_element_type=F32),
                jnp.dot(og_ref[rs, :], wg_ref[...], preferred_element_type=F32), mg)

    pending = up(blocks[0])
    for k, rs in enumerate(blocks):
        a, b, mg = pending
        if k + 1 < len(blocks):
            pending = up(blocks[k + 1])
        y = _sigmoid(mg[0]) * a + _sigmoid(mg[1]) * b
        out = jnp.dot(y.astype(BF16), wo_ref[...], preferred_element_type=F32)
        r = out * lax.rsqrt(jnp.mean(out * out, axis=-1, keepdims=True) + NORM_EPS)
        o_ref[rs, :] = x_ref[rs, :] + r * gp_ref[...]


def _out_call(o_nsa, o_gla, xf, g_pre, wz, wn, wg, wo, gp, tm):
    n, d = xf.shape
    row = lambda w: pl.BlockSpec((tm, w), lambda i: (i, 0))
    full = lambda a: pl.BlockSpec(a.shape, lambda i: (0,) * a.ndim)
    return pl.pallas_call(
        _out_kernel,
        grid=(n // tm,),
        in_specs=[row(o_nsa.shape[1]), row(o_gla.shape[1]), row(d),
                  full(g_pre), full(wz), full(wn), full(wg), full(wo), full(gp)],
        out_specs=row(d),
        out_shape=jax.ShapeDtypeStruct((n, d), F32),
        compiler_params=pltpu.CompilerParams(dimension_semantics=("parallel",), vmem_limit_bytes=VMEM_LIMIT),
        name="merge_out",
    )(o_nsa, o_gla, xf, g_pre, wz, wn, wg, wo, gp)


def _position_tables(S):
    inv_freq = ROPE_THETA ** (-jnp.arange(0, HEAD_DIM, 2, dtype=F32) / HEAD_DIM)
    step = SLC_BLOCK
    a_hi = (step * jnp.arange(S // step, dtype=F32))[:, None] * inv_freq[None, :]
    a_lo = jnp.arange(step, dtype=F32)[:, None] * inv_freq[None, :]
    c_hi, s_hi, c_lo, s_lo = jnp.cos(a_hi)[:, None, :], jnp.sin(a_hi)[:, None, :], jnp.cos(a_lo)[None], jnp.sin(a_lo)[None]
    cos = (c_hi * c_lo - s_hi * s_lo).reshape(S, HEAD_DIM // 2)
    sin = (s_hi * c_lo + c_hi * s_lo).reshape(S, HEAD_DIM // 2)
    cosf = jnp.concatenate([cos] * 4, axis=1)
    sinf = jnp.concatenate([-sin, sin, -sin, sin], axis=1)
    blk = np.arange(S) // SLC_BLOCK
    noh = np.zeros((S, LANES), np.float32)
    noh[np.arange(S), HEAD_DIM + blk] = -MASK_BIG
    nb, nc = S // SLC_BLOCK, S // CMP_STRIDE
    cs = CMP_STRIDE * np.arange(nc)
    bs = SLC_BLOCK * np.arange(nb)
    ov = (cs[None, :] < bs[:, None] + SLC_BLOCK) & (cs[None, :] + CMP_BLOCK > bs[:, None])
    ov[:, (S - CMP_BLOCK) // CMP_STRIDE + 1:] = False
    return cosf, sinf, jnp.asarray(noh), jnp.asarray(ov.astype(np.float32)).astype(BF16)


def _compress_weights(pos_k, w1_k, w2_k, pos_v, w1_v, w2_v):
    assert NSA_GROUPS == 2

    def block_diag2(w):
        z = jnp.zeros_like(w)
        return jnp.concatenate([jnp.concatenate([w, z], axis=-1), jnp.concatenate([z, w], axis=-1)], axis=-2)

    pad_k = jnp.concatenate([w2_k, jnp.zeros((CMP_HIDDEN, LANES - HEAD_DIM), w2_k.dtype)], axis=1)
    return (jnp.concatenate([pos_k, pos_k], axis=1), jnp.concatenate([pos_v, pos_v], axis=1),
            block_diag2(w1_k).astype(BF16), block_diag2(w1_v).astype(BF16),
            block_diag2(pad_k).astype(BF16),
            block_diag2(w2_v).astype(BF16))


def _layer(xf, tabs, B, S, g_pre, w_in, cmp_pos_k, cmp_w1_k, cmp_w2_k, cmp_pos_v, cmp_w1_v, cmp_w2_v,
           gla_w_a, gla_b_a, gla_g_norm, w_up_nsa, w_up_gla, w_out, g_post):
    cosf, sinf, noh, ovt = tabs
    d = xf.shape[1]
    p = _proj_call(xf, g_pre.reshape(1, d), _pad_proj_weight(w_in), cosf, sinf, noh, S, tm=ROW_BLOCK)

    cmp_w = _compress_weights(cmp_pos_k, cmp_w1_k, cmp_w2_k, cmp_pos_v, cmp_w1_v, cmp_w2_v)
    kc, vct = _cmp_call(p["kcr"], p["vcr"], cmp_w, B, S)
    o_nsa = _nsa_call(p, kc, vct, ovt, B, S, qt=TILE)

    wa = jnp.zeros((LANES, gla_w_a.shape[1]), F32).at[:GLA_RANK].set(gla_w_a)
    o_gla = _gla_call(p, wa, gla_b_a.reshape(1, -1), gla_g_norm.reshape(1, GLA_DV), B, S, tc=ROW_BLOCK)

    return _out_call(o_nsa, o_gla, xf, g_pre.reshape(1, d), _merge_gate_weight(w_in), w_up_nsa.astype(BF16),
                     w_up_gla.astype(BF16), w_out.astype(BF16), g_post.reshape(1, d), tm=ROW_BLOCK)


def kernel(x, g_pre, w_in, cmp_pos_k, cmp_w1_k, cmp_w2_k, cmp_pos_v, cmp_w1_v, cmp_w2_v, gla_w_a, gla_b_a,
           gla_g_norm, w_up_nsa, w_up_gla, w_out, g_post):
    B, S, d = x.shape
    assert d == D_MODEL and S % 2048 == 0 and S // SLC_BLOCK <= LANES - HEAD_DIM
    tabs = _position_tables(S)
    xf = x.reshape(B * S, d)
    for l in range(g_pre.shape[0]):
        xf = _layer(xf, tabs, B, S, g_pre[l], w_in[l], cmp_pos_k[l], cmp_w1_k[l], cmp_w2_k[l],
                    cmp_pos_v[l], cmp_w1_v[l], cmp_w2_v[l], gla_w_a[l], gla_b_a[l], gla_g_norm[l],
                    w_up_nsa[l], w_up_gla[l], w_out[l], g_post[l])
    return xf.reshape(B, S, d)
```
